```python
import math
import jax, jax.numpy as jnp
from jax import lax
import numpy as np

D_MODEL = 1024
BATCH = 8
SEQ = 2048
DEPTH = 1
DEC_BATCH = 128
DEC_SEQ = 1
PAST_LEN = 16384
PAGE_SIZE = 128

HG_HEADS = 8
HG_FDIM = 64
HG_IDIM = 64
HG_W = HG_HEADS * HG_FDIM
HG_IW = HG_HEADS * HG_IDIM
RET_HEADS = 4
RET_DK = 128
RET_DV = 128
RET_QW = RET_HEADS * RET_DK
RET_VW = RET_HEADS * RET_DV
D_FF = 2816
CONV_W = 3
PLE_DIM = 256
CHUNK = 64
ROPE_BASE = 10000.0
EPS = 1e-6
IN_SPLITS = (HG_W, HG_W, HG_IW, HG_IW, RET_QW, RET_QW, RET_VW, RET_VW, D_MODEL, D_MODEL)
IN_COLS = 2 * HG_W + 2 * HG_IW + 2 * RET_QW + 2 * RET_VW + 2 * D_MODEL

kernel_name = 'hybrid_hgrn2_retention_convffn_step'


def _rmsnorm(x, g):
    xf = x.astype(jnp.float32)
    xf = xf * lax.rsqrt(jnp.mean(xf * xf, axis=-1, keepdims=True) + EPS)
    return (xf * g.astype(jnp.float32)).astype(x.dtype)


def _head_rmsnorm(o, n_heads, g):
    b, l, w = o.shape
    oh = o.reshape(b, l, n_heads, w // n_heads)
    oh = oh * lax.rsqrt(jnp.mean(oh * oh, axis=-1, keepdims=True) + EPS)
    return oh.reshape(b, l, w) * g.astype(jnp.float32)


def _split_heads(t, n_heads):
    b, l, w = t.shape
    return t.reshape(b, l, n_heads, w // n_heads).transpose(0, 2, 1, 3)


def _merge_heads(t):
    b, h, l, d = t.shape
    return t.transpose(0, 2, 1, 3).reshape(b, l, h * d)


def _to_chunks(t, c):
    b, h, l, d = t.shape
    return t.reshape(b, h, l // c, c, d).transpose(2, 0, 1, 3, 4)


def _from_chunks(t):
    n, b, h, c, d = t.shape
    return t.transpose(1, 2, 0, 3, 4).reshape(b, h, n * c, d)


def _rotary(x, pos0):
    d = x.shape[-1]
    l = x.shape[-2]
    inv_freq = ROPE_BASE ** (-jnp.arange(0, d, 2, dtype=jnp.float32) / d)
    pos = pos0 + jnp.arange(l, dtype=jnp.float32)
    ang = pos[:, None] * inv_freq[None, :]
    cos, sin = jnp.cos(ang), jnp.sin(ang)
    x1, x2 = x[..., : d // 2], x[..., d // 2:]
    return jnp.concatenate([x1 * cos - x2 * sin, x1 * sin + x2 * cos], axis=-1)


def _hgrn2_chunked(q, k, log_f, v, s0):
    l = q.shape[2]
    c = math.gcd(l, CHUNK)
    causal = jnp.tril(jnp.ones((c, c), dtype=bool))[:, :, None]

    def step(s, xs):
        qc, kc, lfc, vc = xs
        cum = jnp.cumsum(lfc, axis=-2)
        o_inter = jnp.einsum('bhtk,bhkv->bhtv', qc * jnp.exp(cum), s)
        diff = cum[:, :, :, None, :] - cum[:, :, None, :, :]
        decay = jnp.where(causal, jnp.exp(jnp.minimum(diff, 0.0)), 0.0)
        scores = jnp.einsum('bhtsk,bhsk->bhts', qc[:, :, :, None, :] * decay, kc)
        o = o_inter + jnp.einsum('bhts,bhsv->bhtv', scores, vc)
        last = cum[:, :, -1:, :]
        s_new = (jnp.exp(last[:, :, 0, :])[..., None] * s
                 + jnp.einsum('bhsk,bhsv->bhkv', kc * jnp.exp(last - cum), vc))
        return s_new, o

    s_fin, o = lax.scan(step, s0, (_to_chunks(q, c), _to_chunks(k, c),
                                   _to_chunks(log_f, c), _to_chunks(v, c)))
    return _from_chunks(o), s_fin


def _retention_chunked(q, k, v, log_gamma, r0):
    l = q.shape[2]
    c = math.gcd(l, CHUNK)
    t = jnp.arange(c, dtype=jnp.float32)
    delta = t[:, None] - t[None, :]
    decay = jnp.where(delta >= 0, jnp.exp(jnp.maximum(delta, 0.0) * log_gamma[:, None, None]), 0.0)
    q_decay = jnp.exp((t + 1.0)[None, :] * log_gamma[:, None])[:, :, None]
    k_decay = jnp.exp((c - 1.0 - t)[None, :] * log_gamma[:, None])[:, :, None]
    chunk_decay = jnp.exp(c * log_gamma)[:, None, None]

    def step(r, xs):
        qc, kc, vc = xs
        scores = jnp.einsum('bhtd,bhsd->bhts', qc, kc) * decay
        o = (jnp.einsum('bhts,bhsv->bhtv', scores, vc)
             + jnp.einsum('bhtd,bhdv->bhtv', qc * q_decay, r))
        r_new = chunk_decay * r + jnp.einsum('bhsd,bhsv->bhdv', kc * k_decay, vc)
        return r_new, o

    r_fin, o = lax.scan(step, r0, (_to_chunks(q, c), _to_chunks(k, c), _to_chunks(v, c)))
    return _from_chunks(o), r_fin


def _layer(h, p, s_hg0, s_ret0, conv0, pos0, log_gamma, lb,
           w_in, g_mix, g_hg_out, g_ret_out, w_br_hg, w_br_ret, w_out,
           g_ffn, w_up, conv_w, conv_b, w_down, g_ple, w_ple, w_ple_gate):
    f32 = jnp.float32
    l = h.shape[1]
    xn = _rmsnorm(h, g_mix)
    u = xn @ w_in
    split_idx = np.cumsum(IN_SPLITS)[:-1].tolist()
    hq, hf, hi, hg, rq, rk, rv, rg, ga, gb = jnp.split(u, split_idx, axis=-1)

    f = lb + (1.0 - lb) * jax.nn.sigmoid(hf.astype(f32))
    o_hg, s_hg = _hgrn2_chunked(
        _split_heads(jax.nn.silu(hq.astype(f32)), HG_HEADS),
        _split_heads(1.0 - f, HG_HEADS),
        _split_heads(jnp.log(f), HG_HEADS),
        _split_heads(hi.astype(f32), HG_HEADS),
        s_hg0.astype(f32))
    a_hg = _head_rmsnorm(_merge_heads(o_hg), HG_HEADS, g_hg_out) * jax.nn.silu(hg.astype(f32))
    y_hg = a_hg.astype(h.dtype) @ w_br_hg

    q_r = _rotary(_split_heads(rq.astype(f32), RET_HEADS), pos0)
    k_r = _rotary(_split_heads(rk.astype(f32), RET_HEADS), pos0) * (RET_DK ** -0.5)
    o_r, s_ret = _retention_chunked(q_r, k_r, _split_heads(rv.astype(f32), RET_HEADS),
                                    log_gamma, s_ret0.astype(f32))
    a_r = _head_rmsnorm(_merge_heads(o_r), RET_HEADS, g_ret_out) * jax.nn.silu(rg.astype(f32))
    y_ret = a_r.astype(h.dtype) @ w_br_ret

    mixed = jax.nn.sigmoid(ga) * y_hg + jax.nn.sigmoid(gb) * y_ret
    h = h + mixed @ w_out

    xn = _rmsnorm(h, g_ffn)
    a, g = jnp.split(xn @ w_up, 2, axis=-1)
    padded = jnp.concatenate([conv0.astype(a.dtype), a], axis=1)
    ac = conv_b + sum(padded[:, j:j + l, :] * conv_w[j] for j in range(CONV_W))
    conv_new = padded[:, -(CONV_W - 1):, :]
    h = h + (jax.nn.gelu(ac) * g) @ w_down

    gate = jax.nn.sigmoid(_rmsnorm(h, g_ple) @ w_ple_gate)
    h = h + gate * (p @ w_ple)
    return h, s_hg, s_ret, conv_new


def setup_inputs(seed: int = 0) -> dict:
    key = jax.random.key(seed)
    ks = jax.random.split(key, 24)

    def nrm(k, shape, scale):
        return jax.random.normal(k, shape, jnp.float32) * scale

    def gain(k, shape):
        return 1.0 + 0.05 * jax.random.normal(k, shape, jnp.float32)

    return {
        'x_prompt': nrm(ks[0], (BATCH, SEQ, D_MODEL), 1.0),
        'x_sample': nrm(ks[1], (DEC_BATCH, DEC_SEQ, D_MODEL), 1.0),
        'state_hgrn': nrm(ks[2], (DEPTH, DEC_BATCH, HG_HEADS, HG_FDIM, HG_IDIM), 0.5),
        'state_ret': nrm(ks[3], (DEPTH, DEC_BATCH, RET_HEADS, RET_DK, RET_DV), 1.0),
        'state_conv': nrm(ks[4], (DEPTH, DEC_BATCH, CONV_W - 1, D_FF), 1.0),
        'p_prompt': nrm(ks[5], (DEPTH, BATCH, SEQ, PLE_DIM), 1.0),
        'p_sample': nrm(ks[6], (DEPTH, DEC_BATCH, DEC_SEQ, PLE_DIM), 1.0),
        'lb_logits': nrm(ks[7], (DEPTH + 1, HG_W), 0.5),
        'w_in': nrm(ks[8], (DEPTH, D_MODEL, IN_COLS), D_MODEL ** -0.5),
        'g_mix': gain(ks[9], (DEPTH, D_MODEL)),
        'g_hg_out': gain(ks[10], (DEPTH, HG_IW)),
        'g_ret_out': gain(ks[11], (DEPTH, RET_VW)),
        'w_br_hg': nrm(ks[12], (DEPTH, HG_IW, D_MODEL), HG_IW ** -0.5),
        'w_br_ret': nrm(ks[13], (DEPTH, RET_VW, D_MODEL), RET_VW ** -0.5),
        'w_out': nrm(ks[14], (DEPTH, D_MODEL, D_MODEL), D_MODEL ** -0.5),
        'g_ffn': gain(ks[15], (DEPTH, D_MODEL)),
        'w_up': nrm(ks[16], (DEPTH, D_MODEL, 2 * D_FF), D_MODEL ** -0.5),
        'conv_w': nrm(ks[17], (DEPTH, CONV_W, D_FF), CONV_W ** -0.5),
        'conv_b': nrm(ks[18], (DEPTH, D_FF), 0.01),
        'w_down': nrm(ks[19], (DEPTH, D_FF, D_MODEL), D_FF ** -0.5),
        'g_ple': gain(ks[20], (DEPTH, D_MODEL)),
        'w_ple': nrm(ks[21], (DEPTH, PLE_DIM, D_MODEL), PLE_DIM ** -0.5),
        'w_ple_gate': nrm(ks[22], (DEPTH, D_MODEL, D_MODEL), D_MODEL ** -0.5),
        'g_final': gain(ks[23], (D_MODEL,)),
    }


def reference(x_prompt, x_sample, state_hgrn, state_ret, state_conv, p_prompt, p_sample,
              lb_logits, w_in, g_mix, g_hg_out, g_ret_out, w_br_hg, w_br_ret, w_out,
              g_ffn, w_up, conv_w, conv_b, w_down, g_ple, w_ple, w_ple_gate, g_final):
    f32 = jnp.float32
    log_gamma = jnp.log1p(-jnp.exp2(-5.0 - jnp.arange(RET_HEADS, dtype=f32)))
    lb_all = jnp.cumsum(jax.nn.softmax(lb_logits.astype(f32), axis=0), axis=0)

    hp, hs = x_prompt, x_sample
    bp = x_prompt.shape[0]
    hg_p, ret_p, cv_p, hg_s, ret_s, cv_s = [], [], [], [], [], []
    for i in range(DEPTH):
        lw = (log_gamma, lb_all[i], w_in[i], g_mix[i], g_hg_out[i], g_ret_out[i],
              w_br_hg[i], w_br_ret[i], w_out[i], g_ffn[i], w_up[i], conv_w[i], conv_b[i],
              w_down[i], g_ple[i], w_ple[i], w_ple_gate[i])
        hp, s1, s2, s3 = _layer(
            hp, p_prompt[i],
            jnp.zeros((bp, HG_HEADS, HG_FDIM, HG_IDIM), f32),
            jnp.zeros((bp, RET_HEADS, RET_DK, RET_DV), f32),
            jnp.zeros((bp, CONV_W - 1, D_FF), hp.dtype),
            0, *lw)
        hg_p.append(s1); ret_p.append(s2); cv_p.append(s3)
        hs, s1, s2, s3 = _layer(
            hs, p_sample[i], state_hgrn[i], state_ret[i], state_conv[i],
            PAST_LEN, *lw)
        hg_s.append(s1); ret_s.append(s2); cv_s.append(s3)

    y_prompt = _rmsnorm(hp, g_final)
    y_sample = _rmsnorm(hs, g_final)
    return (y_prompt, y_sample,
            jnp.stack(hg_p, axis=0), jnp.stack(ret_p, axis=0), jnp.stack(cv_p, axis=0),
            jnp.stack(hg_s, axis=0), jnp.stack(ret_s, axis=0), jnp.stack(cv_s, axis=0))
```

```python
import functools

import numpy as np
import jax
import jax.numpy as jnp
from jax import lax
from jax.experimental import pallas as pl
from jax.experimental.pallas import tpu as pltpu

D_MODEL = 1024
BATCH = 8
SEQ = 2048
DEPTH = 1
DEC_BATCH = 128
DEC_SEQ = 1
PAST_LEN = 16384
HG_HEADS = 8
HG_FDIM = 64
HG_IDIM = 64
HG_W = HG_HEADS * HG_FDIM
RET_HEADS = 4
RET_DK = 128
RET_DV = 128
RET_W = RET_HEADS * RET_DK
D_FF = 2816
CONV_W = 3
PLE_DIM = 256
CHUNK = 64
ROPE_BASE = 10000.0
EPS = 1e-6

C_HQ, C_HF, C_HI, C_HG = 0, 512, 1024, 1536
C_RQ, C_RK, C_RV, C_RG = 2048, 2560, 3072, 3584
C_GA, C_GB, C_END = 4096, 5120, 6144

N_LEVELS = 6
HG_GROUP = 4
RET_GROUP = 2
MXU_W = 256

MIX_TILE = 256
FFN_TILE = 256
CARRY_ROWS = 8
V7X_VMEM_LIMIT = 56 * 1024 * 1024

F32 = jnp.float32
BF16 = jnp.bfloat16


def _level_tables():
    c = CHUNK
    t = np.arange(c)[:, None]
    r = np.arange(c)[None, :]
    blocks = []
    for l in range(N_LEVELS):
        same = (t >> l) == (r >> l)
        upper = ((t >> l) & 1) == 1
        blocks.append(np.where(upper, same & (r <= t), same & (r > t)))
    blocks.append(r <= t)
    blocks.append(r > t)
    sel = np.concatenate(blocks, axis=0).astype(np.float32)
    sel2 = np.concatenate([sel, sel], axis=1)
    s = r
    level = np.full((c, c), N_LEVELS + 1, np.int32)
    level[np.broadcast_to(s == t, (c, c))] = N_LEVELS
    x = t ^ s
    for l in range(N_LEVELS):
        hit = (s < t) & ((x >> l) == 1)
        level[hit] = l
    return sel2, level


def _block_mask(rows, cols, rblk, cblk):
    r = np.arange(rows)[:, None] // rblk
    c = np.arange(cols)[None, :] // cblk
    return (r == c).astype(np.float32)


def _rope_tables(pos0, length):
    d = RET_DK
    inv_freq = ROPE_BASE ** (-jnp.arange(0, d, 2, dtype=F32) / d)
    pos = pos0 + jnp.arange(length, dtype=F32)
    ang = pos[:, None] * inv_freq[None, :]
    cos, sin = jnp.cos(ang), jnp.sin(ang)
    cos_full = jnp.tile(jnp.concatenate([cos, cos], axis=-1), (1, RET_HEADS))
    sin_signed = jnp.tile(jnp.concatenate([-sin, sin], axis=-1), (1, RET_HEADS))
    return cos_full, sin_signed


def _retention_tables(log_gamma):
    c = CHUNK
    t = jnp.arange(c, dtype=F32)
    delta = t[:, None] - t[None, :]
    decay = jnp.where(delta >= 0, jnp.exp(jnp.maximum(delta, 0.0) * log_gamma[:, None, None]), 0.0)
    dmask = jnp.transpose(decay, (1, 0, 2)).reshape(c, RET_HEADS * c)
    q_decay = jnp.exp((t + 1.0)[None, :] * log_gamma[:, None])
    k_decay = jnp.exp((c - 1.0 - t)[None, :] * log_gamma[:, None])
    qdec = jnp.repeat(q_decay.T, RET_DK, axis=1)
    kdec = jnp.repeat(k_decay.T, RET_DK, axis=1)
    chunk_decay = jnp.exp(c * log_gamma)
    cd = jnp.repeat(chunk_decay, RET_DK).reshape(RET_HEADS // RET_GROUP, MXU_W, 1)
    cd = jnp.broadcast_to(cd, (RET_HEADS // RET_GROUP, MXU_W, MXU_W))
    return dmask, qdec, kdec, cd


def _rms(x, g):
    ms = jnp.mean(x * x, axis=-1, keepdims=True)
    return x * lax.rsqrt(ms + EPS) * g


def _silu(x):
    return x * jax.nn.sigmoid(x)


def _dot(a, b):
    return jnp.dot(a, b, preferred_element_type=F32)


def _dot_nt(a, b):
    return lax.dot_general(a, b, (((1,), (1,)), ((), ())), preferred_element_type=F32)


def _dot_tn(a, b):
    return lax.dot_general(a, b, (((0,), (0,)), ((), ())), preferred_element_type=F32)


def _rotary(x, cos_full, sin_signed):
    swapped = jnp.concatenate(
        [pltpu.roll(x[:, RET_DK * h:RET_DK * (h + 1)], RET_DK // 2, 1) for h in range(RET_HEADS)], axis=1)
    return x * cos_full + swapped * sin_signed


def _hgrn_inputs(hq, hf, lb):
    q = _silu(hq)
    f = lb + (1.0 - lb) * jax.nn.sigmoid(hf)
    return q, f, 1.0 - f


def _merge_branches(x, o_hg, o_ret, u_hg, u_rg, u_gab, hm_hg, hm_ret, g_hg, g_ret, w_br_hg, w_br_ret, w_out):
    ms = _dot((o_hg * o_hg).astype(BF16), hm_hg)
    a_hg = o_hg * lax.rsqrt(ms + EPS) * g_hg * _silu(u_hg)
    y_hg = _dot(a_hg.astype(BF16), w_br_hg)
    ms = _dot((o_ret * o_ret).astype(BF16), hm_ret)
    a_ret = o_ret * lax.rsqrt(ms + EPS) * g_ret * _silu(u_rg)
    y_ret = _dot(a_ret.astype(BF16), w_br_ret)
    mixed = jax.nn.sigmoid(u_gab[:, :D_MODEL]) * y_hg + jax.nn.sigmoid(u_gab[:, D_MODEL:]) * y_ret
    return x + _dot(mixed.astype(BF16), w_out)


def _ffn_tail(h, ac, gate_half, p, w_down, g_ple, w_ple_gate, w_ple, g_final):
    h = h + _dot((jax.nn.gelu(ac) * gate_half).astype(BF16), w_down)
    gate = jax.nn.sigmoid(_dot(_rms(h, g_ple).astype(BF16), w_ple_gate))
    h = h + gate * _dot(p.astype(BF16), w_ple)
    return _rms(h, g_final)


def _mix_prompt_kernel(x_ref, cos_ref, sin_ref, w_in_ref, g_mix_ref, lb_ref, sel_ref, lvl_ref,
                       bd64_ref, bd64f_ref, bdret_ref, bd128f_ref, dmask_ref, qdec_ref, kdec_ref, cd_ref,
                       hm_hg_ref, hm_ret_ref, g_hg_ref, g_ret_ref, w_br_hg_ref, w_br_ret_ref, w_out_ref,
                       h_ref, st_hg_out_ref, st_ret_out_ref,
                       xn_s, q_s, kk_s, lfh_s, lfl_s, v_s, qr_s, kr_s, vr_s, o_hg_s, o_ret_s, st_hg, st_ret):
    t_idx = pl.program_id(1)
    n_hg_groups = HG_HEADS // HG_GROUP
    n_ret_groups = RET_HEADS // RET_GROUP

    @pl.when(t_idx == 0)
    def _():
        st_hg[...] = jnp.zeros_like(st_hg)
        st_ret[...] = jnp.zeros_like(st_ret)

    x = x_ref[...]
    xn = _rms(x, g_mix_ref[...]).astype(BF16)
    xn_s[...] = xn

    u = _dot(xn, w_in_ref[:, C_HQ:C_HG])
    q, f, kk = _hgrn_inputs(u[:, 0:HG_W], u[:, HG_W:2 * HG_W], lb_ref[...])
    q_s[...] = q
    kk_s[...] = kk
    lf = jnp.log(f)
    lf_hi = lf.astype(BF16)
    lfh_s[...] = lf_hi
    lfl_s[...] = (lf - lf_hi.astype(F32)).astype(BF16)
    v_s[...] = u[:, 2 * HG_W:3 * HG_W].astype(BF16)

    u = _dot(xn, w_in_ref[:, C_RQ:C_RG])
    cos_full = cos_ref[...]
    sin_signed = sin_ref[...]
    qr_s[...] = _rotary(u[:, 0:RET_W], cos_full, sin_signed)
    kr_s[...] = _rotary(u[:, RET_W:2 * RET_W], cos_full, sin_signed) * (RET_DK ** -0.5)
    vr_s[...] = u[:, 2 * RET_W:3 * RET_W].astype(BF16)

    def chunk_body(c, carry):
        rows = pl.ds(pl.multiple_of(c * CHUNK, CHUNK), CHUNK)

        q = q_s[rows, :]
        kk = kk_s[rows, :]
        v = v_s[rows, :]
        lf2 = jnp.concatenate([lfh_s[rows, :], lfl_s[rows, :]], axis=0)
        ex = jnp.exp(_dot(sel_ref[...], lf2))
        lvl = lvl_ref[...]
        bd64 = bd64_ref[...]
        bd64f = bd64f_ref[...]
        o_parts = []
        for g in range(n_hg_groups):
            cs = slice(MXU_W * g, MXU_W * (g + 1))
            qg = q[:, cs]
            kg = kk[:, cs]
            scores = jnp.zeros((CHUNK, MXU_W), F32)
            for l in range(N_LEVELS + 1):
                if l < N_LEVELS:
                    e = ex[CHUNK * l:CHUNK * (l + 1), cs]
                    ql = (qg * e).astype(BF16)
                    kl = (kg * e).astype(BF16)
                else:
                    ql = qg.astype(BF16)
                    kl = kg.astype(BF16)
                rhs = jnp.concatenate([kl] * HG_GROUP, axis=0) * bd64
                scores = jnp.where(lvl == l, _dot_nt(ql, rhs), scores)
            v_bd = jnp.concatenate([v[:, cs]] * HG_GROUP, axis=0) * bd64
            o_intra = _dot(scores.astype(BF16), v_bd)
            e_cum = ex[CHUNK * N_LEVELS:CHUNK * (N_LEVELS + 1), cs]
            e_suf = ex[CHUNK * (N_LEVELS + 1):CHUNK * (N_LEVELS + 2), cs]
            st = st_hg[g]
            o_inter = _dot_nt((qg * e_cum).astype(BF16), st.astype(BF16))
            o_parts.append(o_intra + o_inter)
            upd = _dot_tn(v[:, cs], (kg * e_suf).astype(BF16))
            st_hg[g] = st * e_cum[CHUNK - 1:CHUNK, :] + upd * bd64f
        o_hg_s[rows, :] = jnp.concatenate(o_parts, axis=1)

        qr = qr_s[rows, :]
        kr = kr_s[rows, :]
        vr = vr_s[rows, :]
        bdret = bdret_ref[...]
        k_bd = jnp.concatenate([kr.astype(BF16)] * RET_HEADS, axis=0) * bdret
        scores = _dot_nt(qr.astype(BF16), k_bd) * dmask_ref[...]
        v_bd = jnp.concatenate([vr] * RET_HEADS, axis=0) * bdret
        o_ret = _dot(scores.astype(BF16), v_bd)
        qd = (qr * qdec_ref[...]).astype(BF16)
        kd = (kr * kdec_ref[...]).astype(BF16)
        bd128f = bd128f_ref[...]
        o_parts = []
        for g in range(n_ret_groups):
            cs = slice(MXU_W * g, MXU_W * (g + 1))
            st = st_ret[g]
            o_parts.append(_dot(qd[:, cs], st.astype(BF16)))
            upd = _dot_tn(kd[:, cs], vr[:, cs])
            st_ret[g] = cd_ref[g] * st + upd * bd128f
        o_ret_s[rows, :] = o_ret + jnp.concatenate(o_parts, axis=1)
        return carry

    lax.fori_loop(0, MIX_TILE // CHUNK, chunk_body, 0)

    xn = xn_s[...]
    h_ref[...] = _merge_branches(
        x_ref[...], o_hg_s[...], o_ret_s[...],
        _dot(xn, w_in_ref[:, C_HG:C_RQ]), _dot(xn, w_in_ref[:, C_RG:C_GA]), _dot(xn, w_in_ref[:, C_GA:C_END]),
        hm_hg_ref[...], hm_ret_ref[...], g_hg_ref[...], g_ret_ref[...],
        w_br_hg_ref[...], w_br_ret_ref[...], w_out_ref[...])

    @pl.when(t_idx == pl.num_programs(1) - 1)
    def _():
        st_hg_out_ref[...] = st_hg[...]
        st_ret_out_ref[...] = st_ret[...]


def _const_spec(shape):
    zeros = (0,) * len(shape)
    return pl.BlockSpec(shape, lambda *_: zeros, pipeline_mode=pl.Buffered(1))


def _mix_prompt(x, cos_full, sin_signed, w_in, g_mix, lb, consts, hm_hg, hm_ret, g_hg, g_ret,
                w_br_hg, w_br_ret, w_out):
    b, l, d = x.shape
    nt = l // MIX_TILE
    tile = lambda w: pl.BlockSpec((None, MIX_TILE, w), lambda i, j: (i, j, 0))
    table = lambda w: pl.BlockSpec((MIX_TILE, w), lambda i, j: (j, 0))
    n_hg_groups = HG_HEADS // HG_GROUP
    n_ret_groups = RET_HEADS // RET_GROUP
    const_args = (w_in, g_mix, lb) + tuple(consts) + (hm_hg, hm_ret, g_hg, g_ret, w_br_hg, w_br_ret, w_out)
    state_spec = lambda n: pl.BlockSpec((None, n, MXU_W, MXU_W), lambda i, j: (i, 0, 0, 0))
    return pl.pallas_call(
        _mix_prompt_kernel,
        grid=(b, nt),
        in_specs=[tile(d), table(RET_W), table(RET_W)] + [_const_spec(a.shape) for a in const_args],
        out_specs=[tile(d), state_spec(n_hg_groups), state_spec(n_ret_groups)],
        out_shape=[jax.ShapeDtypeStruct((b, l, d), F32),
                   jax.ShapeDtypeStruct((b, n_hg_groups, MXU_W, MXU_W), F32),
                   jax.ShapeDtypeStruct((b, n_ret_groups, MXU_W, MXU_W), F32)],
        scratch_shapes=[
            pltpu.VMEM((MIX_TILE, d), BF16),
            pltpu.VMEM((MIX_TILE, HG_W), F32),
            pltpu.VMEM((MIX_TILE, HG_W), F32),
            pltpu.VMEM((MIX_TILE, HG_W), BF16),
            pltpu.VMEM((MIX_TILE, HG_W), BF16),
            pltpu.VMEM((MIX_TILE, HG_W), BF16),
            pltpu.VMEM((MIX_TILE, RET_W), F32),
            pltpu.VMEM((MIX_TILE, RET_W), F32),
            pltpu.VMEM((MIX_TILE, RET_W), BF16),
            pltpu.VMEM((MIX_TILE, HG_W), F32),
            pltpu.VMEM((MIX_TILE, RET_W), F32),
            pltpu.VMEM((n_hg_groups, MXU_W, MXU_W), F32),
            pltpu.VMEM((n_ret_groups, MXU_W, MXU_W), F32),
        ],
        compiler_params=pltpu.CompilerParams(
            dimension_semantics=("arbitrary", "arbitrary"), vmem_limit_bytes=V7X_VMEM_LIMIT),
        name="mix_prompt",
    )(x, cos_full, sin_signed, *const_args)


def _ffn_prompt_kernel(h_ref, p_ref, g_ffn_ref, w_up_ref, conv_w_ref, conv_b_ref, w_down_ref,
                       g_ple_ref, w_ple_gate_ref, w_ple_ref, g_final_ref,
                       y_ref, conv_out_ref, a_ext):
    t_idx = pl.program_id(1)
    tl = FFN_TILE

    @pl.when(t_idx == 0)
    def _():
        a_ext[0:CARRY_ROWS, :] = jnp.zeros((CARRY_ROWS, D_FF), F32)

    @pl.when(t_idx > 0)
    def _():
        a_ext[0:CARRY_ROWS, :] = a_ext[tl:tl + CARRY_ROWS, :]

    h = h_ref[...]
    xn = _rms(h, g_ffn_ref[...]).astype(BF16)
    a_ext[CARRY_ROWS:CARRY_ROWS + tl, :] = _dot(xn, w_up_ref[:, 0:D_FF])
    gate_half = _dot(xn, w_up_ref[:, D_FF:2 * D_FF])
    cw = conv_w_ref[...]
    ac = conv_b_ref[...]
    for j in range(CONV_W):
        off = CARRY_ROWS - (CONV_W - 1) + j
        ac = ac + a_ext[off:off + tl, :] * cw[j:j + 1, :]
    y_ref[...] = _ffn_tail(h, ac, gate_half, p_ref[...], w_down_ref[...], g_ple_ref[...],
                           w_ple_gate_ref[...], w_ple_ref[...], g_final_ref[...])

    @pl.when(t_idx == pl.num_programs(1) - 1)
    def _():
        conv_out_ref[...] = a_ext[CARRY_ROWS + tl - (CONV_W - 1):CARRY_ROWS + tl, :]


def _ffn_prompt(h, p, g_ffn, w_up, conv_w, conv_b, w_down, g_ple, w_ple_gate, w_ple, g_final):
    b, l, d = h.shape
    nt = l // FFN_TILE
    tile = lambda w: pl.BlockSpec((None, FFN_TILE, w), lambda i, j: (i, j, 0))
    const_args = (g_ffn, w_up, conv_w, conv_b, w_down, g_ple, w_ple_gate, w_ple, g_final)
    return pl.pallas_call(
        _ffn_prompt_kernel,
        grid=(b, nt),
        in_specs=[tile(d), tile(PLE_DIM)] + [_const_spec(a.shape) for a in const_args],
        out_specs=[tile(d), pl.BlockSpec((None, CONV_W - 1, D_FF), lambda i, j: (i, 0, 0))],
        out_shape=[jax.ShapeDtypeStruct((b, l, d), F32),
                   jax.ShapeDtypeStruct((b, CONV_W - 1, D_FF), F32)],
        scratch_shapes=[pltpu.VMEM((FFN_TILE + CARRY_ROWS, D_FF), F32)],
        compiler_params=pltpu.CompilerParams(
            dimension_semantics=("arbitrary", "arbitrary"), vmem_limit_bytes=V7X_VMEM_LIMIT),
        name="ffn_prompt",
    )(h, p, *const_args)


def _proj_sample_kernel(x_ref, cos_ref, sin_ref, w_in_ref, g_mix_ref, lb_ref,
                        f_t_ref, kk_t_ref, q_t_ref, v_t_ref, kr_t_ref, qr_t_ref, vr_t_ref, gates_ref):
    xn = _rms(x_ref[...], g_mix_ref[...]).astype(BF16)
    u = _dot(xn, w_in_ref[:, C_HQ:C_HG])
    q, f, kk = _hgrn_inputs(u[:, 0:HG_W], u[:, HG_W:2 * HG_W], lb_ref[...])
    f_t_ref[...] = f.T
    kk_t_ref[...] = kk.T
    q_t_ref[...] = q.T
    v_t_ref[...] = u[:, 2 * HG_W:3 * HG_W].T
    u = _dot(xn, w_in_ref[:, C_RQ:C_RG])
    cos_full = cos_ref[...]
    sin_signed = sin_ref[...]
    qr_t_ref[...] = _rotary(u[:, 0:RET_W], cos_full, sin_signed).T
    kr_t_ref[...] = (_rotary(u[:, RET_W:2 * RET_W], cos_full, sin_signed) * (RET_DK ** -0.5)).T
    vr_t_ref[...] = u[:, 2 * RET_W:3 * RET_W].T
    gates_ref[:, 0:HG_W] = _dot(xn, w_in_ref[:, C_HG:C_RQ])
    gates_ref[:, HG_W:HG_W + RET_W] = _dot(xn, w_in_ref[:, C_RG:C_GA])
    gates_ref[:, HG_W + RET_W:] = _dot(xn, w_in_ref[:, C_GA:C_END])


def _proj_sample(x, cos_full, sin_signed, w_in, g_mix, lb):
    n = x.shape[0]
    col = jax.ShapeDtypeStruct((HG_W, n), F32)
    return pl.pallas_call(
        _proj_sample_kernel,
        out_shape=[col] * 7 + [jax.ShapeDtypeStruct((n, HG_W + RET_W + 2 * D_MODEL), F32)],
        compiler_params=pltpu.CompilerParams(vmem_limit_bytes=V7X_VMEM_LIMIT),
        name="proj_sample",
    )(x, cos_full, sin_signed, w_in, g_mix, lb)


def _state_kernel(s_ref, a_t_ref, k_t_ref, q_t_ref, v_t_ref, s_out_ref, o_t_ref, xt_s, *, dk, dv, kb):
    j = pl.program_id(0)

    @pl.when(j == 0)
    def _():
        o_t_ref[...] = jnp.zeros_like(o_t_ref)

    xt_s[...] = s_ref[...].T
    row0 = j * kb
    head_row = pl.multiple_of((row0 // dk) * dv, dv)
    v_blk = v_t_ref[pl.ds(head_row, dv), :]

    def body(k, acc):
        r = row0 + k
        sl = pl.ds(pl.multiple_of(k * dv, dv), dv)
        s_new = a_t_ref[pl.ds(r, 1), :] * xt_s[sl, :] + k_t_ref[pl.ds(r, 1), :] * v_blk
        xt_s[sl, :] = s_new
        return acc + q_t_ref[pl.ds(r, 1), :] * s_new

    acc = lax.fori_loop(0, kb, body, jnp.zeros(v_blk.shape, F32))
    o_t_ref[pl.ds(head_row, dv), :] += acc
    s_out_ref[...] = xt_s[...].T


def _state_update(state2d, a_t, k_t, q_t, v_t, *, dk, dv, kb, name):
    n, width = state2d.shape
    cb = kb * dv
    rows_k, rows_v = a_t.shape[0], v_t.shape[0]
    full = lambda r: pl.BlockSpec((r, n), lambda j: (0, 0))
    return pl.pallas_call(
        functools.partial(_state_kernel, dk=dk, dv=dv, kb=kb),
        grid=(width // cb,),
        in_specs=[pl.BlockSpec((n, cb), lambda j: (0, j)), full(rows_k), full(rows_k), full(rows_k), full(rows_v)],
        out_specs=[pl.BlockSpec((n, cb), lambda j: (0, j)), full(rows_v)],
        out_shape=[jax.ShapeDtypeStruct((n, width), F32), jax.ShapeDtypeStruct((rows_v, n), F32)],
        scratch_shapes=[pltpu.VMEM((cb, n), F32)],
        compiler_params=pltpu.CompilerParams(
            dimension_semantics=("arbitrary",), vmem_limit_bytes=V7X_VMEM_LIMIT),
        name=name,
    )(state2d, a_t, k_t, q_t, v_t)


def _tail_sample_kernel(x_ref, o_hg_t_ref, o_ret_t_ref, gates_ref, conv0_ref, p_ref,
                        hm_hg_ref, hm_ret_ref, g_hg_ref, g_ret_ref, w_br_hg_ref, w_br_ret_ref, w_out_ref,
                        g_ffn_ref, w_up_ref, conv_w_ref, conv_b_ref, w_down_ref,
                        g_ple_ref, w_ple_gate_ref, w_ple_ref, g_final_ref,
                        y_ref, conv_out_ref):
    h = _merge_branches(
        x_ref[...], o_hg_t_ref[...].T, o_ret_t_ref[...].T,
        gates_ref[:, 0:HG_W], gates_ref[:, HG_W:HG_W + RET_W], gates_ref[:, HG_W + RET_W:],
        hm_hg_ref[...], hm_ret_ref[...], g_hg_ref[...], g_ret_ref[...],
        w_br_hg_ref[...], w_br_ret_ref[...], w_out_ref[...])
    xn = _rms(h, g_ffn_ref[...]).astype(BF16)
    a = _dot(xn, w_up_ref[:, 0:D_FF])
    gate_half = _dot(xn, w_up_ref[:, D_FF:2 * D_FF])
    cw = conv_w_ref[...]
    prev = conv0_ref[:, D_FF:2 * D_FF]
    ac = conv_b_ref[...] + conv0_ref[:, 0:D_FF] * cw[0:1, :] + prev * cw[1:2, :] + a * cw[2:3, :]
    conv_out_ref[:, 0:D_FF] = prev
    conv_out_ref[:, D_FF:2 * D_FF] = a
    y_ref[...] = _ffn_tail(h, ac, gate_half, p_ref[...], w_down_ref[...], g_ple_ref[...],
                           w_ple_gate_ref[...], w_ple_ref[...], g_final_ref[...])


def _tail_sample(x, o_hg_t, o_ret_t, gates, conv0, p, *weights):
    n = x.shape[0]
    return pl.pallas_call(
        _tail_sample_kernel,
        out_shape=[jax.ShapeDtypeStruct((n, D_MODEL), F32),
                   jax.ShapeDtypeStruct((n, (CONV_W - 1) * D_FF), F32)],
        compiler_params=pltpu.CompilerParams(vmem_limit_bytes=V7X_VMEM_LIMIT),
        name="tail_sample",
    )(x, o_hg_t, o_ret_t, gates, conv0, p, *weights)


def _diag_blocks(st, n_blk, blk):
    b, g = st.shape[:2]
    st = st.reshape(b, g, n_blk, blk, n_blk, blk)
    st = jnp.stack([st[:, :, i, :, i, :] for i in range(n_blk)], axis=2)
    return st.reshape(b, g * n_blk, blk, blk)


def kernel(x_prompt, x_sample, state_hgrn, state_ret, state_conv, p_prompt, p_sample, lb_logits, w_in, g_mix, g_hg_out, g_ret_out, w_br_hg, w_br_ret, w_out, g_ffn, w_up, conv_w, conv_b, w_down, g_ple, w_ple, w_ple_gate, g_final):
    assert DEPTH == 1 and x_prompt.shape == (BATCH, SEQ, D_MODEL) and x_sample.shape == (DEC_BATCH, DEC_SEQ, D_MODEL)
    i = 0
    row = lambda a: a.reshape(1, -1).astype(F32)
    log_gamma = jnp.log1p(-jnp.exp2(-5.0 - jnp.arange(RET_HEADS, dtype=F32)))
    lb = jnp.cumsum(jax.nn.softmax(lb_logits.astype(F32), axis=0), axis=0)[i].reshape(1, HG_W)

    sel2, level = _level_tables()
    consts = (
        jnp.asarray(sel2, BF16),
        jnp.asarray(np.tile(level, (1, HG_GROUP))),
        jnp.asarray(_block_mask(MXU_W, MXU_W, CHUNK, HG_FDIM), BF16),
        jnp.asarray(_block_mask(MXU_W, MXU_W, HG_IDIM, HG_FDIM), F32),
        jnp.asarray(_block_mask(RET_HEADS * CHUNK, RET_W, CHUNK, RET_DK), BF16),
        jnp.asarray(_block_mask(MXU_W, MXU_W, RET_DK, RET_DV), F32),
    ) + _retention_tables(log_gamma)
    hm_hg = jnp.asarray(_block_mask(HG_W, HG_W, HG_IDIM, HG_IDIM) / HG_IDIM, BF16)
    hm_ret = jnp.asarray(_block_mask(RET_W, RET_W, RET_DV, RET_DV) / RET_DV, BF16)

    w_in_b = w_in[i].astype(BF16)
    mix_w = (hm_hg, hm_ret, row(g_hg_out[i]), row(g_ret_out[i]),
             w_br_hg[i].astype(BF16), w_br_ret[i].astype(BF16), w_out[i].astype(BF16))
    ffn_w = (row(g_ffn[i]), w_up[i].astype(BF16), conv_w[i].astype(F32), row(conv_b[i]), w_down[i].astype(BF16),
             row(g_ple[i]), w_ple_gate[i].astype(BF16), w_ple[i].astype(BF16), row(g_final))

    cos_p, sin_p = _rope_tables(0.0, SEQ)
    h_p, st_hg_p, st_ret_p = _mix_prompt(x_prompt, cos_p, sin_p, w_in_b, row(g_mix[i]), lb, consts, *mix_w)
    y_prompt, conv_p = _ffn_prompt(h_p, p_prompt[i], *ffn_w)
    hg_state_p = jnp.swapaxes(_diag_blocks(st_hg_p, HG_GROUP, HG_FDIM), -1, -2)
    ret_state_p = _diag_blocks(st_ret_p, RET_GROUP, RET_DK)

    n = DEC_BATCH
    xs = x_sample.reshape(n, D_MODEL)
    cos_s, sin_s = _rope_tables(float(PAST_LEN), DEC_SEQ)
    f_t, kk_t, q_t, v_t, kr_t, qr_t, vr_t, gates = _proj_sample(xs, cos_s, sin_s, w_in_b, row(g_mix[i]), lb)
    hg_new, o_hg_t = _state_update(
        state_hgrn[i].reshape(n, HG_HEADS * HG_FDIM * HG_IDIM), f_t, kk_t, q_t, v_t,
        dk=HG_FDIM, dv=HG_IDIM, kb=HG_FDIM, name="state_hgrn")
    gamma_t = jnp.broadcast_to(jnp.repeat(jnp.exp(log_gamma), RET_DK)[:, None], (RET_W, n))
    ret_new, o_ret_t = _state_update(
        state_ret[i].reshape(n, RET_HEADS * RET_DK * RET_DV), gamma_t, kr_t, qr_t, vr_t,
        dk=RET_DK, dv=RET_DV, kb=RET_DK // 2, name="state_ret")
    y_sample, conv_s = _tail_sample(
        xs, o_hg_t, o_ret_t, gates, state_conv[i].reshape(n, (CONV_W - 1) * D_FF), p_sample[i].reshape(n, PLE_DIM),
        *mix_w, *ffn_w)

    return (y_prompt, y_sample.reshape(n, DEC_SEQ, D_MODEL),
            hg_state_p[None], ret_state_p[None], conv_p[None],
            hg_new.reshape(1, n, HG_HEADS, HG_FDIM, HG_IDIM),
            ret_new.reshape(1, n, RET_HEADS, RET_DK, RET_DV),
            conv_s.reshape(1, n, CONV_W - 1, D_FF))
```

```python
import functools

import numpy as np
import jax
import jax.numpy as jnp
from jax import lax
from jax.experimental import pallas as pl
from jax.experimental.pallas import tpu as pltpu

D_MODEL = 1024
BATCH = 8
SEQ = 2048
DEPTH = 1
DEC_BATCH = 128
DEC_SEQ = 1
PAST_LEN = 16384
HG_HEADS = 8
HG_FDIM = 64
HG_IDIM = 64
HG_W = HG_HEADS * HG_FDIM
RET_HEADS = 4
RET_DK = 128
RET_DV = 128
RET_W = RET_HEADS * RET_DK
D_FF = 2816
CONV_W = 3
PLE_DIM = 256
CHUNK = 64
ROPE_BASE = 10000.0
EPS = 1e-6

C_HQ, C_HF, C_HI, C_HG = 0, 512, 1024, 1536
C_RQ, C_RK, C_RV, C_RG = 2048, 2560, 3072, 3584
C_GA, C_GB, C_END = 4096, 5120, 6144

N_LEVELS = 6
FAST_DECAY_LIMIT = 80.0
HG_GROUP = 4
RET_GROUP = 2
MXU_W = 256

MIX_TILE = 256
GATE_PIECES = (
    ((C_HG, HG_W, 0), (C_RG, RET_W, HG_W)),
    ((C_GA, 768, HG_W + RET_W),),
    ((C_GA + 768, 768, HG_W + RET_W + 768),),
    ((C_GA + 1536, 512, HG_W + RET_W + 1536),),
)
FFN_TILE = 256
CARRY_ROWS = 8
V7X_VMEM_LIMIT = 56 * 1024 * 1024

F32 = jnp.float32
BF16 = jnp.bfloat16


def _level_tables():
    c = CHUNK
    t = np.arange(c)[:, None]
    r = np.arange(c)[None, :]
    blocks = [r <= t, r > t]
    for l in range(N_LEVELS):
        same = (t >> l) == (r >> l)
        upper = ((t >> l) & 1) == 1
        blocks.append(np.where(upper, same & (r <= t), same & (r > t)))
    sel = np.concatenate(blocks, axis=0).astype(np.float32)
    sel2 = np.concatenate([sel, sel], axis=1)
    s = r
    level = np.full((c, c), N_LEVELS + 1, np.int32)
    level[np.broadcast_to(s == t, (c, c))] = N_LEVELS
    x = t ^ s
    for l in range(N_LEVELS):
        hit = (s < t) & ((x >> l) == 1)
        level[hit] = l
    return sel2, level


def _block_mask(rows, cols, rblk, cblk):
    r = np.arange(rows)[:, None] // rblk
    c = np.arange(cols)[None, :] // cblk
    return (r == c).astype(np.float32)


def _rope_tables(pos0, length):
    d = RET_DK
    inv_freq = ROPE_BASE ** (-jnp.arange(0, d, 2, dtype=F32) / d)
    pos = pos0 + jnp.arange(length, dtype=F32)
    ang = pos[:, None] * inv_freq[None, :]
    cos, sin = jnp.cos(ang), jnp.sin(ang)
    cos_full = jnp.tile(jnp.concatenate([cos, cos], axis=-1), (1, RET_HEADS))
    sin_signed = jnp.tile(jnp.concatenate([-sin, sin], axis=-1), (1, RET_HEADS))
    return cos_full, sin_signed


def _retention_tables(log_gamma):
    c = CHUNK
    t = jnp.arange(c, dtype=F32)
    delta = t[:, None] - t[None, :]
    decay = jnp.where(delta >= 0, jnp.exp(jnp.maximum(delta, 0.0) * log_gamma[:, None, None]), 0.0)
    dmask = jnp.transpose(decay, (1, 0, 2)).reshape(c, RET_HEADS * c)
    q_decay = jnp.exp((t + 1.0)[None, :] * log_gamma[:, None])
    k_decay = jnp.exp((c - 1.0 - t)[None, :] * log_gamma[:, None])
    qdec = jnp.repeat(q_decay.T, RET_DK, axis=1)
    kdec = jnp.repeat(k_decay.T, RET_DK, axis=1)
    chunk_decay = jnp.exp(c * log_gamma)
    cd = jnp.repeat(chunk_decay, RET_DK).reshape(RET_HEADS // RET_GROUP, MXU_W, 1)
    cd = jnp.broadcast_to(cd, (RET_HEADS // RET_GROUP, MXU_W, MXU_W))
    return dmask, qdec, kdec, cd


def _rms(x, g):
    ms = jnp.mean(x * x, axis=-1, keepdims=True)
    return x * lax.rsqrt(ms + EPS) * g


def _silu(x):
    return x * jax.nn.sigmoid(x)


def _dot(a, b):
    return jnp.dot(a, b, preferred_element_type=F32)


def _dot_nt(a, b):
    return lax.dot_general(a, b, (((1,), (1,)), ((), ())), preferred_element_type=F32)


def _dot_tn(a, b):
    return lax.dot_general(a, b, (((0,), (0,)), ((), ())), preferred_element_type=F32)


def _rotary(x, cos_full, sin_signed):
    swapped = jnp.concatenate(
        [pltpu.roll(x[:, RET_DK * h:RET_DK * (h + 1)], RET_DK // 2, 1) for h in range(RET_HEADS)], axis=1)
    return x * cos_full + swapped * sin_signed


def _hgrn_inputs(hq, hf, lb):
    q = _silu(hq)
    f = lb + (1.0 - lb) * jax.nn.sigmoid(hf)
    return q, f, 1.0 - f


def _merge_branches(x, o_hg, o_ret, u_hg, u_rg, u_gab, hm_hg, hm_ret, g_hg, g_ret, w_br_hg, w_br_ret, w_out):
    ms = _dot((o_hg * o_hg).astype(BF16), hm_hg)
    a_hg = o_hg * lax.rsqrt(ms + EPS) * g_hg * _silu(u_hg)
    y_hg = _dot(a_hg.astype(BF16), w_br_hg)
    ms = _dot((o_ret * o_ret).astype(BF16), hm_ret)
    a_ret = o_ret * lax.rsqrt(ms + EPS) * g_ret * _silu(u_rg)
    y_ret = _dot(a_ret.astype(BF16), w_br_ret)
    mixed = jax.nn.sigmoid(u_gab[:, :D_MODEL]) * y_hg + jax.nn.sigmoid(u_gab[:, D_MODEL:]) * y_ret
    return x + _dot(mixed.astype(BF16), w_out)


def _ffn_tail(h, ac, gate_half, p, w_down, g_ple, w_ple_gate, w_ple, g_final):
    h = h + _dot((jax.nn.gelu(ac) * gate_half).astype(BF16), w_down)
    gate = jax.nn.sigmoid(_dot(_rms(h, g_ple).astype(BF16), w_ple_gate))
    h = h + gate * _dot(p.astype(BF16), w_ple)
    return _rms(h, g_final)


def _mix_prompt_kernel(x_ref, cos_ref, sin_ref, w_in_ref, g_mix_ref, lb_ref, sel_ref, lvl_ref,
                       bd64_ref, bd64f_ref, bdret_ref, bd128f_ref, dmask_ref, qdec_ref, kdec_ref, cd_ref,
                       hm_hg_ref, hm_ret_ref, g_hg_ref, g_ret_ref, w_br_hg_ref, w_br_ret_ref, w_out_ref,
                       h_ref, st_hg_out_ref, st_ret_out_ref,
                       xn_s, q_s, kk_s, lfh_s, lfl_s, v_s, qr_s, kr_s, vr_s, o_hg_s, o_ret_s, gate_s, st_hg, st_ret):
    t_idx = pl.program_id(1)
    n_hg_groups = HG_HEADS // HG_GROUP
    n_ret_groups = RET_HEADS // RET_GROUP

    @pl.when(t_idx == 0)
    def _():
        st_hg[...] = jnp.zeros_like(st_hg)
        st_ret[...] = jnp.zeros_like(st_ret)

    x = x_ref[...]
    xn = _rms(x, g_mix_ref[...]).astype(BF16)
    xn_s[...] = xn

    u = _dot(xn, w_in_ref[:, C_HQ:C_HG])
    q, f, kk = _hgrn_inputs(u[:, 0:HG_W], u[:, HG_W:2 * HG_W], lb_ref[...])
    q_s[...] = q
    kk_s[...] = kk
    lf = jnp.log(f)
    lf_hi = lf.astype(BF16)
    lfh_s[...] = lf_hi
    lfl_s[...] = (lf - lf_hi.astype(F32)).astype(BF16)
    v_s[...] = u[:, 2 * HG_W:3 * HG_W].astype(BF16)

    u = _dot(xn, w_in_ref[:, C_RQ:C_RG])
    cos_full = cos_ref[...]
    sin_signed = sin_ref[...]
    qr_s[...] = _rotary(u[:, 0:RET_W], cos_full, sin_signed)
    kr_s[...] = _rotary(u[:, RET_W:2 * RET_W], cos_full, sin_signed) * (RET_DK ** -0.5)
    vr_s[...] = u[:, 2 * RET_W:3 * RET_W].astype(BF16)

    chunk_sums = [jnp.sum(lf[c * CHUNK:(c + 1) * CHUNK, :], axis=0, keepdims=True) for c in range(MIX_TILE // CHUNK)]
    mild = jnp.min(functools.reduce(jnp.minimum, chunk_sums)) >= -FAST_DECAY_LIMIT

    def chunk_body(c, fast):
        rows = slice(c * CHUNK, (c + 1) * CHUNK)

        q = q_s[rows, :]
        kk = kk_s[rows, :]
        v = v_s[rows, :]
        lf2 = jnp.concatenate([lfh_s[rows, :], lfl_s[rows, :]], axis=0)
        x_cs = _dot(sel_ref[0:2 * CHUNK, :], lf2)
        e_cs = jnp.exp(x_cs)
        lvl = lvl_ref[...]
        bd64 = bd64_ref[...]
        bd64f = bd64f_ref[...]
        groups = [slice(MXU_W * g, MXU_W * (g + 1)) for g in range(n_hg_groups)]
        scores_of = []
        if fast:
            e_neg = jnp.exp(-x_cs[0:CHUNK, :])
            for cs in groups:
                qh = (q[:, cs] * e_cs[0:CHUNK, cs]).astype(BF16)
                kh = (kk[:, cs] * e_neg[:, cs]).astype(BF16)
                rhs = jnp.concatenate([kh] * HG_GROUP, axis=0) * bd64
                scores_of.append(jnp.where(lvl <= N_LEVELS, _dot_nt(qh, rhs), 0.0))
        else:
            ex = jnp.exp(_dot(sel_ref[2 * CHUNK:, :], lf2))
            for cs in groups:
                scores = jnp.zeros((CHUNK, MXU_W), F32)
                for l in range(N_LEVELS + 1):
                    if l < N_LEVELS:
                        e = ex[CHUNK * l:CHUNK * (l + 1), cs]
                        ql = (q[:, cs] * e).astype(BF16)
                        kl = (kk[:, cs] * e).astype(BF16)
                    else:
                        ql = q[:, cs].astype(BF16)
                        kl = kk[:, cs].astype(BF16)
                    rhs = jnp.concatenate([kl] * HG_GROUP, axis=0) * bd64
                    scores = jnp.where(lvl == l, _dot_nt(ql, rhs), scores)
                scores_of.append(scores)

        o_parts = []
        for g, cs in enumerate(groups):
            v_bd = jnp.concatenate([v[:, cs]] * HG_GROUP, axis=0) * bd64
            o_intra = _dot(scores_of[g].astype(BF16), v_bd)
            e_cum = e_cs[0:CHUNK, cs]
            e_suf = e_cs[CHUNK:2 * CHUNK, cs]
            st = st_hg[g]
            o_inter = _dot_nt((q[:, cs] * e_cum).astype(BF16), st.astype(BF16))
            o_parts.append(o_intra + o_inter)
            upd = _dot_tn(v[:, cs], (kk[:, cs] * e_suf).astype(BF16))
            st_hg[g] = st * e_cum[CHUNK - 1:CHUNK, :] + upd * bd64f
        o_hg_s[rows, :] = jnp.concatenate(o_parts, axis=1)

        qr = qr_s[rows, :]
        kr = kr_s[rows, :]
        vr = vr_s[rows, :]
        bdret = bdret_ref[...]
        k_bd = jnp.concatenate([kr.astype(BF16)] * RET_HEADS, axis=0) * bdret
        scores = _dot_nt(qr.astype(BF16), k_bd) * dmask_ref[...]
        v_bd = jnp.concatenate([vr] * RET_HEADS, axis=0) * bdret
        o_ret = _dot(scores.astype(BF16), v_bd)
        qd = (qr * qdec_ref[...]).astype(BF16)
        kd = (kr * kdec_ref[...]).astype(BF16)
        bd128f = bd128f_ref[...]
        o_parts = []
        for g in range(n_ret_groups):
            cs = slice(MXU_W * g, MXU_W * (g + 1))
            st = st_ret[g]
            o_parts.append(_dot(qd[:, cs], st.astype(BF16)))
            upd = _dot_tn(kd[:, cs], vr[:, cs])
            st_ret[g] = cd_ref[g] * st + upd * bd128f
        o_ret_s[rows, :] = o_ret + jnp.concatenate(o_parts, axis=1)

    def recurrences(fast):
        for c, pieces in enumerate(GATE_PIECES):
            chunk_body(c, fast)
            for src, width, dst in pieces:
                gate_s[:, dst:dst + width] = _dot(xn_s[...], w_in_ref[:, src:src + width])

    pl.when(mild)(functools.partial(recurrences, True))
    pl.when(jnp.logical_not(mild))(functools.partial(recurrences, False))

    h_ref[...] = _merge_branches(
        x_ref[...], o_hg_s[...], o_ret_s[...],
        gate_s[:, 0:HG_W], gate_s[:, HG_W:HG_W + RET_W], gate_s[:, HG_W + RET_W:],
        hm_hg_ref[...], hm_ret_ref[...], g_hg_ref[...], g_ret_ref[...],
        w_br_hg_ref[...], w_br_ret_ref[...], w_out_ref[...])

    @pl.when(t_idx == pl.num_programs(1) - 1)
    def _():
        st_hg_out_ref[...] = st_hg[...]
        st_ret_out_ref[...] = st_ret[...]


def _const_spec(shape):
    zeros = (0,) * len(shape)
    return pl.BlockSpec(shape, lambda *_: zeros, pipeline_mode=pl.Buffered(1))


def _mix_prompt(x, cos_full, sin_signed, w_in, g_mix, lb, consts, hm_hg, hm_ret, g_hg, g_ret,
                w_br_hg, w_br_ret, w_out):
    b, l, d = x.shape
    nt = l // MIX_TILE
    tile = lambda w: pl.BlockSpec((None, MIX_TILE, w), lambda i, j: (i, j, 0))
    table = lambda w: pl.BlockSpec((MIX_TILE, w), lambda i, j: (j, 0))
    n_hg_groups = HG_HEADS // HG_GROUP
    n_ret_groups = RET_HEADS // RET_GROUP
    const_args = (w_in, g_mix, lb) + tuple(consts) + (hm_hg, hm_ret, g_hg, g_ret, w_br_hg, w_br_ret, w_out)
    state_spec = lambda n: pl.BlockSpec((None, n, MXU_W, MXU_W), lambda i, j: (i, 0, 0, 0))
    return pl.pallas_call(
        _mix_prompt_kernel,
        grid=(b, nt),
        in_specs=[tile(d), table(RET_W), table(RET_W)] + [_const_spec(a.shape) for a in const_args],
        out_specs=[tile(d), state_spec(n_hg_groups), state_spec(n_ret_groups)],
        out_shape=[jax.ShapeDtypeStruct((b, l, d), F32),
                   jax.ShapeDtypeStruct((b, n_hg_groups, MXU_W, MXU_W), F32),
                   jax.ShapeDtypeStruct((b, n_ret_groups, MXU_W, MXU_W), F32)],
        scratch_shapes=[
            pltpu.VMEM((MIX_TILE, d), BF16),
            pltpu.VMEM((MIX_TILE, HG_W), F32),
            pltpu.VMEM((MIX_TILE, HG_W), F32),
            pltpu.VMEM((MIX_TILE, HG_W), BF16),
            pltpu.VMEM((MIX_TILE, HG_W), BF16),
            pltpu.VMEM((MIX_TILE, HG_W), BF16),
            pltpu.VMEM((MIX_TILE, RET_W), F32),
            pltpu.VMEM((MIX_TILE, RET_W), F32),
            pltpu.VMEM((MIX_TILE, RET_W), BF16),
            pltpu.VMEM((MIX_TILE, HG_W), F32),
            pltpu.VMEM((MIX_TILE, RET_W), F32),
            pltpu.VMEM((MIX_TILE, HG_W + RET_W + 2 * D_MODEL), F32),
            pltpu.VMEM((n_hg_groups, MXU_W, MXU_W), F32),
            pltpu.VMEM((n_ret_groups, MXU_W, MXU_W), F32),
        ],
        compiler_params=pltpu.CompilerParams(
            dimension_semantics=("arbitrary", "arbitrary"), vmem_limit_bytes=V7X_VMEM_LIMIT),
        name="mix_prompt",
    )(x, cos_full, sin_signed, *const_args)


def _ffn_prompt_kernel(h_ref, p_ref, g_ffn_ref, w_up_ref, conv_w_ref, conv_b_ref, w_down_ref,
                       g_ple_ref, w_ple_gate_ref, w_ple_ref, g_final_ref,
                       y_ref, conv_out_ref, a_ext):
    t_idx = pl.program_id(1)
    tl = FFN_TILE

    @pl.when(t_idx == 0)
    def _():
        a_ext[0:CARRY_ROWS, :] = jnp.zeros((CARRY_ROWS, D_FF), F32)

    @pl.when(t_idx > 0)
    def _():
        a_ext[0:CARRY_ROWS, :] = a_ext[tl:tl + CARRY_ROWS, :]

    h = h_ref[...]
    xn = _rms(h, g_ffn_ref[...]).astype(BF16)
    a_ext[CARRY_ROWS:CARRY_ROWS + tl, :] = _dot(xn, w_up_ref[:, 0:D_FF])
    gate_half = _dot(xn, w_up_ref[:, D_FF:2 * D_FF])
    cw = conv_w_ref[...]
    ac = conv_b_ref[...]
    for j in range(CONV_W):
        off = CARRY_ROWS - (CONV_W - 1) + j
        ac = ac + a_ext[off:off + tl, :] * cw[j:j + 1, :]
    y_ref[...] = _ffn_tail(h, ac, gate_half, p_ref[...], w_down_ref[...], g_ple_ref[...],
                           w_ple_gate_ref[...], w_ple_ref[...], g_final_ref[...])

    @pl.when(t_idx == pl.num_programs(1) - 1)
    def _():
        conv_out_ref[...] = a_ext[CARRY_ROWS + tl - (CONV_W - 1):CARRY_ROWS + tl, :]


def _ffn_prompt(h, p, g_ffn, w_up, conv_w, conv_b, w_down, g_ple, w_ple_gate, w_ple, g_final):
    b, l, d = h.shape
    nt = l // FFN_TILE
    tile = lambda w: pl.BlockSpec((None, FFN_TILE, w), lambda i, j: (i, j, 0))
    const_args = (g_ffn, w_up, conv_w, conv_b, w_down, g_ple, w_ple_gate, w_ple, g_final)
    return pl.pallas_call(
        _ffn_prompt_kernel,
        grid=(b, nt),
        in_specs=[tile(d), tile(PLE_DIM)] + [_const_spec(a.shape) for a in const_args],
        out_specs=[tile(d), pl.BlockSpec((None, CONV_W - 1, D_FF), lambda i, j: (i, 0, 0))],
        out_shape=[jax.ShapeDtypeStruct((b, l, d), F32),
                   jax.ShapeDtypeStruct((b, CONV_W - 1, D_FF), F32)],
        scratch_shapes=[pltpu.VMEM((FFN_TILE + CARRY_ROWS, D_FF), F32)],
        compiler_params=pltpu.CompilerParams(
            dimension_semantics=("arbitrary", "arbitrary"), vmem_limit_bytes=V7X_VMEM_LIMIT),
        name="ffn_prompt",
    )(h, p, *const_args)


def _proj_sample_kernel(x_ref, cos_ref, sin_ref, w_in_ref, g_mix_ref, lb_ref,
                        f_t_ref, kk_t_ref, q_t_ref, v_t_ref, kr_t_ref, qr_t_ref, vr_t_ref, gates_ref):
    xn = _rms(x_ref[...], g_mix_ref[...]).astype(BF16)
    u = _dot(xn, w_in_ref[:, C_HQ:C_HG])
    q, f, kk = _hgrn_inputs(u[:, 0:HG_W], u[:, HG_W:2 * HG_W], lb_ref[...])
    f_t_ref[...] = f.T
    kk_t_ref[...] = kk.T
    q_t_ref[...] = q.T
    v_t_ref[...] = u[:, 2 * HG_W:3 * HG_W].T
    u = _dot(xn, w_in_ref[:, C_RQ:C_RG])
    cos_full = cos_ref[...]
    sin_signed = sin_ref[...]
    qr_t_ref[...] = _rotary(u[:, 0:RET_W], cos_full, sin_signed).T
    kr_t_ref[...] = (_rotary(u[:, RET_W:2 * RET_W], cos_full, sin_signed) * (RET_DK ** -0.5)).T
    vr_t_ref[...] = u[:, 2 * RET_W:3 * RET_W].T
    gates_ref[:, 0:HG_W] = _dot(xn, w_in_ref[:, C_HG:C_RQ])
    gates_ref[:, HG_W:HG_W + RET_W] = _dot(xn, w_in_ref[:, C_RG:C_GA])
    gates_ref[:, HG_W + RET_W:] = _dot(xn, w_in_ref[:, C_GA:C_END])


def _proj_sample(x, cos_full, sin_signed, w_in, g_mix, lb):
    n = x.shape[0]
    col = jax.ShapeDtypeStruct((HG_W, n), F32)
    return pl.pallas_call(
        _proj_sample_kernel,
        out_shape=[col] * 7 + [jax.ShapeDtypeStruct((n, HG_W + RET_W + 2 * D_MODEL), F32)],
        compiler_params=pltpu.CompilerParams(vmem_limit_bytes=V7X_VMEM_LIMIT),
        name="proj_sample",
    )(x, cos_full, sin_signed, w_in, g_mix, lb)


def _state_kernel(s_ref, a_t_ref, k_t_ref, q_t_ref, v_t_ref, s_out_ref, o_t_ref, xt_s, *, dk, dv, kb):
    j = pl.program_id(0)

    @pl.when(j == 0)
    def _():
        o_t_ref[...] = jnp.zeros_like(o_t_ref)

    xt_s[...] = s_ref[...].T
    row0 = j * kb
    head_row = pl.multiple_of((row0 // dk) * dv, dv)
    v_blk = v_t_ref[pl.ds(head_row, dv), :]

    def body(k, acc):
        r = row0 + k
        sl = pl.ds(pl.multiple_of(k * dv, dv), dv)
        s_new = a_t_ref[pl.ds(r, 1), :] * xt_s[sl, :] + k_t_ref[pl.ds(r, 1), :] * v_blk
        xt_s[sl, :] = s_new
        return acc + q_t_ref[pl.ds(r, 1), :] * s_new

    acc = lax.fori_loop(0, kb, body, jnp.zeros(v_blk.shape, F32))
    o_t_ref[pl.ds(head_row, dv), :] += acc
    s_out_ref[...] = xt_s[...].T


def _state_update(state2d, a_t, k_t, q_t, v_t, *, dk, dv, kb, name):
    n, width = state2d.shape
    cb = kb * dv
    rows_k, rows_v = a_t.shape[0], v_t.shape[0]
    full = lambda r: pl.BlockSpec((r, n), lambda j: (0, 0))
    return pl.pallas_call(
        functools.partial(_state_kernel, dk=dk, dv=dv, kb=kb),
        grid=(width // cb,),
        in_specs=[pl.BlockSpec((n, cb), lambda j: (0, j)), full(rows_k), full(rows_k), full(rows_k), full(rows_v)],
        out_specs=[pl.BlockSpec((n, cb), lambda j: (0, j)), full(rows_v)],
        out_shape=[jax.ShapeDtypeStruct((n, width), F32), jax.ShapeDtypeStruct((rows_v, n), F32)],
        scratch_shapes=[pltpu.VMEM((cb, n), F32)],
        compiler_params=pltpu.CompilerParams(
            dimension_semantics=("arbitrary",), vmem_limit_bytes=V7X_VMEM_LIMIT),
        name=name,
    )(state2d, a_t, k_t, q_t, v_t)


def _tail_sample_kernel(x_ref, o_hg_t_ref, o_ret_t_ref, gates_ref, conv0_ref, p_ref,
                        hm_hg_ref, hm_ret_ref, g_hg_ref, g_ret_ref, w_br_hg_ref, w_br_ret_ref, w_out_ref,
                        g_ffn_ref, w_up_ref, conv_w_ref, conv_b_ref, w_down_ref,
                        g_ple_ref, w_ple_gate_ref, w_ple_ref, g_final_ref,
                        y_ref, conv_out_ref):
    h = _merge_branches(
        x_ref[...], o_hg_t_ref[...].T, o_ret_t_ref[...].T,
        gates_ref[:, 0:HG_W], gates_ref[:, HG_W:HG_W + RET_W], gates_ref[:, HG_W + RET_W:],
        hm_hg_ref[...], hm_ret_ref[...], g_hg_ref[...], g_ret_ref[...],
        w_br_hg_ref[...], w_br_ret_ref[...], w_out_ref[...])
    xn = _rms(h, g_ffn_ref[...]).astype(BF16)
    a = _dot(xn, w_up_ref[:, 0:D_FF])
    gate_half = _dot(xn, w_up_ref[:, D_FF:2 * D_FF])
    cw = conv_w_ref[...]
    prev = conv0_ref[:, D_FF:2 * D_FF]
    ac = conv_b_ref[...] + conv0_ref[:, 0:D_FF] * cw[0:1, :] + prev * cw[1:2, :] + a * cw[2:3, :]
    conv_out_ref[:, 0:D_FF] = prev
    conv_out_ref[:, D_FF:2 * D_FF] = a
    y_ref[...] = _ffn_tail(h, ac, gate_half, p_ref[...], w_down_ref[...], g_ple_ref[...],
                           w_ple_gate_ref[...], w_ple_ref[...], g_final_ref[...])


def _tail_sample(x, o_hg_t, o_ret_t, gates, conv0, p, *weights):
    n = x.shape[0]
    return pl.pallas_call(
        _tail_sample_kernel,
        out_shape=[jax.ShapeDtypeStruct((n, D_MODEL), F32),
                   jax.ShapeDtypeStruct((n, (CONV_W - 1) * D_FF), F32)],
        compiler_params=pltpu.CompilerParams(vmem_limit_bytes=V7X_VMEM_LIMIT),
        name="tail_sample",
    )(x, o_hg_t, o_ret_t, gates, conv0, p, *weights)


def _diag_blocks(st, n_blk, blk):
    b, g = st.shape[:2]
    st = st.reshape(b, g, n_blk, blk, n_blk, blk)
    st = jnp.stack([st[:, :, i, :, i, :] for i in range(n_blk)], axis=2)
    return st.reshape(b, g * n_blk, blk, blk)


def kernel(x_prompt, x_sample, state_hgrn, state_ret, state_conv, p_prompt, p_sample, lb_logits, w_in, g_mix, g_hg_out, g_ret_out, w_br_hg, w_br_ret, w_out, g_ffn, w_up, conv_w, conv_b, w_down, g_ple, w_ple, w_ple_gate, g_final):
    assert DEPTH == 1 and x_prompt.shape == (BATCH, SEQ, D_MODEL) and x_sample.shape == (DEC_BATCH, DEC_SEQ, D_MODEL)
    i = 0
    row = lambda a: a.reshape(1, -1).astype(F32)
    log_gamma = jnp.log1p(-jnp.exp2(-5.0 - jnp.arange(RET_HEADS, dtype=F32)))
    lb = jnp.cumsum(jax.nn.softmax(lb_logits.astype(F32), axis=0), axis=0)[i].reshape(1, HG_W)

    sel2, level = _level_tables()
    consts = (
        jnp.asarray(sel2, BF16),
        jnp.asarray(np.tile(level, (1, HG_GROUP))),
        jnp.asarray(_block_mask(MXU_W, MXU_W, CHUNK, HG_FDIM), BF16),
        jnp.asarray(_block_mask(MXU_W, MXU_W, HG_IDIM, HG_FDIM), F32),
        jnp.asarray(_block_mask(RET_HEADS * CHUNK, RET_W, CHUNK, RET_DK), BF16),
        jnp.asarray(_block_mask(MXU_W, MXU_W, RET_DK, RET_DV), F32),
    ) + _retention_tables(log_gamma)
    hm_hg = jnp.asarray(_block_mask(HG_W, HG_W, HG_IDIM, HG_IDIM) / HG_IDIM, BF16)
    hm_ret = jnp.asarray(_block_mask(RET_W, RET_W, RET_DV, RET_DV) / RET_DV, BF16)

    w_in_b = w_in[i].astype(BF16)
    mix_w = (hm_hg, hm_ret, row(g_hg_out[i]), row(g_ret_out[i]),
             w_br_hg[i].astype(BF16), w_br_ret[i].astype(BF16), w_out[i].astype(BF16))
    ffn_w = (row(g_ffn[i]), w_up[i].astype(BF16), conv_w[i].astype(F32), row(conv_b[i]), w_down[i].astype(BF16),
             row(g_ple[i]), w_ple_gate[i].astype(BF16), w_ple[i].astype(BF16), row(g_final))

    cos_p, sin_p = _rope_tables(0.0, SEQ)
    h_p, st_hg_p, st_ret_p = _mix_prompt(x_prompt, cos_p, sin_p, w_in_b, row(g_mix[i]), lb, consts, *mix_w)
    y_prompt, conv_p = _ffn_prompt(h_p, p_prompt[i], *ffn_w)
    hg_state_p = jnp.swapaxes(_diag_blocks(st_hg_p, HG_GROUP, HG_FDIM), -1, -2)
    ret_state_p = _diag_blocks(st_ret_p, RET_GROUP, RET_DK)

    n = DEC_BATCH
    xs = x_sample.reshape(n, D_MODEL)
    cos_s, sin_s = _rope_tables(float(PAST_LEN), DEC_SEQ)
    f_t, kk_t, q_t, v_t, kr_t, qr_t, vr_t, gates = _proj_sample(xs, cos_s, sin_s, w_in_b, row(g_mix[i]), lb)
    hg_new, o_hg_t = _state_update(
        state_hgrn[i].reshape(n, HG_HEADS * HG_FDIM * HG_IDIM), f_t, kk_t, q_t, v_t,
        dk=HG_FDIM, dv=HG_IDIM, kb=HG_FDIM, name="state_hgrn")
    gamma_t = jnp.broadcast_to(jnp.repeat(jnp.exp(log_gamma), RET_DK)[:, None], (RET_W, n))
    ret_new, o_ret_t = _state_update(
        state_ret[i].reshape(n, RET_HEADS * RET_DK * RET_DV), gamma_t, kr_t, qr_t, vr_t,
        dk=RET_DK, dv=RET_DV, kb=RET_DK // 2, name="state_ret")
    y_sample, conv_s = _tail_sample(
        xs, o_hg_t, o_ret_t, gates, state_conv[i].reshape(n, (CONV_W - 1) * D_FF), p_sample[i].reshape(n, PLE_DIM),
        *mix_w, *ffn_w)

    return (y_prompt, y_sample.reshape(n, DEC_SEQ, D_MODEL),
            hg_state_p[None], ret_state_p[None], conv_p[None],
            hg_new.reshape(1, n, HG_HEADS, HG_FDIM, HG_IDIM),
            ret_new.reshape(1, n, RET_HEADS, RET_DK, RET_DV),
            conv_s.reshape(1, n, CONV_W - 1, D_FF))
```

```python
import functools

import numpy as np
import jax
import jax.numpy as jnp
from jax import lax
from jax.experimental import pallas as pl
from jax.experimental.pallas import tpu as pltpu

D_MODEL = 1024
BATCH = 8
SEQ = 2048
DEPTH = 1
DEC_BATCH = 128
DEC_SEQ = 1
PAST_LEN = 16384
HG_HEADS = 8
HG_FDIM = 64
HG_IDIM = 64
HG_W = HG_HEADS * HG_FDIM
RET_HEADS = 4
RET_DK = 128
RET_DV = 128
RET_W = RET_HEADS * RET_DK
D_FF = 2816
CONV_W = 3
PLE_DIM = 256
CHUNK = 64
ROPE_BASE = 10000.0
EPS = 1e-6

C_HQ, C_HF, C_HI, C_HG = 0, 512, 1024, 1536
C_RQ, C_RK, C_RV, C_RG = 2048, 2560, 3072, 3584
C_GA, C_GB, C_END = 4096, 5120, 6144

N_LEVELS = 6
FAST_DECAY_LIMIT = 80.0
HG_GROUP = 4
RET_GROUP = 2
MXU_W = 256
SUBLANES = 8

MIX_TILE = 256
GATE_PIECES = (
    ((C_HG, HG_W, 0), (C_RG, RET_W, HG_W)),
    ((C_GA, 768, HG_W + RET_W),),
    ((C_GA + 768, 768, HG_W + RET_W + 768),),
    ((C_GA + 1536, 512, HG_W + RET_W + 1536),),
)
FFN_TILE = 512
FFN_SUB = 256
CARRY_ROWS = 8
V7X_VMEM_LIMIT = 56 * 1024 * 1024

F32 = jnp.float32
BF16 = jnp.bfloat16


def _level_tables():
    c = CHUNK
    t = np.arange(c)[:, None]
    r = np.arange(c)[None, :]
    blocks = [r <= t, r > t]
    for l in range(N_LEVELS):
        same = (t >> l) == (r >> l)
        upper = ((t >> l) & 1) == 1
        blocks.append(np.where(upper, same & (r <= t), same & (r > t)))
    sel = np.concatenate(blocks, axis=0).astype(np.float32)
    sel2 = np.concatenate([sel, sel], axis=1)
    s = r
    level = np.full((c, c), N_LEVELS + 1, np.int32)
    level[np.broadcast_to(s == t, (c, c))] = N_LEVELS
    x = t ^ s
    for l in range(N_LEVELS):
        hit = (s < t) & ((x >> l) == 1)
        level[hit] = l
    return sel2, level


def _block_mask(rows, cols, rblk, cblk):
    r = np.arange(rows)[:, None] // rblk
    c = np.arange(cols)[None, :] // cblk
    return (r == c).astype(np.float32)


def _rope_tables(pos0, length):
    d = RET_DK
    inv_freq = ROPE_BASE ** (-jnp.arange(0, d, 2, dtype=F32) / d)
    pos = pos0 + jnp.arange(length, dtype=F32)
    ang = pos[:, None] * inv_freq[None, :]
    cos, sin = jnp.cos(ang), jnp.sin(ang)
    cos_full = jnp.tile(jnp.concatenate([cos, cos], axis=-1), (1, RET_HEADS))
    sin_signed = jnp.tile(jnp.concatenate([-sin, sin], axis=-1), (1, RET_HEADS))
    return cos_full, sin_signed


def _retention_tables(log_gamma):
    c = CHUNK
    t = jnp.arange(c, dtype=F32)
    delta = t[:, None] - t[None, :]
    decay = jnp.where(delta >= 0, jnp.exp(jnp.maximum(delta, 0.0) * log_gamma[:, None, None]), 0.0)
    dmask = jnp.transpose(decay, (1, 0, 2)).reshape(c, RET_HEADS * c)
    q_decay = jnp.exp((t + 1.0)[None, :] * log_gamma[:, None])
    k_decay = jnp.exp((c - 1.0 - t)[None, :] * log_gamma[:, None])
    qdec = jnp.repeat(q_decay.T, RET_DK, axis=1)
    kdec = jnp.repeat(k_decay.T, RET_DK, axis=1)
    chunk_decay = jnp.exp(c * log_gamma)
    cd = jnp.repeat(chunk_decay, RET_DK).reshape(RET_HEADS // RET_GROUP, MXU_W, 1)
    cd = jnp.broadcast_to(cd, (RET_HEADS // RET_GROUP, MXU_W, MXU_W))
    return dmask, qdec, kdec, cd


def _rms(x, g):
    ms = jnp.mean(x * x, axis=-1, keepdims=True)
    return x * lax.rsqrt(ms + EPS) * g


def _silu(x):
    return x * jax.nn.sigmoid(x)


def _dot(a, b):
    return jnp.dot(a, b, preferred_element_type=F32)


def _dot_nt(a, b):
    return lax.dot_general(a, b, (((1,), (1,)), ((), ())), preferred_element_type=F32)


def _dot_tn(a, b):
    return lax.dot_general(a, b, (((0,), (0,)), ((), ())), preferred_element_type=F32)


def _rotary(x, cos_full, sin_signed):
    swapped = jnp.concatenate(
        [pltpu.roll(x[:, RET_DK * h:RET_DK * (h + 1)], RET_DK // 2, 1) for h in range(RET_HEADS)], axis=1)
    return x * cos_full + swapped * sin_signed


def _hgrn_inputs(hq, hf, lb):
    q = _silu(hq)
    f = lb + (1.0 - lb) * jax.nn.sigmoid(hf)
    return q, f, 1.0 - f


def _merge_branches(x, o_hg, o_ret, u_hg, u_rg, u_gab, hm_hg, hm_ret, g_hg, g_ret, w_br_hg, w_br_ret, w_out):
    ms = _dot((o_hg * o_hg).astype(BF16), hm_hg)
    a_hg = o_hg * lax.rsqrt(ms + EPS) * g_hg * _silu(u_hg)
    y_hg = _dot(a_hg.astype(BF16), w_br_hg)
    ms = _dot((o_ret * o_ret).astype(BF16), hm_ret)
    a_ret = o_ret * lax.rsqrt(ms + EPS) * g_ret * _silu(u_rg)
    y_ret = _dot(a_ret.astype(BF16), w_br_ret)
    mixed = jax.nn.sigmoid(u_gab[:, :D_MODEL]) * y_hg + jax.nn.sigmoid(u_gab[:, D_MODEL:]) * y_ret
    return x + _dot(mixed.astype(BF16), w_out)


def _ffn_tail(h, ac, gate_half, p, w_down, g_ple, w_ple_gate, w_ple, g_final):
    h = h + _dot((jax.nn.gelu(ac) * gate_half).astype(BF16), w_down)
    gate = jax.nn.sigmoid(_dot(_rms(h, g_ple).astype(BF16), w_ple_gate))
    h = h + gate * _dot(p.astype(BF16), w_ple)
    return _rms(h, g_final)


def _mix_prompt_kernel(x_ref, cos_ref, sin_ref, w_in_ref, g_mix_ref, lb_ref, sel_ref, lvl_ref,
                       bd64_ref, bd64f_ref, bdret_ref, bd128f_ref, dmask_ref, qdec_ref, kdec_ref, cd_ref,
                       hm_hg_ref, hm_ret_ref, g_hg_ref, g_ret_ref, w_br_hg_ref, w_br_ret_ref, w_out_ref,
                       h_ref, st_hg_out_ref, st_ret_out_ref,
                       xn_s, q_s, kk_s, lfh_s, lfl_s, v_s, qr_s, kr_s, vr_s, o_hg_s, o_ret_s, gate_s, st_hg, st_ret):
    t_idx = pl.program_id(1)
    n_hg_groups = HG_HEADS // HG_GROUP
    n_ret_groups = RET_HEADS // RET_GROUP

    @pl.when(t_idx == 0)
    def _():
        st_hg[...] = jnp.zeros_like(st_hg)
        st_ret[...] = jnp.zeros_like(st_ret)

    x = x_ref[...]
    xn = _rms(x, g_mix_ref[...]).astype(BF16)
    xn_s[...] = xn

    u = _dot(xn, w_in_ref[:, C_HQ:C_HG])
    q, f, kk = _hgrn_inputs(u[:, 0:HG_W], u[:, HG_W:2 * HG_W], lb_ref[...])
    q_s[...] = q
    kk_s[...] = kk
    lf = jnp.log(f)
    lf_hi = lf.astype(BF16)
    lfh_s[...] = lf_hi
    lfl_s[...] = (lf - lf_hi.astype(F32)).astype(BF16)
    v_s[...] = u[:, 2 * HG_W:3 * HG_W].astype(BF16)

    u = _dot(xn, w_in_ref[:, C_RQ:C_RG])
    cos_full = cos_ref[...]
    sin_signed = sin_ref[...]
    qr_s[...] = _rotary(u[:, 0:RET_W], cos_full, sin_signed)
    kr_s[...] = _rotary(u[:, RET_W:2 * RET_W], cos_full, sin_signed) * (RET_DK ** -0.5)
    vr_s[...] = u[:, 2 * RET_W:3 * RET_W].astype(BF16)

    chunk_sums = [jnp.sum(lf[c * CHUNK:(c + 1) * CHUNK, :], axis=0, keepdims=True) for c in range(MIX_TILE // CHUNK)]
    mild = jnp.min(functools.reduce(jnp.minimum, chunk_sums)) >= -FAST_DECAY_LIMIT

    def chunk_body(c, fast):
        rows = slice(c * CHUNK, (c + 1) * CHUNK)

        q = q_s[rows, :]
        kk = kk_s[rows, :]
        v = v_s[rows, :]
        lf2 = jnp.concatenate([lfh_s[rows, :], lfl_s[rows, :]], axis=0)
        x_cs = _dot(sel_ref[0:2 * CHUNK, :], lf2)
        e_cs = jnp.exp(x_cs)
        lvl = lvl_ref[...]
        bd64 = bd64_ref[...]
        bd64f = bd64f_ref[...]
        groups = [slice(MXU_W * g, MXU_W * (g + 1)) for g in range(n_hg_groups)]
        scores_of = []
        if fast:
            e_neg = jnp.exp(-x_cs[0:CHUNK, :])
            for cs in groups:
                qh = (q[:, cs] * e_cs[0:CHUNK, cs]).astype(BF16)
                kh = (kk[:, cs] * e_neg[:, cs]).astype(BF16)
                rhs = jnp.concatenate([kh] * HG_GROUP, axis=0) * bd64
                scores_of.append(jnp.where(lvl <= N_LEVELS, _dot_nt(qh, rhs), 0.0))
        else:
            ex = jnp.exp(_dot(sel_ref[2 * CHUNK:, :], lf2))
            for cs in groups:
                scores = jnp.zeros((CHUNK, MXU_W), F32)
                for l in range(N_LEVELS + 1):
                    if l < N_LEVELS:
                        e = ex[CHUNK * l:CHUNK * (l + 1), cs]
                        ql = (q[:, cs] * e).astype(BF16)
                        kl = (kk[:, cs] * e).astype(BF16)
                    else:
                        ql = q[:, cs].astype(BF16)
                        kl = kk[:, cs].astype(BF16)
                    rhs = jnp.concatenate([kl] * HG_GROUP, axis=0) * bd64
                    scores = jnp.where(lvl == l, _dot_nt(ql, rhs), scores)
                scores_of.append(scores)

        o_parts = []
        for g, cs in enumerate(groups):
            v_bd = jnp.concatenate([v[:, cs]] * HG_GROUP, axis=0) * bd64
            o_intra = _dot(scores_of[g].astype(BF16), v_bd)
            e_cum = e_cs[0:CHUNK, cs]
            e_suf = e_cs[CHUNK:2 * CHUNK, cs]
            st = st_hg[g]
            o_inter = _dot_nt((q[:, cs] * e_cum).astype(BF16), st.astype(BF16))
            o_parts.append(o_intra + o_inter)
            upd = _dot_tn(v[:, cs], (kk[:, cs] * e_suf).astype(BF16))
            st_hg[g] = st * e_cum[CHUNK - 1:CHUNK, :] + upd * bd64f
        o_hg_s[rows, :] = jnp.concatenate(o_parts, axis=1)

        qr = qr_s[rows, :]
        kr = kr_s[rows, :]
        vr = vr_s[rows, :]
        bdret = bdret_ref[...]
        k_bd = jnp.concatenate([kr.astype(BF16)] * RET_HEADS, axis=0) * bdret
        scores = _dot_nt(qr.astype(BF16), k_bd) * dmask_ref[...]
        v_bd = jnp.concatenate([vr] * RET_HEADS, axis=0) * bdret
        o_ret = _dot(scores.astype(BF16), v_bd)
        qd = (qr * qdec_ref[...]).astype(BF16)
        kd = (kr * kdec_ref[...]).astype(BF16)
        bd128f = bd128f_ref[...]
        o_parts = []
        for g in range(n_ret_groups):
            cs = slice(MXU_W * g, MXU_W * (g + 1))
            st = st_ret[g]
            o_parts.append(_dot(qd[:, cs], st.astype(BF16)))
            upd = _dot_tn(kd[:, cs], vr[:, cs])
            st_ret[g] = cd_ref[g] * st + upd * bd128f
        o_ret_s[rows, :] = o_ret + jnp.concatenate(o_parts, axis=1)

    def recurrences(fast):
        for c, pieces in enumerate(GATE_PIECES):
            chunk_body(c, fast)
            for src, width, dst in pieces:
                gate_s[:, dst:dst + width] = _dot(xn_s[...], w_in_ref[:, src:src + width])

    pl.when(mild)(functools.partial(recurrences, True))
    pl.when(jnp.logical_not(mild))(functools.partial(recurrences, False))

    h_ref[...] = _merge_branches(
        x_ref[...], o_hg_s[...], o_ret_s[...],
        gate_s[:, 0:HG_W], gate_s[:, HG_W:HG_W + RET_W], gate_s[:, HG_W + RET_W:],
        hm_hg_ref[...], hm_ret_ref[...], g_hg_ref[...], g_ret_ref[...],
        w_br_hg_ref[...], w_br_ret_ref[...], w_out_ref[...])

    @pl.when(t_idx == pl.num_programs(1) - 1)
    def _():
        st_hg_out_ref[...] = st_hg[...]
        st_ret_out_ref[...] = st_ret[...]


def _const_spec(shape):
    zeros = (0,) * len(shape)
    return pl.BlockSpec(shape, lambda *_: zeros, pipeline_mode=pl.Buffered(1))


def _mix_prompt(x, cos_full, sin_signed, w_in, g_mix, lb, consts, hm_hg, hm_ret, g_hg, g_ret,
                w_br_hg, w_br_ret, w_out):
    b, l, d = x.shape
    nt = l // MIX_TILE
    tile = lambda w: pl.BlockSpec((None, MIX_TILE, w), lambda i, j: (i, j, 0))
    table = lambda w: pl.BlockSpec((MIX_TILE, w), lambda i, j: (j, 0))
    n_hg_groups = HG_HEADS // HG_GROUP
    n_ret_groups = RET_HEADS // RET_GROUP
    const_args = (w_in, g_mix, lb) + tuple(consts) + (hm_hg, hm_ret, g_hg, g_ret, w_br_hg, w_br_ret, w_out)
    state_spec = lambda n: pl.BlockSpec((None, n, MXU_W, MXU_W), lambda i, j: (i, 0, 0, 0))
    return pl.pallas_call(
        _mix_prompt_kernel,
        grid=(b, nt),
        in_specs=[tile(d), table(RET_W), table(RET_W)] + [_const_spec(a.shape) for a in const_args],
        out_specs=[tile(d), state_spec(n_hg_groups), state_spec(n_ret_groups)],
        out_shape=[jax.ShapeDtypeStruct((b, l, d), F32),
                   jax.ShapeDtypeStruct((b, n_hg_groups, MXU_W, MXU_W), F32),
                   jax.ShapeDtypeStruct((b, n_ret_groups, MXU_W, MXU_W), F32)],
        scratch_shapes=[
            pltpu.VMEM((MIX_TILE, d), BF16),
            pltpu.VMEM((MIX_TILE, HG_W), F32),
            pltpu.VMEM((MIX_TILE, HG_W), F32),
            pltpu.VMEM((MIX_TILE, HG_W), BF16),
            pltpu.VMEM((MIX_TILE, HG_W), BF16),
            pltpu.VMEM((MIX_TILE, HG_W), BF16),
            pltpu.VMEM((MIX_TILE, RET_W), F32),
            pltpu.VMEM((MIX_TILE, RET_W), F32),
            pltpu.VMEM((MIX_TILE, RET_W), BF16),
            pltpu.VMEM((MIX_TILE, HG_W), F32),
            pltpu.VMEM((MIX_TILE, RET_W), F32),
            pltpu.VMEM((MIX_TILE, HG_W + RET_W + 2 * D_MODEL), F32),
            pltpu.VMEM((n_hg_groups, MXU_W, MXU_W), F32),
            pltpu.VMEM((n_ret_groups, MXU_W, MXU_W), F32),
        ],
        compiler_params=pltpu.CompilerParams(
            dimension_semantics=("arbitrary", "arbitrary"), vmem_limit_bytes=V7X_VMEM_LIMIT),
        name="mix_prompt",
    )(x, cos_full, sin_signed, *const_args)


def _ffn_prompt_kernel(h_ref, p_ref, g_ffn_ref, w_up_ref, conv_w_ref, conv_b_ref, w_down_ref,
                       g_ple_ref, w_ple_gate_ref, w_ple_ref, g_final_ref,
                       y_ref, conv_out_ref, a_ext, gate_s):
    t_idx = pl.program_id(1)
    tl = FFN_TILE

    @pl.when(t_idx == 0)
    def _():
        a_ext[0:CARRY_ROWS, :] = jnp.zeros((CARRY_ROWS, D_FF), F32)

    @pl.when(t_idx > 0)
    def _():
        a_ext[0:CARRY_ROWS, :] = a_ext[tl:tl + CARRY_ROWS, :]

    subs = [slice(i * FFN_SUB, (i + 1) * FFN_SUB) for i in range(tl // FFN_SUB)]
    for r in subs:
        xn = _rms(h_ref[r, :], g_ffn_ref[...]).astype(BF16)
        a_ext[CARRY_ROWS + r.start:CARRY_ROWS + r.stop, :] = _dot(xn, w_up_ref[:, 0:D_FF])
        gate_s[r, :] = _dot(xn, w_up_ref[:, D_FF:2 * D_FF])
    cw = conv_w_ref[...]
    for r in subs:
        ac = conv_b_ref[...]
        for j in range(CONV_W):
            off = CARRY_ROWS - (CONV_W - 1) + j
            ac = ac + a_ext[off + r.start:off + r.stop, :] * cw[j:j + 1, :]
        y_ref[r, :] = _ffn_tail(h_ref[r, :], ac, gate_s[r, :], p_ref[r, :], w_down_ref[...], g_ple_ref[...],
                                w_ple_gate_ref[...], w_ple_ref[...], g_final_ref[...])

    @pl.when(t_idx == pl.num_programs(1) - 1)
    def _():
        conv_out_ref[...] = a_ext[CARRY_ROWS + tl - (CONV_W - 1):CARRY_ROWS + tl, :]


def _ffn_prompt(h, p, g_ffn, w_up, conv_w, conv_b, w_down, g_ple, w_ple_gate, w_ple, g_final):
    b, l, d = h.shape
    nt = l // FFN_TILE
    tile = lambda w: pl.BlockSpec((None, FFN_TILE, w), lambda i, j: (i, j, 0))
    const_args = (g_ffn, w_up, conv_w, conv_b, w_down, g_ple, w_ple_gate, w_ple, g_final)
    return pl.pallas_call(
        _ffn_prompt_kernel,
        grid=(b, nt),
        in_specs=[tile(d), tile(PLE_DIM)] + [_const_spec(a.shape) for a in const_args],
        out_specs=[tile(d), pl.BlockSpec((None, CONV_W - 1, D_FF), lambda i, j: (i, 0, 0))],
        out_shape=[jax.ShapeDtypeStruct((b, l, d), F32),
                   jax.ShapeDtypeStruct((b, CONV_W - 1, D_FF), F32)],
        scratch_shapes=[pltpu.VMEM((FFN_TILE + CARRY_ROWS, D_FF), F32), pltpu.VMEM((FFN_TILE, D_FF), F32)],
        compiler_params=pltpu.CompilerParams(
            dimension_semantics=("arbitrary", "arbitrary"), vmem_limit_bytes=V7X_VMEM_LIMIT),
        name="ffn_prompt",
    )(h, p, *const_args)


def _proj_sample_kernel(x_ref, cos_ref, sin_ref, w_in_ref, g_mix_ref, lb_ref,
                        f_ref, kk_ref, q_ref, v_ref, kr_ref, qr_ref, vr_ref, gates_ref):
    xn = _rms(x_ref[...], g_mix_ref[...]).astype(BF16)
    u = _dot(xn, w_in_ref[:, C_HQ:C_HG])
    q, f, kk = _hgrn_inputs(u[:, 0:HG_W], u[:, HG_W:2 * HG_W], lb_ref[...])
    f_ref[...] = f
    kk_ref[...] = kk
    q_ref[...] = q
    v_ref[...] = u[:, 2 * HG_W:3 * HG_W]
    u = _dot(xn, w_in_ref[:, C_RQ:C_RG])
    cos_full = cos_ref[...]
    sin_signed = sin_ref[...]
    qr_ref[...] = _rotary(u[:, 0:RET_W], cos_full, sin_signed)
    kr_ref[...] = _rotary(u[:, RET_W:2 * RET_W], cos_full, sin_signed) * (RET_DK ** -0.5)
    vr_ref[...] = u[:, 2 * RET_W:3 * RET_W]
    gates_ref[:, 0:HG_W] = _dot(xn, w_in_ref[:, C_HG:C_RQ])
    gates_ref[:, HG_W:HG_W + RET_W] = _dot(xn, w_in_ref[:, C_RG:C_GA])
    gates_ref[:, HG_W + RET_W:] = _dot(xn, w_in_ref[:, C_GA:C_END])


def _proj_sample(x, cos_full, sin_signed, w_in, g_mix, lb):
    n = x.shape[0]
    col = jax.ShapeDtypeStruct((n, HG_W), F32)
    return pl.pallas_call(
        _proj_sample_kernel,
        out_shape=[col] * 7 + [jax.ShapeDtypeStruct((n, HG_W + RET_W + 2 * D_MODEL), F32)],
        compiler_params=pltpu.CompilerParams(vmem_limit_bytes=V7X_VMEM_LIMIT),
        name="proj_sample",
    )(x, cos_full, sin_signed, w_in, g_mix, lb)


def _state_kernel(s_ref, a_ref, k_ref, q_ref, v_ref, s_out_ref, o_ref, *, dk, dv, kb):
    j = pl.program_id(0)
    n = s_ref.shape[0]

    @pl.when(j == 0)
    def _():
        o_ref[...] = jnp.zeros_like(o_ref)

    group0 = j * (kb // SUBLANES)
    head_row = pl.multiple_of((j * kb // dk) * dv, dv)
    v_blk = v_ref[pl.ds(head_row, dv), :]
    acc = jnp.zeros(v_blk.shape, F32)
    for g in range(kb // SUBLANES):
        rows = slice(SUBLANES * g, SUBLANES * (g + 1))
        s_t = s_ref[:, rows, :].reshape(n * SUBLANES, dv).T
        s_new = a_ref[pl.ds(group0 + g, 1), :] * s_t + k_ref[pl.ds(group0 + g, 1), :] * v_blk
        s_out_ref[:, rows, :] = s_new.T.reshape(n, SUBLANES, dv)
        acc = acc + q_ref[pl.ds(group0 + g, 1), :] * s_new
    o_ref[pl.ds(head_row, dv), :] += acc


def _state_update(state3d, a_il, k_il, q_il, v_rep, *, dk, dv, kb, name):
    n, rows, _ = state3d.shape
    full = lambda a: pl.BlockSpec(a.shape, lambda j: (0, 0))
    slab = pl.BlockSpec((n, kb, dv), lambda j: (0, j, 0))
    return pl.pallas_call(
        functools.partial(_state_kernel, dk=dk, dv=dv, kb=kb),
        grid=(rows // kb,),
        in_specs=[slab, full(a_il), full(k_il), full(q_il), full(v_rep)],
        out_specs=[slab, full(v_rep)],
        out_shape=[jax.ShapeDtypeStruct(state3d.shape, F32), jax.ShapeDtypeStruct(v_rep.shape, F32)],
        compiler_params=pltpu.CompilerParams(
            dimension_semantics=("arbitrary",), vmem_limit_bytes=V7X_VMEM_LIMIT),
        name=name,
    )(state3d, a_il, k_il, q_il, v_rep)


def _lane_group_sum(o_ref, sel):
    o = o_ref[...]
    hi = o.astype(BF16)
    lo = (o - hi.astype(F32)).astype(BF16)
    return _dot_nt(sel, hi) + _dot_nt(sel, lo)


def _tail_sample_kernel(x_ref, o_hg_ref, o_ret_ref, sel_ref, gates_ref, conv0_ref, p_ref,
                        hm_hg_ref, hm_ret_ref, g_hg_ref, g_ret_ref, w_br_hg_ref, w_br_ret_ref, w_out_ref,
                        g_ffn_ref, w_up_ref, conv_w_ref, conv_b_ref, w_down_ref,
                        g_ple_ref, w_ple_gate_ref, w_ple_ref, g_final_ref,
                        y_ref, conv_out_ref):
    h = _merge_branches(
        x_ref[...], _lane_group_sum(o_hg_ref, sel_ref[...]), _lane_group_sum(o_ret_ref, sel_ref[...]),
        gates_ref[:, 0:HG_W], gates_ref[:, HG_W:HG_W + RET_W], gates_ref[:, HG_W + RET_W:],
        hm_hg_ref[...], hm_ret_ref[...], g_hg_ref[...], g_ret_ref[...],
        w_br_hg_ref[...], w_br_ret_ref[...], w_out_ref[...])
    xn = _rms(h, g_ffn_ref[...]).astype(BF16)
    a = _dot(xn, w_up_ref[:, 0:D_FF])
    gate_half = _dot(xn, w_up_ref[:, D_FF:2 * D_FF])
    cw = conv_w_ref[...]
    prev = conv0_ref[:, D_FF:2 * D_FF]
    ac = conv_b_ref[...] + conv0_ref[:, 0:D_FF] * cw[0:1, :] + prev * cw[1:2, :] + a * cw[2:3, :]
    conv_out_ref[:, 0:D_FF] = prev
    conv_out_ref[:, D_FF:2 * D_FF] = a
    y_ref[...] = _ffn_tail(h, ac, gate_half, p_ref[...], w_down_ref[...], g_ple_ref[...],
                           w_ple_gate_ref[...], w_ple_ref[...], g_final_ref[...])


def _tail_sample(x, o_hg, o_ret, sel, gates, conv0, p, *weights):
    n = x.shape[0]
    return pl.pallas_call(
        _tail_sample_kernel,
        out_shape=[jax.ShapeDtypeStruct((n, D_MODEL), F32),
                   jax.ShapeDtypeStruct((n, (CONV_W - 1) * D_FF), F32)],
        compiler_params=pltpu.CompilerParams(vmem_limit_bytes=V7X_VMEM_LIMIT),
        name="tail_sample",
    )(x, o_hg, o_ret, sel, gates, conv0, p, *weights)


def _diag_blocks(st, n_blk, blk):
    b, g = st.shape[:2]
    st = st.reshape(b, g, n_blk, blk, n_blk, blk)
    st = jnp.stack([st[:, :, i, :, i, :] for i in range(n_blk)], axis=2)
    return st.reshape(b, g * n_blk, blk, blk)


def kernel(x_prompt, x_sample, state_hgrn, state_ret, state_conv, p_prompt, p_sample, lb_logits, w_in, g_mix, g_hg_out, g_ret_out, w_br_hg, w_br_ret, w_out, g_ffn, w_up, conv_w, conv_b, w_down, g_ple, w_ple, w_ple_gate, g_final):
    assert DEPTH == 1 and x_prompt.shape == (BATCH, SEQ, D_MODEL) and x_sample.shape == (DEC_BATCH, DEC_SEQ, D_MODEL)
    i = 0
    row = lambda a: a.reshape(1, -1).astype(F32)
    log_gamma = jnp.log1p(-jnp.exp2(-5.0 - jnp.arange(RET_HEADS, dtype=F32)))
    lb = jnp.cumsum(jax.nn.softmax(lb_logits.astype(F32), axis=0), axis=0)[i].reshape(1, HG_W)

    sel2, level = _level_tables()
    consts = (
        jnp.asarray(sel2, BF16),
        jnp.asarray(np.tile(level, (1, HG_GROUP))),
        jnp.asarray(_block_mask(MXU_W, MXU_W, CHUNK, HG_FDIM), BF16),
        jnp.asarray(_block_mask(MXU_W, MXU_W, HG_IDIM, HG_FDIM), F32),
        jnp.asarray(_block_mask(RET_HEADS * CHUNK, RET_W, CHUNK, RET_DK), BF16),
        jnp.asarray(_block_mask(MXU_W, MXU_W, RET_DK, RET_DV), F32),
    ) + _retention_tables(log_gamma)
    hm_hg = jnp.asarray(_block_mask(HG_W, HG_W, HG_IDIM, HG_IDIM) / HG_IDIM, BF16)
    hm_ret = jnp.asarray(_block_mask(RET_W, RET_W, RET_DV, RET_DV) / RET_DV, BF16)

    w_in_b = w_in[i].astype(BF16)
    mix_w = (hm_hg, hm_ret, row(g_hg_out[i]), row(g_ret_out[i]),
             w_br_hg[i].astype(BF16), w_br_ret[i].astype(BF16), w_out[i].astype(BF16))
    ffn_w = (row(g_ffn[i]), w_up[i].astype(BF16), conv_w[i].astype(F32), row(conv_b[i]), w_down[i].astype(BF16),
             row(g_ple[i]), w_ple_gate[i].astype(BF16), w_ple[i].astype(BF16), row(g_final))

    cos_p, sin_p = _rope_tables(0.0, SEQ)
    h_p, st_hg_p, st_ret_p = _mix_prompt(x_prompt, cos_p, sin_p, w_in_b, row(g_mix[i]), lb, consts, *mix_w)
    y_prompt, conv_p = _ffn_prompt(h_p, p_prompt[i], *ffn_w)
    hg_state_p = jnp.swapaxes(_diag_blocks(st_hg_p, HG_GROUP, HG_FDIM), -1, -2)
    ret_state_p = _diag_blocks(st_ret_p, RET_GROUP, RET_DK)

    n = DEC_BATCH
    xs = x_sample.reshape(n, D_MODEL)
    cos_s, sin_s = _rope_tables(float(PAST_LEN), DEC_SEQ)
    f, kk, q, v, kr, qr, vr, gates = _proj_sample(xs, cos_s, sin_s, w_in_b, row(g_mix[i]), lb)
    interleave = lambda a: a.reshape(n, -1, SUBLANES).transpose(1, 0, 2).reshape(-1, n * SUBLANES)
    repeat = lambda a: jnp.repeat(a.T, SUBLANES, axis=1)
    hg_new, o_hg = _state_update(
        state_hgrn[i].reshape(n, HG_HEADS * HG_FDIM, HG_IDIM), interleave(f), interleave(kk), interleave(q), repeat(v),
        dk=HG_FDIM, dv=HG_IDIM, kb=HG_FDIM, name="state_hgrn")
    gamma = jnp.broadcast_to(jnp.repeat(jnp.exp(log_gamma), RET_DK // SUBLANES)[:, None],
                             (RET_W // SUBLANES, n * SUBLANES))
    ret_new, o_ret = _state_update(
        state_ret[i].reshape(n, RET_HEADS * RET_DK, RET_DV), gamma, interleave(kr), interleave(qr), repeat(vr),
        dk=RET_DK, dv=RET_DV, kb=RET_DK // 2, name="state_ret")
    sel = jnp.asarray(_block_mask(n, n * SUBLANES, 1, SUBLANES), BF16)
    y_sample, conv_s = _tail_sample(
        xs, o_hg, o_ret, sel, gates, state_conv[i].reshape(n, (CONV_W - 1) * D_FF), p_sample[i].reshape(n, PLE_DIM),
        *mix_w, *ffn_w)

    return (y_prompt, y_sample.reshape(n, DEC_SEQ, D_MODEL),
            hg_state_p[None], ret_state_p[None], conv_p[None],
            hg_new.reshape(1, n, HG_HEADS, HG_FDIM, HG_IDIM),
            ret_new.reshape(1, n, RET_HEADS, RET_DK, RET_DV),
            conv_s.reshape(1, n, CONV_W - 1, D_FF))
```

```python
import functools

import numpy as np
import jax
import jax.numpy as jnp
from jax import lax
from jax.experimental import pallas as pl
from jax.experimental.pallas import tpu as pltpu

D_MODEL = 1024
BATCH = 8
SEQ = 2048
DEPTH = 1
DEC_BATCH = 128
DEC_SEQ = 1
PAST_LEN = 16384
HG_HEADS = 8
HG_FDIM = 64
HG_IDIM = 64
HG_W = HG_HEADS * HG_FDIM
RET_HEADS = 4
RET_DK = 128
RET_DV = 128
RET_W = RET_HEADS * RET_DK
D_FF = 2816
CONV_W = 3
PLE_DIM = 256
CHUNK = 64
ROPE_BASE = 10000.0
EPS = 1e-6

C_HQ, C_HF, C_HI, C_HG = 0, 512, 1024, 1536
C_RQ, C_RK, C_RV, C_RG = 2048, 2560, 3072, 3584
C_GA, C_GB, C_END = 4096, 5120, 6144

N_LEVELS = 6
FAST_DECAY_LIMIT = 80.0
HG_GROUP = 4
RET_GROUP = 2
MXU_W = 256
SUBLANES = 8

MIX_TILE = 256
GATE_PIECES = (
    ((C_HG, HG_W, 0), (C_RG, RET_W, HG_W)),
    ((C_GA, 768, HG_W + RET_W),),
    ((C_GA + 768, 768, HG_W + RET_W + 768),),
    ((C_GA + 1536, 512, HG_W + RET_W + 1536),),
)
FFN_TILE = 512
FFN_SUB = 256
CARRY_ROWS = 8
V7X_VMEM_LIMIT = 56 * 1024 * 1024

F32 = jnp.float32
BF16 = jnp.bfloat16


def _level_tables():
    c = CHUNK
    t = np.arange(c)[:, None]
    r = np.arange(c)[None, :]
    blocks = [r <= t, r > t]
    for l in range(N_LEVELS):
        same = (t >> l) == (r >> l)
        upper = ((t >> l) & 1) == 1
        blocks.append(np.where(upper, same & (r <= t), same & (r > t)))
    sel = np.concatenate(blocks, axis=0).astype(np.float32)
    sel2 = np.concatenate([sel, sel], axis=1)
    s = r
    level = np.full((c, c), N_LEVELS + 1, np.int32)
    level[np.broadcast_to(s == t, (c, c))] = N_LEVELS
    x = t ^ s
    for l in range(N_LEVELS):
        hit = (s < t) & ((x >> l) == 1)
        level[hit] = l
    return sel2, level


def _block_mask(rows, cols, rblk, cblk):
    r = np.arange(rows)[:, None] // rblk
    c = np.arange(cols)[None, :] // cblk
    return (r == c).astype(np.float32)


def _rope_tables(pos0, length):
    d = RET_DK
    inv_freq = ROPE_BASE ** (-jnp.arange(0, d, 2, dtype=F32) / d)
    pos = pos0 + jnp.arange(length, dtype=F32)
    ang = pos[:, None] * inv_freq[None, :]
    cos, sin = jnp.cos(ang), jnp.sin(ang)
    cos_full = jnp.tile(jnp.concatenate([cos, cos], axis=-1), (1, RET_HEADS))
    sin_signed = jnp.tile(jnp.concatenate([-sin, sin], axis=-1), (1, RET_HEADS))
    return cos_full, sin_signed


def _retention_tables(log_gamma):
    c = CHUNK
    t = jnp.arange(c, dtype=F32)
    delta = t[:, None] - t[None, :]
    decay = jnp.where(delta >= 0, jnp.exp(jnp.maximum(delta, 0.0) * log_gamma[:, None, None]), 0.0)
    dmask = jnp.transpose(decay, (1, 0, 2)).reshape(c, RET_HEADS * c)
    q_decay = jnp.exp((t + 1.0)[None, :] * log_gamma[:, None])
    k_decay = jnp.exp((c - 1.0 - t)[None, :] * log_gamma[:, None])
    qdec = jnp.repeat(q_decay.T, RET_DK, axis=1)
    kdec = jnp.repeat(k_decay.T, RET_DK, axis=1)
    chunk_decay = jnp.exp(c * log_gamma)
    cd = jnp.repeat(chunk_decay, RET_DK).reshape(RET_HEADS // RET_GROUP, MXU_W, 1)
    cd = jnp.broadcast_to(cd, (RET_HEADS // RET_GROUP, MXU_W, MXU_W))
    return dmask, qdec, kdec, cd


def _rms(x, g):
    ms = jnp.mean(x * x, axis=-1, keepdims=True)
    return x * lax.rsqrt(ms + EPS) * g


def _silu(x):
    return x * jax.nn.sigmoid(x)


def _dot(a, b):
    return jnp.dot(a, b, preferred_element_type=F32)


def _dot_nt(a, b):
    return lax.dot_general(a, b, (((1,), (1,)), ((), ())), preferred_element_type=F32)


def _dot_tn(a, b):
    return lax.dot_general(a, b, (((0,), (0,)), ((), ())), preferred_element_type=F32)


def _rotary(x, cos_full, sin_signed):
    swapped = jnp.concatenate(
        [pltpu.roll(x[:, RET_DK * h:RET_DK * (h + 1)], RET_DK // 2, 1) for h in range(RET_HEADS)], axis=1)
    return x * cos_full + swapped * sin_signed


def _hgrn_inputs(hq, hf, lb):
    q = _silu(hq)
    f = lb + (1.0 - lb) * jax.nn.sigmoid(hf)
    return q, f, 1.0 - f


def _merge_branches(x, o_hg, o_ret, u_hg, u_rg, u_gab, hm_hg, hm_ret, g_hg, g_ret, w_br_hg, w_br_ret, w_out):
    ms = _dot((o_hg * o_hg).astype(BF16), hm_hg)
    a_hg = o_hg * lax.rsqrt(ms + EPS) * g_hg * _silu(u_hg)
    y_hg = _dot(a_hg.astype(BF16), w_br_hg)
    ms = _dot((o_ret * o_ret).astype(BF16), hm_ret)
    a_ret = o_ret * lax.rsqrt(ms + EPS) * g_ret * _silu(u_rg)
    y_ret = _dot(a_ret.astype(BF16), w_br_ret)
    mixed = jax.nn.sigmoid(u_gab[:, :D_MODEL]) * y_hg + jax.nn.sigmoid(u_gab[:, D_MODEL:]) * y_ret
    return x + _dot(mixed.astype(BF16), w_out)


def _ffn_tail(h, ac, gate_half, p, w_down, g_ple, w_ple_gate, w_ple, g_final):
    h = h + _dot((jax.nn.gelu(ac) * gate_half).astype(BF16), w_down)
    gate = jax.nn.sigmoid(_dot(_rms(h, g_ple).astype(BF16), w_ple_gate))
    h = h + gate * _dot(p.astype(BF16), w_ple)
    return _rms(h, g_final)


def _mix_prompt_kernel(x_ref, cos_ref, sin_ref, w_in_ref, g_mix_ref, lb_ref, sel_ref, lvl_ref,
                       bd64_ref, bd64f_ref, bdret_ref, bd128f_ref, dmask_ref, qdec_ref, kdec_ref, cd_ref,
                       hm_hg_ref, hm_ret_ref, g_hg_ref, g_ret_ref, w_br_hg_ref, w_br_ret_ref, w_out_ref,
                       h_ref, st_hg_out_ref, st_ret_out_ref,
                       xn_s, q_s, kk_s, lfh_s, lfl_s, v_s, qr_s, kr_s, vr_s, o_hg_s, o_ret_s, gate_s, st_hg, st_ret):
    t_idx = pl.program_id(1)
    n_hg_groups = HG_HEADS // HG_GROUP
    n_ret_groups = RET_HEADS // RET_GROUP

    @pl.when(t_idx == 0)
    def _():
        st_hg[...] = jnp.zeros_like(st_hg)
        st_ret[...] = jnp.zeros_like(st_ret)

    x = x_ref[...]
    xn = _rms(x, g_mix_ref[...]).astype(BF16)
    xn_s[...] = xn

    u = _dot(xn, w_in_ref[:, C_HQ:C_HG])
    q, f, kk = _hgrn_inputs(u[:, 0:HG_W], u[:, HG_W:2 * HG_W], lb_ref[...])
    q_s[...] = q
    kk_s[...] = kk
    lf = jnp.log(f)
    lf_hi = lf.astype(BF16)
    lfh_s[...] = lf_hi
    lfl_s[...] = (lf - lf_hi.astype(F32)).astype(BF16)
    v_s[...] = u[:, 2 * HG_W:3 * HG_W].astype(BF16)

    u = _dot(xn, w_in_ref[:, C_RQ:C_RG])
    cos_full = cos_ref[...]
    sin_signed = sin_ref[...]
    qr_s[...] = _rotary(u[:, 0:RET_W], cos_full, sin_signed)
    kr_s[...] = _rotary(u[:, RET_W:2 * RET_W], cos_full, sin_signed) * (RET_DK ** -0.5)
    vr_s[...] = u[:, 2 * RET_W:3 * RET_W].astype(BF16)

    chunk_sums = [jnp.sum(lf[c * CHUNK:(c + 1) * CHUNK, :], axis=0, keepdims=True) for c in range(MIX_TILE // CHUNK)]
    mild = jnp.min(functools.reduce(jnp.minimum, chunk_sums)) >= -FAST_DECAY_LIMIT

    def chunk_body(c, fast):
        rows = slice(c * CHUNK, (c + 1) * CHUNK)

        q = q_s[rows, :]
        kk = kk_s[rows, :]
        v = v_s[rows, :]
        lf2 = jnp.concatenate([lfh_s[rows, :], lfl_s[rows, :]], axis=0)
        x_cs = _dot(sel_ref[0:2 * CHUNK, :], lf2)
        e_cs = jnp.exp(x_cs)
        lvl = lvl_ref[...]
        bd64 = bd64_ref[...]
        bd64f = bd64f_ref[...]
        groups = [slice(MXU_W * g, MXU_W * (g + 1)) for g in range(n_hg_groups)]
        scores_of = []
        if fast:
            e_neg = jnp.exp(-x_cs[0:CHUNK, :])
            for cs in groups:
                qh = (q[:, cs] * e_cs[0:CHUNK, cs]).astype(BF16)
                kh = (kk[:, cs] * e_neg[:, cs]).astype(BF16)
                rhs = jnp.concatenate([kh] * HG_GROUP, axis=0) * bd64
                scores_of.append(jnp.where(lvl <= N_LEVELS, _dot_nt(qh, rhs), 0.0))
        else:
            ex = jnp.exp(_dot(sel_ref[2 * CHUNK:, :], lf2))
            for cs in groups:
                scores = jnp.zeros((CHUNK, MXU_W), F32)
                for l in range(N_LEVELS + 1):
                    if l < N_LEVELS:
                        e = ex[CHUNK * l:CHUNK * (l + 1), cs]
                        ql = (q[:, cs] * e).astype(BF16)
                        kl = (kk[:, cs] * e).astype(BF16)
                    else:
                        ql = q[:, cs].astype(BF16)
                        kl = kk[:, cs].astype(BF16)
                    rhs = jnp.concatenate([kl] * HG_GROUP, axis=0) * bd64
                    scores = jnp.where(lvl == l, _dot_nt(ql, rhs), scores)
                scores_of.append(scores)

        o_parts = []
        for g, cs in enumerate(groups):
            v_bd = jnp.concatenate([v[:, cs]] * HG_GROUP, axis=0) * bd64
            o_intra = _dot(scores_of[g].astype(BF16), v_bd)
            e_cum = e_cs[0:CHUNK, cs]
            e_suf = e_cs[CHUNK:2 * CHUNK, cs]
            st = st_hg[g]
            o_inter = _dot_nt((q[:, cs] * e_cum).astype(BF16), st.astype(BF16))
            o_parts.append(o_intra + o_inter)
            upd = _dot_tn(v[:, cs], (kk[:, cs] * e_suf).astype(BF16))
            st_hg[g] = st * e_cum[CHUNK - 1:CHUNK, :] + upd * bd64f
        o_hg_s[rows, :] = jnp.concatenate(o_parts, axis=1)

        qr = qr_s[rows, :]
        kr = kr_s[rows, :]
        vr = vr_s[rows, :]
        bdret = bdret_ref[...]
        k_bd = jnp.concatenate([kr.astype(BF16)] * RET_HEADS, axis=0) * bdret
        scores = _dot_nt(qr.astype(BF16), k_bd) * dmask_ref[...]
        v_bd = jnp.concatenate([vr] * RET_HEADS, axis=0) * bdret
        o_ret = _dot(scores.astype(BF16), v_bd)
        qd = (qr * qdec_ref[...]).astype(BF16)
        kd = (kr * kdec_ref[...]).astype(BF16)
        bd128f = bd128f_ref[...]
        o_parts = []
        for g in range(n_ret_groups):
            cs = slice(MXU_W * g, MXU_W * (g + 1))
            st = st_ret[g]
            o_parts.append(_dot(qd[:, cs], st.astype(BF16)))
            upd = _dot_tn(kd[:, cs], vr[:, cs])
            st_ret[g] = cd_ref[g] * st + upd * bd128f
        o_ret_s[rows, :] = o_ret + jnp.concatenate(o_parts, axis=1)

    def recurrences(fast):
        for c, pieces in enumerate(GATE_PIECES):
            chunk_body(c, fast)
            for src, width, dst in pieces:
                gate_s[:, dst:dst + width] = _dot(xn_s[...], w_in_ref[:, src:src + width])

    pl.when(mild)(functools.partial(recurrences, True))
    pl.when(jnp.logical_not(mild))(functools.partial(recurrences, False))

    h_ref[...] = _merge_branches(
        x_ref[...], o_hg_s[...], o_ret_s[...],
        gate_s[:, 0:HG_W], gate_s[:, HG_W:HG_W + RET_W], gate_s[:, HG_W + RET_W:],
        hm_hg_ref[...], hm_ret_ref[...], g_hg_ref[...], g_ret_ref[...],
        w_br_hg_ref[...], w_br_ret_ref[...], w_out_ref[...])

    @pl.when(t_idx == pl.num_programs(1) - 1)
    def _():
        for g in range(n_hg_groups):
            st_t = st_hg[g].T
            for h in range(HG_GROUP):
                blk = slice(HG_FDIM * h, HG_FDIM * (h + 1))
                st_hg_out_ref[HG_GROUP * g + h] = st_t[blk, blk]
        for g in range(n_ret_groups):
            for h in range(RET_GROUP):
                blk = slice(RET_DK * h, RET_DK * (h + 1))
                st_ret_out_ref[RET_GROUP * g + h] = st_ret[g, blk, blk]


def _const_spec(shape):
    zeros = (0,) * len(shape)
    return pl.BlockSpec(shape, lambda *_: zeros, pipeline_mode=pl.Buffered(1))


def _mix_prompt(x, cos_full, sin_signed, w_in, g_mix, lb, consts, hm_hg, hm_ret, g_hg, g_ret,
                w_br_hg, w_br_ret, w_out):
    b, l, d = x.shape
    nt = l // MIX_TILE
    tile = lambda w: pl.BlockSpec((None, MIX_TILE, w), lambda i, j: (i, j, 0))
    table = lambda w: pl.BlockSpec((MIX_TILE, w), lambda i, j: (j, 0))
    n_hg_groups = HG_HEADS // HG_GROUP
    n_ret_groups = RET_HEADS // RET_GROUP
    const_args = (w_in, g_mix, lb) + tuple(consts) + (hm_hg, hm_ret, g_hg, g_ret, w_br_hg, w_br_ret, w_out)
    state_spec = lambda h, dk, dv: pl.BlockSpec((None, None, h, dk, dv), lambda i, j: (0, i, 0, 0, 0))
    return pl.pallas_call(
        _mix_prompt_kernel,
        grid=(b, nt),
        in_specs=[tile(d), table(RET_W), table(RET_W)] + [_const_spec(a.shape) for a in const_args],
        out_specs=[tile(d), state_spec(HG_HEADS, HG_FDIM, HG_IDIM), state_spec(RET_HEADS, RET_DK, RET_DV)],
        out_shape=[jax.ShapeDtypeStruct((b, l, d), F32),
                   jax.ShapeDtypeStruct((1, b, HG_HEADS, HG_FDIM, HG_IDIM), F32),
                   jax.ShapeDtypeStruct((1, b, RET_HEADS, RET_DK, RET_DV), F32)],
        scratch_shapes=[
            pltpu.VMEM((MIX_TILE, d), BF16),
            pltpu.VMEM((MIX_TILE, HG_W), F32),
            pltpu.VMEM((MIX_TILE, HG_W), F32),
            pltpu.VMEM((MIX_TILE, HG_W), BF16),
            pltpu.VMEM((MIX_TILE, HG_W), BF16),
            pltpu.VMEM((MIX_TILE, HG_W), BF16),
            pltpu.VMEM((MIX_TILE, RET_W), F32),
            pltpu.VMEM((MIX_TILE, RET_W), F32),
            pltpu.VMEM((MIX_TILE, RET_W), BF16),
            pltpu.VMEM((MIX_TILE, HG_W), F32),
            pltpu.VMEM((MIX_TILE, RET_W), F32),
            pltpu.VMEM((MIX_TILE, HG_W + RET_W + 2 * D_MODEL), F32),
            pltpu.VMEM((n_hg_groups, MXU_W, MXU_W), F32),
            pltpu.VMEM((n_ret_groups, MXU_W, MXU_W), F32),
        ],
        compiler_params=pltpu.CompilerParams(
            dimension_semantics=("arbitrary", "arbitrary"), vmem_limit_bytes=V7X_VMEM_LIMIT),
        name="mix_prompt",
    )(x, cos_full, sin_signed, *const_args)


def _ffn_prompt_kernel(h_ref, p_ref, g_ffn_ref, w_up_ref, conv_w_ref, conv_b_ref, w_down_ref,
                       g_ple_ref, w_ple_gate_ref, w_ple_ref, g_final_ref,
                       y_ref, conv_out_ref, a_ext, gate_s):
    t_idx = pl.program_id(1)
    tl = FFN_TILE

    @pl.when(t_idx == 0)
    def _():
        a_ext[0:CARRY_ROWS, :] = jnp.zeros((CARRY_ROWS, D_FF), F32)

    @pl.when(t_idx > 0)
    def _():
        a_ext[0:CARRY_ROWS, :] = a_ext[tl:tl + CARRY_ROWS, :]

    subs = [slice(i * FFN_SUB, (i + 1) * FFN_SUB) for i in range(tl // FFN_SUB)]
    for r in subs:
        xn = _rms(h_ref[r, :], g_ffn_ref[...]).astype(BF16)
        a_ext[CARRY_ROWS + r.start:CARRY_ROWS + r.stop, :] = _dot(xn, w_up_ref[:, 0:D_FF])
        gate_s[r, :] = _dot(xn, w_up_ref[:, D_FF:2 * D_FF])
    cw = conv_w_ref[...]
    for r in subs:
        ac = conv_b_ref[...]
        for j in range(CONV_W):
            off = CARRY_ROWS - (CONV_W - 1) + j
            ac = ac + a_ext[off + r.start:off + r.stop, :] * cw[j:j + 1, :]
        y_ref[r, :] = _ffn_tail(h_ref[r, :], ac, gate_s[r, :], p_ref[r, :], w_down_ref[...], g_ple_ref[...],
                                w_ple_gate_ref[...], w_ple_ref[...], g_final_ref[...])

    @pl.when(t_idx == pl.num_programs(1) - 1)
    def _():
        conv_out_ref[...] = a_ext[CARRY_ROWS + tl - (CONV_W - 1):CARRY_ROWS + tl, :]


def _ffn_prompt(h, p, g_ffn, w_up, conv_w, conv_b, w_down, g_ple, w_ple_gate, w_ple, g_final):
    b, l, d = h.shape
    nt = l // FFN_TILE
    tile = lambda w: pl.BlockSpec((None, FFN_TILE, w), lambda i, j: (i, j, 0))
    const_args = (g_ffn, w_up, conv_w, conv_b, w_down, g_ple, w_ple_gate, w_ple, g_final)
    return pl.pallas_call(
        _ffn_prompt_kernel,
        grid=(b, nt),
        in_specs=[tile(d), tile(PLE_DIM)] + [_const_spec(a.shape) for a in const_args],
        out_specs=[tile(d), pl.BlockSpec((None, CONV_W - 1, D_FF), lambda i, j: (i, 0, 0))],
        out_shape=[jax.ShapeDtypeStruct((b, l, d), F32),
                   jax.ShapeDtypeStruct((b, CONV_W - 1, D_FF), F32)],
        scratch_shapes=[pltpu.VMEM((FFN_TILE + CARRY_ROWS, D_FF), F32), pltpu.VMEM((FFN_TILE, D_FF), F32)],
        compiler_params=pltpu.CompilerParams(
            dimension_semantics=("arbitrary", "arbitrary"), vmem_limit_bytes=V7X_VMEM_LIMIT),
        name="ffn_prompt",
    )(h, p, *const_args)


def _proj_sample_kernel(x_ref, cos_ref, sin_ref, w_in_ref, g_mix_ref, lb_ref,
                        f_ref, kk_ref, q_ref, v_ref, kr_ref, qr_ref, vr_ref, gates_ref):
    xn = _rms(x_ref[...], g_mix_ref[...]).astype(BF16)
    u = _dot(xn, w_in_ref[:, C_HQ:C_HG])
    q, f, kk = _hgrn_inputs(u[:, 0:HG_W], u[:, HG_W:2 * HG_W], lb_ref[...])
    f_ref[...] = f
    kk_ref[...] = kk
    q_ref[...] = q
    v_ref[...] = u[:, 2 * HG_W:3 * HG_W]
    u = _dot(xn, w_in_ref[:, C_RQ:C_RG])
    cos_full = cos_ref[...]
    sin_signed = sin_ref[...]
    qr_ref[...] = _rotary(u[:, 0:RET_W], cos_full, sin_signed)
    kr_ref[...] = _rotary(u[:, RET_W:2 * RET_W], cos_full, sin_signed) * (RET_DK ** -0.5)
    vr_ref[...] = u[:, 2 * RET_W:3 * RET_W]
    gates_ref[:, 0:HG_W] = _dot(xn, w_in_ref[:, C_HG:C_RQ])
    gates_ref[:, HG_W:HG_W + RET_W] = _dot(xn, w_in_ref[:, C_RG:C_GA])
    gates_ref[:, HG_W + RET_W:] = _dot(xn, w_in_ref[:, C_GA:C_END])


def _proj_sample(x, cos_full, sin_signed, w_in, g_mix, lb):
    n = x.shape[0]
    col = jax.ShapeDtypeStruct((n, HG_W), F32)
    return pl.pallas_call(
        _proj_sample_kernel,
        out_shape=[col] * 7 + [jax.ShapeDtypeStruct((n, HG_W + RET_W + 2 * D_MODEL), F32)],
        compiler_params=pltpu.CompilerParams(vmem_limit_bytes=V7X_VMEM_LIMIT),
        name="proj_sample",
    )(x, cos_full, sin_signed, w_in, g_mix, lb)


def _state_kernel(s_ref, a_ref, k_ref, q_ref, v_ref, s_out_ref, o_ref, *, dk, dv, kb):
    j = pl.program_id(0)
    n = s_ref.shape[0]

    @pl.when(j == 0)
    def _():
        o_ref[...] = jnp.zeros_like(o_ref)

    group0 = j * (kb // SUBLANES)
    head_row = pl.multiple_of((j * kb // dk) * dv, dv)
    v_blk = v_ref[pl.ds(head_row, dv), :]
    acc = jnp.zeros(v_blk.shape, F32)
    for g in range(kb // SUBLANES):
        rows = slice(SUBLANES * g, SUBLANES * (g + 1))
        s_t = s_ref[:, rows, :].reshape(n * SUBLANES, dv).T
        s_new = a_ref[pl.ds(group0 + g, 1), :] * s_t + k_ref[pl.ds(group0 + g, 1), :] * v_blk
        s_out_ref[:, rows, :] = s_new.T.reshape(n, SUBLANES, dv)
        acc = acc + q_ref[pl.ds(group0 + g, 1), :] * s_new
    o_ref[pl.ds(head_row, dv), :] += acc


def _state_update(state, layer, a_il, k_il, q_il, v_rep, *, kb, name):
    _, n, heads, dk, dv = state.shape
    per_head = dk // kb
    full = lambda a: pl.BlockSpec(a.shape, lambda j: (0, 0))
    slab = lambda lead: pl.BlockSpec((None, n, None, kb, dv), lambda j: (lead, 0, j // per_head, j % per_head, 0))
    return pl.pallas_call(
        functools.partial(_state_kernel, dk=dk, dv=dv, kb=kb),
        grid=(heads * per_head,),
        in_specs=[slab(layer), full(a_il), full(k_il), full(q_il), full(v_rep)],
        out_specs=[slab(0), full(v_rep)],
        out_shape=[jax.ShapeDtypeStruct((1,) + state.shape[1:], F32), jax.ShapeDtypeStruct(v_rep.shape, F32)],
        compiler_params=pltpu.CompilerParams(
            dimension_semantics=("arbitrary",), vmem_limit_bytes=V7X_VMEM_LIMIT),
        name=name,
    )(state, a_il, k_il, q_il, v_rep)


def _lane_group_sum(o_ref, sel):
    o = o_ref[...]
    hi = o.astype(BF16)
    lo = (o - hi.astype(F32)).astype(BF16)
    return _dot_nt(sel, hi) + _dot_nt(sel, lo)


def _tail_sample_kernel(x_ref, o_hg_ref, o_ret_ref, sel_ref, gates_ref, conv0_ref, p_ref,
                        hm_hg_ref, hm_ret_ref, g_hg_ref, g_ret_ref, w_br_hg_ref, w_br_ret_ref, w_out_ref,
                        g_ffn_ref, w_up_ref, conv_w_ref, conv_b_ref, w_down_ref,
                        g_ple_ref, w_ple_gate_ref, w_ple_ref, g_final_ref,
                        y_ref, conv_out_ref):
    h = _merge_branches(
        x_ref[...], _lane_group_sum(o_hg_ref, sel_ref[...]), _lane_group_sum(o_ret_ref, sel_ref[...]),
        gates_ref[:, 0:HG_W], gates_ref[:, HG_W:HG_W + RET_W], gates_ref[:, HG_W + RET_W:],
        hm_hg_ref[...], hm_ret_ref[...], g_hg_ref[...], g_ret_ref[...],
        w_br_hg_ref[...], w_br_ret_ref[...], w_out_ref[...])
    xn = _rms(h, g_ffn_ref[...]).astype(BF16)
    a = _dot(xn, w_up_ref[:, 0:D_FF])
    gate_half = _dot(xn, w_up_ref[:, D_FF:2 * D_FF])
    cw = conv_w_ref[...]
    prev = conv0_ref[:, D_FF:2 * D_FF]
    ac = conv_b_ref[...] + conv0_ref[:, 0:D_FF] * cw[0:1, :] + prev * cw[1:2, :] + a * cw[2:3, :]
    conv_out_ref[:, 0:D_FF] = prev
    conv_out_ref[:, D_FF:2 * D_FF] = a
    y_ref[...] = _ffn_tail(h, ac, gate_half, p_ref[...], w_down_ref[...], g_ple_ref[...],
                           w_ple_gate_ref[...], w_ple_ref[...], g_final_ref[...])


def _tail_sample(x, o_hg, o_ret, sel, gates, conv0, p, *weights):
    n = x.shape[0]
    return pl.pallas_call(
        _tail_sample_kernel,
        out_shape=[jax.ShapeDtypeStruct((n, D_MODEL), F32),
                   jax.ShapeDtypeStruct((n, (CONV_W - 1) * D_FF), F32)],
        compiler_params=pltpu.CompilerParams(vmem_limit_bytes=V7X_VMEM_LIMIT),
        name="tail_sample",
    )(x, o_hg, o_ret, sel, gates, conv0, p, *weights)


def kernel(x_prompt, x_sample, state_hgrn, state_ret, state_conv, p_prompt, p_sample, lb_logits, w_in, g_mix, g_hg_out, g_ret_out, w_br_hg, w_br_ret, w_out, g_ffn, w_up, conv_w, conv_b, w_down, g_ple, w_ple, w_ple_gate, g_final):
    assert DEPTH == 1 and x_prompt.shape == (BATCH, SEQ, D_MODEL) and x_sample.shape == (DEC_BATCH, DEC_SEQ, D_MODEL)
    i = 0
    row = lambda a: a.reshape(1, -1).astype(F32)
    log_gamma = jnp.log1p(-jnp.exp2(-5.0 - jnp.arange(RET_HEADS, dtype=F32)))
    lb = jnp.cumsum(jax.nn.softmax(lb_logits.astype(F32), axis=0), axis=0)[i].reshape(1, HG_W)

    sel2, level = _level_tables()
    consts = (
        jnp.asarray(sel2, BF16),
        jnp.asarray(np.tile(level, (1, HG_GROUP))),
        jnp.asarray(_block_mask(MXU_W, MXU_W, CHUNK, HG_FDIM), BF16),
        jnp.asarray(_block_mask(MXU_W, MXU_W, HG_IDIM, HG_FDIM), F32),
        jnp.asarray(_block_mask(RET_HEADS * CHUNK, RET_W, CHUNK, RET_DK), BF16),
        jnp.asarray(_block_mask(MXU_W, MXU_W, RET_DK, RET_DV), F32),
    ) + _retention_tables(log_gamma)
    hm_hg = jnp.asarray(_block_mask(HG_W, HG_W, HG_IDIM, HG_IDIM) / HG_IDIM, BF16)
    hm_ret = jnp.asarray(_block_mask(RET_W, RET_W, RET_DV, RET_DV) / RET_DV, BF16)

    w_in_b = w_in[i].astype(BF16)
    mix_w = (hm_hg, hm_ret, row(g_hg_out[i]), row(g_ret_out[i]),
             w_br_hg[i].astype(BF16), w_br_ret[i].astype(BF16), w_out[i].astype(BF16))
    ffn_w = (row(g_ffn[i]), w_up[i].astype(BF16), conv_w[i].astype(F32), row(conv_b[i]), w_down[i].astype(BF16),
             row(g_ple[i]), w_ple_gate[i].astype(BF16), w_ple[i].astype(BF16), row(g_final))

    cos_p, sin_p = _rope_tables(0.0, SEQ)
    h_p, hg_state_p, ret_state_p = _mix_prompt(x_prompt, cos_p, sin_p, w_in_b, row(g_mix[i]), lb, consts, *mix_w)
    y_prompt, conv_p = _ffn_prompt(h_p, p_prompt[i], *ffn_w)

    n = DEC_BATCH
    xs = x_sample.reshape(n, D_MODEL)
    cos_s, sin_s = _rope_tables(float(PAST_LEN), DEC_SEQ)
    f, kk, q, v, kr, qr, vr, gates = _proj_sample(xs, cos_s, sin_s, w_in_b, row(g_mix[i]), lb)
    interleave = lambda a: a.reshape(n, -1, SUBLANES).transpose(1, 0, 2).reshape(-1, n * SUBLANES)
    repeat = lambda a: jnp.repeat(a.T, SUBLANES, axis=1)
    hg_new, o_hg = _state_update(
        state_hgrn, i, interleave(f), interleave(kk), interleave(q), repeat(v), kb=HG_FDIM, name="state_hgrn")
    gamma = jnp.broadcast_to(jnp.repeat(jnp.exp(log_gamma), RET_DK // SUBLANES)[:, None],
                             (RET_W // SUBLANES, n * SUBLANES))
    ret_new, o_ret = _state_update(
        state_ret, i, gamma, interleave(kr), interleave(qr), repeat(vr), kb=RET_DK // 2, name="state_ret")
    sel = jnp.asarray(_block_mask(n, n * SUBLANES, 1, SUBLANES), BF16)
    y_sample, conv_s = _tail_sample(
        xs, o_hg, o_ret, sel, gates, state_conv[i].reshape(n, (CONV_W - 1) * D_FF), p_sample[i].reshape(n, PLE_DIM),
        *mix_w, *ffn_w)

    return (y_prompt, y_sample.reshape(n, DEC_SEQ, D_MODEL),
            hg_state_p, ret_state_p, conv_p[None],
            hg_new, ret_new,
            conv_s.reshape(1, n, CONV_W - 1, D_FF))
```

```python
import functools

import numpy as np
import jax
import jax.numpy as jnp
from jax import lax
from jax.experimental import pallas as pl
from jax.experimental.pallas import tpu as pltpu

D_MODEL = 1024
BATCH = 8
SEQ = 2048
DEPTH = 1
DEC_BATCH = 128
DEC_SEQ = 1
PAST_LEN = 16384
HG_HEADS = 8
HG_FDIM = 64
HG_IDIM = 64
HG_W = HG_HEADS * HG_FDIM
RET_HEADS = 4
RET_DK = 128
RET_DV = 128
RET_W = RET_HEADS * RET_DK
D_FF = 2816
CONV_W = 3
PLE_DIM = 256
CHUNK = 64
ROPE_BASE = 10000.0
EPS = 1e-6

C_HQ, C_HF, C_HI, C_HG = 0, 512, 1024, 1536
C_RQ, C_RK, C_RV, C_RG = 2048, 2560, 3072, 3584
C_GA, C_GB, C_END = 4096, 5120, 6144

N_LEVELS = 6
FAST_DECAY_LIMIT = 80.0
HG_GROUP = 4
RET_GROUP = 2
MXU_W = 256
SUBLANES = 8

MIX_TILE = 256
GATE_PIECES = (
    ((C_HG, HG_W, 0), (C_RG, RET_W, HG_W)),
    ((C_GA, 768, HG_W + RET_W),),
    ((C_GA + 768, 768, HG_W + RET_W + 768),),
    ((C_GA + 1536, 512, HG_W + RET_W + 1536),),
)
FFN_TILE = 512
FFN_SUB = 256
CARRY_ROWS = 8
V7X_VMEM_LIMIT = 56 * 1024 * 1024

F32 = jnp.float32
BF16 = jnp.bfloat16


def _level_tables():
    c = CHUNK
    t = np.arange(c)[:, None]
    r = np.arange(c)[None, :]
    blocks = [r <= t, r > t]
    for l in range(N_LEVELS):
        same = (t >> l) == (r >> l)
        upper = ((t >> l) & 1) == 1
        blocks.append(np.where(upper, same & (r <= t), same & (r > t)))
    sel = np.concatenate(blocks, axis=0).astype(np.float32)
    sel2 = np.concatenate([sel, sel], axis=1)
    s = r
    level = np.full((c, c), N_LEVELS + 1, np.int32)
    level[np.broadcast_to(s == t, (c, c))] = N_LEVELS
    x = t ^ s
    for l in range(N_LEVELS):
        hit = (s < t) & ((x >> l) == 1)
        level[hit] = l
    return sel2, level


def _block_mask(rows, cols, rblk, cblk):
    r = np.arange(rows)[:, None] // rblk
    c = np.arange(cols)[None, :] // cblk
    return (r == c).astype(np.float32)


def _rope_tables(pos0, length):
    d = RET_DK
    inv_freq = ROPE_BASE ** (-jnp.arange(0, d, 2, dtype=F32) / d)
    pos = pos0 + jnp.arange(length, dtype=F32)
    ang = pos[:, None] * inv_freq[None, :]
    cos, sin = jnp.cos(ang), jnp.sin(ang)
    cos_full = jnp.tile(jnp.concatenate([cos, cos], axis=-1), (1, RET_HEADS))
    sin_signed = jnp.tile(jnp.concatenate([-sin, sin], axis=-1), (1, RET_HEADS))
    return cos_full, sin_signed


def _retention_tables(log_gamma):
    c = CHUNK
    t = jnp.arange(c, dtype=F32)
    delta = t[:, None] - t[None, :]
    decay = jnp.where(delta >= 0, jnp.exp(jnp.maximum(delta, 0.0) * log_gamma[:, None, None]), 0.0)
    dmask = jnp.transpose(decay, (1, 0, 2)).reshape(c, RET_HEADS * c)
    q_decay = jnp.exp((t + 1.0)[None, :] * log_gamma[:, None])
    k_decay = jnp.exp((c - 1.0 - t)[None, :] * log_gamma[:, None])
    qdec = jnp.repeat(q_decay.T, RET_DK, axis=1)
    kdec = jnp.repeat(k_decay.T, RET_DK, axis=1)
    chunk_decay = jnp.exp(c * log_gamma)
    cd = jnp.repeat(chunk_decay, RET_DK).reshape(RET_HEADS // RET_GROUP, MXU_W, 1)
    cd = jnp.broadcast_to(cd, (RET_HEADS // RET_GROUP, MXU_W, MXU_W))
    return dmask, qdec, kdec, cd


def _rms(x, g):
    ms = jnp.mean(x * x, axis=-1, keepdims=True)
    return x * lax.rsqrt(ms + EPS) * g


def _silu(x):
    return x * jax.nn.sigmoid(x)


def _dot(a, b):
    return jnp.dot(a, b, preferred_element_type=F32)


def _dot_nt(a, b):
    return lax.dot_general(a, b, (((1,), (1,)), ((), ())), preferred_element_type=F32)


def _dot_tn(a, b):
    return lax.dot_general(a, b, (((0,), (0,)), ((), ())), preferred_element_type=F32)


def _rotary(x, cos_full, sin_signed):
    swapped = jnp.concatenate(
        [pltpu.roll(x[:, RET_DK * h:RET_DK * (h + 1)], RET_DK // 2, 1) for h in range(RET_HEADS)], axis=1)
    return x * cos_full + swapped * sin_signed


def _hgrn_inputs(hq, hf, lb):
    q = _silu(hq)
    f = lb + (1.0 - lb) * jax.nn.sigmoid(hf)
    return q, f, 1.0 - f


def _merge_branches(x, o_hg, o_ret, u_hg, u_rg, u_gab, hm_hg, hm_ret, g_hg, g_ret, w_br_hg, w_br_ret, w_out):
    ms = _dot((o_hg * o_hg).astype(BF16), hm_hg)
    a_hg = o_hg * lax.rsqrt(ms + EPS) * g_hg * _silu(u_hg)
    y_hg = _dot(a_hg.astype(BF16), w_br_hg)
    ms = _dot((o_ret * o_ret).astype(BF16), hm_ret)
    a_ret = o_ret * lax.rsqrt(ms + EPS) * g_ret * _silu(u_rg)
    y_ret = _dot(a_ret.astype(BF16), w_br_ret)
    mixed = jax.nn.sigmoid(u_gab[:, :D_MODEL]) * y_hg + jax.nn.sigmoid(u_gab[:, D_MODEL:]) * y_ret
    return x + _dot(mixed.astype(BF16), w_out)


def _ffn_tail(h, ac, gate_half, p, w_down, g_ple, w_ple_gate, w_ple, g_final):
    h = h + _dot((jax.nn.gelu(ac) * gate_half).astype(BF16), w_down)
    gate = jax.nn.sigmoid(_dot(_rms(h, g_ple).astype(BF16), w_ple_gate))
    h = h + gate * _dot(p.astype(BF16), w_ple)
    return _rms(h, g_final)


def _mix_prompt_kernel(x_ref, cos_ref, sin_ref, w_in_ref, g_mix_ref, lb_ref, sel_ref, lvl_ref,
                       bd64_ref, bd64f_ref, bdret_ref, bd128f_ref, dmask_ref, qdec_ref, kdec_ref, cd_ref,
                       hm_hg_ref, hm_ret_ref, g_hg_ref, g_ret_ref, w_br_hg_ref, w_br_ret_ref, w_out_ref,
                       h_ref, st_hg_out_ref, st_ret_out_ref,
                       xn_s, q_s, kk_s, lfh_s, lfl_s, v_s, qr_s, kr_s, vr_s, o_hg_s, o_ret_s, gate_s, st_hg, st_ret):
    t_idx = pl.program_id(1)
    n_hg_groups = HG_HEADS // HG_GROUP
    n_ret_groups = RET_HEADS // RET_GROUP

    @pl.when(t_idx == 0)
    def _():
        st_hg[...] = jnp.zeros_like(st_hg)
        st_ret[...] = jnp.zeros_like(st_ret)

    x = x_ref[...]
    xn = _rms(x, g_mix_ref[...]).astype(BF16)
    xn_s[...] = xn

    u = _dot(xn, w_in_ref[:, C_HQ:C_HG])
    q, f, kk = _hgrn_inputs(u[:, 0:HG_W], u[:, HG_W:2 * HG_W], lb_ref[...])
    q_s[...] = q
    kk_s[...] = kk
    lf = jnp.log(f)
    lf_hi = lf.astype(BF16)
    lfh_s[...] = lf_hi
    lfl_s[...] = (lf - lf_hi.astype(F32)).astype(BF16)
    v_s[...] = u[:, 2 * HG_W:3 * HG_W].astype(BF16)

    u = _dot(xn, w_in_ref[:, C_RQ:C_RG])
    cos_full = cos_ref[...]
    sin_signed = sin_ref[...]
    qr_s[...] = _rotary(u[:, 0:RET_W], cos_full, sin_signed)
    kr_s[...] = _rotary(u[:, RET_W:2 * RET_W], cos_full, sin_signed) * (RET_DK ** -0.5)
    vr_s[...] = u[:, 2 * RET_W:3 * RET_W].astype(BF16)

    chunk_sums = [jnp.sum(lf[c * CHUNK:(c + 1) * CHUNK, :], axis=0, keepdims=True) for c in range(MIX_TILE // CHUNK)]
    mild = jnp.min(functools.reduce(jnp.minimum, chunk_sums)) >= -FAST_DECAY_LIMIT

    def chunk_body(c, fast):
        rows = slice(c * CHUNK, (c + 1) * CHUNK)

        q = q_s[rows, :]
        kk = kk_s[rows, :]
        v = v_s[rows, :]
        lf2 = jnp.concatenate([lfh_s[rows, :], lfl_s[rows, :]], axis=0)
        x_cs = _dot(sel_ref[0:2 * CHUNK, :], lf2)
        e_cs = jnp.exp(x_cs)
        lvl = lvl_ref[...]
        bd64 = bd64_ref[...]
        bd64f = bd64f_ref[...]
        groups = [slice(MXU_W * g, MXU_W * (g + 1)) for g in range(n_hg_groups)]
        scores_of = []
        if fast:
            e_neg = jnp.exp(-x_cs[0:CHUNK, :])
            for cs in groups:
                qh = (q[:, cs] * e_cs[0:CHUNK, cs]).astype(BF16)
                kh = (kk[:, cs] * e_neg[:, cs]).astype(BF16)
                rhs = jnp.concatenate([kh] * HG_GROUP, axis=0) * bd64
                scores_of.append(jnp.where(lvl <= N_LEVELS, _dot_nt(qh, rhs), 0.0))
        else:
            ex = jnp.exp(_dot(sel_ref[2 * CHUNK:, :], lf2))
            for cs in groups:
                scores = jnp.zeros((CHUNK, MXU_W), F32)
                for l in range(N_LEVELS + 1):
                    if l < N_LEVELS:
                        e = ex[CHUNK * l:CHUNK * (l + 1), cs]
                        ql = (q[:, cs] * e).astype(BF16)
                        kl = (kk[:, cs] * e).astype(BF16)
                    else:
                        ql = q[:, cs].astype(BF16)
                        kl = kk[:, cs].astype(BF16)
                    rhs = jnp.concatenate([kl] * HG_GROUP, axis=0) * bd64
                    scores = jnp.where(lvl == l, _dot_nt(ql, rhs), scores)
                scores_of.append(scores)

        o_parts = []
        for g, cs in enumerate(groups):
            v_bd = jnp.concatenate([v[:, cs]] * HG_GROUP, axis=0) * bd64
            o_intra = _dot(scores_of[g].astype(BF16), v_bd)
            e_cum = e_cs[0:CHUNK, cs]
            e_suf = e_cs[CHUNK:2 * CHUNK, cs]
            st = st_hg[g]
            o_inter = _dot_nt((q[:, cs] * e_cum).astype(BF16), st.astype(BF16))
            o_parts.append(o_intra + o_inter)
            upd = _dot_tn(v[:, cs], (kk[:, cs] * e_suf).astype(BF16))
            st_hg[g] = st * e_cum[CHUNK - 1:CHUNK, :] + upd * bd64f
        o_hg_s[rows, :] = jnp.concatenate(o_parts, axis=1)

        qr = qr_s[rows, :]
        kr = kr_s[rows, :]
        vr = vr_s[rows, :]
        bdret = bdret_ref[...]
        k_bd = jnp.concatenate([kr.astype(BF16)] * RET_HEADS, axis=0) * bdret
        scores = _dot_nt(qr.astype(BF16), k_bd) * dmask_ref[...]
        v_bd = jnp.concatenate([vr] * RET_HEADS, axis=0) * bdret
        o_ret = _dot(scores.astype(BF16), v_bd)
        qd = (qr * qdec_ref[...]).astype(BF16)
        kd = (kr * kdec_ref[...]).astype(BF16)
        bd128f = bd128f_ref[...]
        o_parts = []
        for g in range(n_ret_groups):
            cs = slice(MXU_W * g, MXU_W * (g + 1))
            st = st_ret[g]
            o_parts.append(_dot(qd[:, cs], st.astype(BF16)))
            upd = _dot_tn(kd[:, cs], vr[:, cs])
            st_ret[g] = cd_ref[g] * st + upd * bd128f
        o_ret_s[rows, :] = o_ret + jnp.concatenate(o_parts, axis=1)

    def recurrences(fast):
        for c, pieces in enumerate(GATE_PIECES):
            chunk_body(c, fast)
            for src, width, dst in pieces:
                gate_s[:, dst:dst + width] = _dot(xn_s[...], w_in_ref[:, src:src + width])

    pl.when(mild)(functools.partial(recurrences, True))
    pl.when(jnp.logical_not(mild))(functools.partial(recurrences, False))

    h_ref[...] = _merge_branches(
        x_ref[...], o_hg_s[...], o_ret_s[...],
        gate_s[:, 0:HG_W], gate_s[:, HG_W:HG_W + RET_W], gate_s[:, HG_W + RET_W:],
        hm_hg_ref[...], hm_ret_ref[...], g_hg_ref[...], g_ret_ref[...],
        w_br_hg_ref[...], w_br_ret_ref[...], w_out_ref[...])

    @pl.when(t_idx == pl.num_programs(1) - 1)
    def _():
        for g in range(n_hg_groups):
            st_t = st_hg[g].T
            for h in range(HG_GROUP):
                blk = slice(HG_FDIM * h, HG_FDIM * (h + 1))
                st_hg_out_ref[HG_GROUP * g + h] = st_t[blk, blk]
        for g in range(n_ret_groups):
            for h in range(RET_GROUP):
                blk = slice(RET_DK * h, RET_DK * (h + 1))
                st_ret_out_ref[RET_GROUP * g + h] = st_ret[g, blk, blk]


def _const_spec(shape):
    zeros = (0,) * len(shape)
    return pl.BlockSpec(shape, lambda *_: zeros, pipeline_mode=pl.Buffered(1))


def _mix_prompt(x, cos_full, sin_signed, w_in, g_mix, lb, consts, hm_hg, hm_ret, g_hg, g_ret,
                w_br_hg, w_br_ret, w_out):
    b, l, d = x.shape
    nt = l // MIX_TILE
    tile = lambda w: pl.BlockSpec((None, MIX_TILE, w), lambda i, j: (i, j, 0))
    table = lambda w: pl.BlockSpec((MIX_TILE, w), lambda i, j: (j, 0))
    n_hg_groups = HG_HEADS // HG_GROUP
    n_ret_groups = RET_HEADS // RET_GROUP
    const_args = (w_in, g_mix, lb) + tuple(consts) + (hm_hg, hm_ret, g_hg, g_ret, w_br_hg, w_br_ret, w_out)
    state_spec = lambda h, dk, dv: pl.BlockSpec((None, None, h, dk, dv), lambda i, j: (0, i, 0, 0, 0))
    return pl.pallas_call(
        _mix_prompt_kernel,
        grid=(b, nt),
        in_specs=[tile(d), table(RET_W), table(RET_W)] + [_const_spec(a.shape) for a in const_args],
        out_specs=[tile(d), state_spec(HG_HEADS, HG_FDIM, HG_IDIM), state_spec(RET_HEADS, RET_DK, RET_DV)],
        out_shape=[jax.ShapeDtypeStruct((b, l, d), F32),
                   jax.ShapeDtypeStruct((1, b, HG_HEADS, HG_FDIM, HG_IDIM), F32),
                   jax.ShapeDtypeStruct((1, b, RET_HEADS, RET_DK, RET_DV), F32)],
        scratch_shapes=[
            pltpu.VMEM((MIX_TILE, d), BF16),
            pltpu.VMEM((MIX_TILE, HG_W), F32),
            pltpu.VMEM((MIX_TILE, HG_W), F32),
            pltpu.VMEM((MIX_TILE, HG_W), BF16),
            pltpu.VMEM((MIX_TILE, HG_W), BF16),
            pltpu.VMEM((MIX_TILE, HG_W), BF16),
            pltpu.VMEM((MIX_TILE, RET_W), F32),
            pltpu.VMEM((MIX_TILE, RET_W), F32),
            pltpu.VMEM((MIX_TILE, RET_W), BF16),
            pltpu.VMEM((MIX_TILE, HG_W), F32),
            pltpu.VMEM((MIX_TILE, RET_W), F32),
            pltpu.VMEM((MIX_TILE, HG_W + RET_W + 2 * D_MODEL), F32),
            pltpu.VMEM((n_hg_groups, MXU_W, MXU_W), F32),
            pltpu.VMEM((n_ret_groups, MXU_W, MXU_W), F32),
        ],
        compiler_params=pltpu.CompilerParams(
            dimension_semantics=("arbitrary", "arbitrary"), vmem_limit_bytes=V7X_VMEM_LIMIT),
        name="mix_prompt",
    )(x, cos_full, sin_signed, *const_args)


def _ffn_prompt_kernel(h_ref, p_ref, g_ffn_ref, w_up_ref, conv_w_ref, conv_b_ref, w_down_ref,
                       g_ple_ref, w_ple_gate_ref, w_ple_ref, g_final_ref,
                       y_ref, conv_out_ref, a_ext, gate_s):
    t_idx = pl.program_id(1)
    tl = FFN_TILE

    @pl.when(t_idx == 0)
    def _():
        a_ext[0:CARRY_ROWS, :] = jnp.zeros((CARRY_ROWS, D_FF), F32)

    @pl.when(t_idx > 0)
    def _():
        a_ext[0:CARRY_ROWS, :] = a_ext[tl:tl + CARRY_ROWS, :]

    subs = [slice(i * FFN_SUB, (i + 1) * FFN_SUB) for i in range(tl // FFN_SUB)]
    for r in subs:
        xn = _rms(h_ref[r, :], g_ffn_ref[...]).astype(BF16)
        a_ext[CARRY_ROWS + r.start:CARRY_ROWS + r.stop, :] = _dot(xn, w_up_ref[:, 0:D_FF])
        gate_s[r, :] = _dot(xn, w_up_ref[:, D_FF:2 * D_FF])
    cw = conv_w_ref[...]
    for r in subs:
        ac = conv_b_ref[...]
        for j in range(CONV_W):
            off = CARRY_ROWS - (CONV_W - 1) + j
            ac = ac + a_ext[off + r.start:off + r.stop, :] * cw[j:j + 1, :]
        y_ref[r, :] = _ffn_tail(h_ref[r, :], ac, gate_s[r, :], p_ref[r, :], w_down_ref[...], g_ple_ref[...],
                                w_ple_gate_ref[...], w_ple_ref[...], g_final_ref[...])

    @pl.when(t_idx == pl.num_programs(1) - 1)
    def _():
        conv_out_ref[...] = a_ext[CARRY_ROWS + tl - (CONV_W - 1):CARRY_ROWS + tl, :]


def _ffn_prompt(h, p, g_ffn, w_up, conv_w, conv_b, w_down, g_ple, w_ple_gate, w_ple, g_final):
    b, l, d = h.shape
    nt = l // FFN_TILE
    tile = lambda w: pl.BlockSpec((None, FFN_TILE, w), lambda i, j: (i, j, 0))
    const_args = (g_ffn, w_up, conv_w, conv_b, w_down, g_ple, w_ple_gate, w_ple, g_final)
    return pl.pallas_call(
        _ffn_prompt_kernel,
        grid=(b, nt),
        in_specs=[tile(d), tile(PLE_DIM)] + [_const_spec(a.shape) for a in const_args],
        out_specs=[tile(d), pl.BlockSpec((None, CONV_W - 1, D_FF), lambda i, j: (i, 0, 0))],
        out_shape=[jax.ShapeDtypeStruct((b, l, d), F32),
                   jax.ShapeDtypeStruct((b, CONV_W - 1, D_FF), F32)],
        scratch_shapes=[pltpu.VMEM((FFN_TILE + CARRY_ROWS, D_FF), F32), pltpu.VMEM((FFN_TILE, D_FF), F32)],
        compiler_params=pltpu.CompilerParams(
            dimension_semantics=("arbitrary", "arbitrary"), vmem_limit_bytes=V7X_VMEM_LIMIT),
        name="ffn_prompt",
    )(h, p, *const_args)


def _proj_sample_kernel(x_ref, cos_ref, sin_ref, w_in_ref, g_mix_ref, lb_ref,
                        f_ref, kk_ref, q_ref, v_ref, kr_ref, qr_ref, vr_ref, gates_ref):
    xn = _rms(x_ref[...], g_mix_ref[...]).astype(BF16)
    u = _dot(xn, w_in_ref[:, C_HQ:C_HG])
    q, f, kk = _hgrn_inputs(u[:, 0:HG_W], u[:, HG_W:2 * HG_W], lb_ref[...])
    f_ref[...] = f.T
    kk_ref[...] = kk.T
    q_ref[...] = q.T
    v_ref[...] = u[:, 2 * HG_W:3 * HG_W].T
    u = _dot(xn, w_in_ref[:, C_RQ:C_RG])
    cos_full = cos_ref[...]
    sin_signed = sin_ref[...]
    qr_ref[...] = _rotary(u[:, 0:RET_W], cos_full, sin_signed)
    kr_ref[...] = _rotary(u[:, RET_W:2 * RET_W], cos_full, sin_signed) * (RET_DK ** -0.5)
    vr_ref[...] = u[:, 2 * RET_W:3 * RET_W]
    gates_ref[:, 0:HG_W] = _dot(xn, w_in_ref[:, C_HG:C_RQ])
    gates_ref[:, HG_W:HG_W + RET_W] = _dot(xn, w_in_ref[:, C_RG:C_GA])
    gates_ref[:, HG_W + RET_W:] = _dot(xn, w_in_ref[:, C_GA:C_END])


def _proj_sample(x, cos_full, sin_signed, w_in, g_mix, lb):
    n = x.shape[0]
    col = jax.ShapeDtypeStruct((HG_W, n), F32)
    row = jax.ShapeDtypeStruct((n, RET_W), F32)
    return pl.pallas_call(
        _proj_sample_kernel,
        out_shape=[col] * 4 + [row] * 3 + [jax.ShapeDtypeStruct((n, HG_W + RET_W + 2 * D_MODEL), F32)],
        compiler_params=pltpu.CompilerParams(vmem_limit_bytes=V7X_VMEM_LIMIT),
        name="proj_sample",
    )(x, cos_full, sin_signed, w_in, g_mix, lb)


def _state_tokens_minor_kernel(s_ref, a_t_ref, k_t_ref, q_t_ref, v_t_ref, s_out_ref, o_t_ref):
    dk, dv, _ = s_ref.shape
    row0 = pl.program_id(0) * dk
    v_blk = v_t_ref[...]

    def body(k, acc):
        s_new = a_t_ref[pl.ds(row0 + k, 1), :] * s_ref[k] + k_t_ref[pl.ds(row0 + k, 1), :] * v_blk
        s_out_ref[k] = s_new
        return acc + q_t_ref[pl.ds(row0 + k, 1), :] * s_new

    o_t_ref[...] = lax.fori_loop(0, dk, body, jnp.zeros(v_blk.shape, F32), unroll=SUBLANES)


def _state_update_tokens_minor(state_t, layer, a_t, k_t, q_t, v_t, *, name):
    _, heads, dk, dv, n = state_t.shape
    full = lambda a: pl.BlockSpec(a.shape, lambda h: (0, 0))
    per_head = pl.BlockSpec((dv, n), lambda h: (h, 0))
    slab = lambda lead: pl.BlockSpec((None, None, dk, dv, n), lambda h: (lead, h, 0, 0, 0))
    return pl.pallas_call(
        _state_tokens_minor_kernel,
        grid=(heads,),
        in_specs=[slab(layer), full(a_t), full(k_t), full(q_t), per_head],
        out_specs=[slab(0), per_head],
        out_shape=[jax.ShapeDtypeStruct((1,) + state_t.shape[1:], F32), jax.ShapeDtypeStruct(v_t.shape, F32)],
        compiler_params=pltpu.CompilerParams(
            dimension_semantics=("arbitrary",), vmem_limit_bytes=V7X_VMEM_LIMIT),
        name=name,
    )(state_t, a_t, k_t, q_t, v_t)


def _state_kernel(s_ref, a_ref, k_ref, q_ref, v_ref, s_out_ref, o_ref, *, dk, dv, kb):
    j = pl.program_id(0)
    n = s_ref.shape[0]

    @pl.when(j == 0)
    def _():
        o_ref[...] = jnp.zeros_like(o_ref)

    group0 = j * (kb // SUBLANES)
    head_row = pl.multiple_of((j * kb // dk) * dv, dv)
    v_blk = v_ref[pl.ds(head_row, dv), :]
    acc = jnp.zeros(v_blk.shape, F32)
    for g in range(kb // SUBLANES):
        rows = slice(SUBLANES * g, SUBLANES * (g + 1))
        s_t = s_ref[:, rows, :].reshape(n * SUBLANES, dv).T
        s_new = a_ref[pl.ds(group0 + g, 1), :] * s_t + k_ref[pl.ds(group0 + g, 1), :] * v_blk
        s_out_ref[:, rows, :] = s_new.T.reshape(n, SUBLANES, dv)
        acc = acc + q_ref[pl.ds(group0 + g, 1), :] * s_new
    o_ref[pl.ds(head_row, dv), :] += acc


def _state_update(state, layer, a_il, k_il, q_il, v_rep, *, kb, name):
    _, n, heads, dk, dv = state.shape
    per_head = dk // kb
    full = lambda a: pl.BlockSpec(a.shape, lambda j: (0, 0))
    slab = lambda lead: pl.BlockSpec((None, n, None, kb, dv), lambda j: (lead, 0, j // per_head, j % per_head, 0))
    return pl.pallas_call(
        functools.partial(_state_kernel, dk=dk, dv=dv, kb=kb),
        grid=(heads * per_head,),
        in_specs=[slab(layer), full(a_il), full(k_il), full(q_il), full(v_rep)],
        out_specs=[slab(0), full(v_rep)],
        out_shape=[jax.ShapeDtypeStruct((1,) + state.shape[1:], F32), jax.ShapeDtypeStruct(v_rep.shape, F32)],
        compiler_params=pltpu.CompilerParams(
            dimension_semantics=("arbitrary",), vmem_limit_bytes=V7X_VMEM_LIMIT),
        name=name,
    )(state, a_il, k_il, q_il, v_rep)


def _lane_group_sum(o_ref, sel):
    o = o_ref[...]
    hi = o.astype(BF16)
    lo = (o - hi.astype(F32)).astype(BF16)
    return _dot_nt(sel, hi) + _dot_nt(sel, lo)


def _tail_sample_kernel(x_ref, o_hg_t_ref, o_ret_ref, sel_ref, gates_ref, conv0_ref, p_ref,
                        hm_hg_ref, hm_ret_ref, g_hg_ref, g_ret_ref, w_br_hg_ref, w_br_ret_ref, w_out_ref,
                        g_ffn_ref, w_up_ref, conv_w_ref, conv_b_ref, w_down_ref,
                        g_ple_ref, w_ple_gate_ref, w_ple_ref, g_final_ref,
                        y_ref, conv_out_ref):
    h = _merge_branches(
        x_ref[...], o_hg_t_ref[...].T, _lane_group_sum(o_ret_ref, sel_ref[...]),
        gates_ref[:, 0:HG_W], gates_ref[:, HG_W:HG_W + RET_W], gates_ref[:, HG_W + RET_W:],
        hm_hg_ref[...], hm_ret_ref[...], g_hg_ref[...], g_ret_ref[...],
        w_br_hg_ref[...], w_br_ret_ref[...], w_out_ref[...])
    xn = _rms(h, g_ffn_ref[...]).astype(BF16)
    a = _dot(xn, w_up_ref[:, 0:D_FF])
    gate_half = _dot(xn, w_up_ref[:, D_FF:2 * D_FF])
    cw = conv_w_ref[...]
    prev = conv0_ref[:, 1, :]
    ac = conv_b_ref[...] + conv0_ref[:, 0, :] * cw[0:1, :] + prev * cw[1:2, :] + a * cw[2:3, :]
    conv_out_ref[0, :, 0, :] = prev
    conv_out_ref[0, :, 1, :] = a
    y_ref[...] = _ffn_tail(h, ac, gate_half, p_ref[...], w_down_ref[...], g_ple_ref[...],
                           w_ple_gate_ref[...], w_ple_ref[...], g_final_ref[...])


def _tail_sample(x, o_hg_t, o_ret, sel, gates, conv0, p, *weights):
    n = x.shape[0]
    return pl.pallas_call(
        _tail_sample_kernel,
        out_shape=[jax.ShapeDtypeStruct((n, D_MODEL), F32),
                   jax.ShapeDtypeStruct((1, n, CONV_W - 1, D_FF), F32)],
        compiler_params=pltpu.CompilerParams(vmem_limit_bytes=V7X_VMEM_LIMIT),
        name="tail_sample",
    )(x, o_hg_t, o_ret, sel, gates, conv0, p, *weights)


def kernel(x_prompt, x_sample, state_hgrn, state_ret, state_conv, p_prompt, p_sample, lb_logits, w_in, g_mix, g_hg_out, g_ret_out, w_br_hg, w_br_ret, w_out, g_ffn, w_up, conv_w, conv_b, w_down, g_ple, w_ple, w_ple_gate, g_final):
    assert DEPTH == 1 and x_prompt.shape == (BATCH, SEQ, D_MODEL) and x_sample.shape == (DEC_BATCH, DEC_SEQ, D_MODEL)
    i = 0
    row = lambda a: a.reshape(1, -1).astype(F32)
    log_gamma = jnp.log1p(-jnp.exp2(-5.0 - jnp.arange(RET_HEADS, dtype=F32)))
    lb = jnp.cumsum(jax.nn.softmax(lb_logits.astype(F32), axis=0), axis=0)[i].reshape(1, HG_W)

    sel2, level = _level_tables()
    consts = (
        jnp.asarray(sel2, BF16),
        jnp.asarray(np.tile(level, (1, HG_GROUP))),
        jnp.asarray(_block_mask(MXU_W, MXU_W, CHUNK, HG_FDIM), BF16),
        jnp.asarray(_block_mask(MXU_W, MXU_W, HG_IDIM, HG_FDIM), F32),
        jnp.asarray(_block_mask(RET_HEADS * CHUNK, RET_W, CHUNK, RET_DK), BF16),
        jnp.asarray(_block_mask(MXU_W, MXU_W, RET_DK, RET_DV), F32),
    ) + _retention_tables(log_gamma)
    hm_hg = jnp.asarray(_block_mask(HG_W, HG_W, HG_IDIM, HG_IDIM) / HG_IDIM, BF16)
    hm_ret = jnp.asarray(_block_mask(RET_W, RET_W, RET_DV, RET_DV) / RET_DV, BF16)

    w_in_b = w_in[i].astype(BF16)
    mix_w = (hm_hg, hm_ret, row(g_hg_out[i]), row(g_ret_out[i]),
             w_br_hg[i].astype(BF16), w_br_ret[i].astype(BF16), w_out[i].astype(BF16))
    ffn_w = (row(g_ffn[i]), w_up[i].astype(BF16), conv_w[i].astype(F32), row(conv_b[i]), w_down[i].astype(BF16),
             row(g_ple[i]), w_ple_gate[i].astype(BF16), w_ple[i].astype(BF16), row(g_final))

    cos_p, sin_p = _rope_tables(0.0, SEQ)
    h_p, hg_state_p, ret_state_p = _mix_prompt(x_prompt, cos_p, sin_p, w_in_b, row(g_mix[i]), lb, consts, *mix_w)
    y_prompt, conv_p = _ffn_prompt(h_p, p_prompt[i], *ffn_w)

    n = DEC_BATCH
    xs = x_sample.reshape(n, D_MODEL)
    cos_s, sin_s = _rope_tables(float(PAST_LEN), DEC_SEQ)
    f_t, kk_t, q_t, v_t, kr, qr, vr, gates = _proj_sample(xs, cos_s, sin_s, w_in_b, row(g_mix[i]), lb)
    hg_new_t, o_hg_t = _state_update_tokens_minor(
        jnp.transpose(state_hgrn, (0, 2, 3, 4, 1)), i, f_t, kk_t, q_t, v_t, name="state_hgrn")
    hg_new = jnp.transpose(hg_new_t, (0, 4, 1, 2, 3))
    interleave = lambda a: a.reshape(n, -1, SUBLANES).transpose(1, 0, 2).reshape(-1, n * SUBLANES)
    repeat = lambda a: jnp.repeat(a.T, SUBLANES, axis=1)
    gamma = jnp.broadcast_to(jnp.repeat(jnp.exp(log_gamma), RET_DK // SUBLANES)[:, None],
                             (RET_W // SUBLANES, n * SUBLANES))
    ret_new, o_ret = _state_update(
        state_ret, i, gamma, interleave(kr), interleave(qr), repeat(vr), kb=RET_DK // 2, name="state_ret")
    sel = jnp.asarray(_block_mask(n, n * SUBLANES, 1, SUBLANES), BF16)
    y_sample, conv_s = _tail_sample(
        xs, o_hg_t, o_ret, sel, gates, state_conv[i], p_sample[i].reshape(n, PLE_DIM),
        *mix_w, *ffn_w)

    return (y_prompt, y_sample.reshape(n, DEC_SEQ, D_MODEL),
            hg_state_p, ret_state_p, conv_p[None],
            hg_new, ret_new,
            conv_s)
```

```python
import functools

import numpy as np
import jax
import jax.numpy as jnp
from jax import lax
from jax.experimental import pallas as pl
from jax.experimental.pallas import tpu as pltpu

D_MODEL = 1024
BATCH = 8
SEQ = 2048
DEPTH = 1
DEC_BATCH = 128
DEC_SEQ = 1
PAST_LEN = 16384
HG_HEADS = 8
HG_FDIM = 64
HG_IDIM = 64
HG_W = HG_HEADS * HG_FDIM
RET_HEADS = 4
RET_DK = 128
RET_DV = 128
RET_W = RET_HEADS * RET_DK
D_FF = 2816
CONV_W = 3
PLE_DIM = 256
CHUNK = 64
ROPE_BASE = 10000.0
EPS = 1e-6

C_HQ, C_HF, C_HI, C_HG = 0, 512, 1024, 1536
C_RQ, C_RK, C_RV, C_RG = 2048, 2560, 3072, 3584
C_GA, C_GB, C_END = 4096, 5120, 6144

N_LEVELS = 6
FAST_DECAY_LIMIT = 80.0
HG_GROUP = 4
RET_GROUP = 2
MXU_W = 256
SUBLANES = 8

MIX_TILE = 512
MIX_SUB = 256
GATE_PIECES = (
    ((C_HG, HG_W, 0), (C_RG, RET_W, HG_W)),
    ((C_GA, 768, HG_W + RET_W),),
    ((C_GA + 768, 768, HG_W + RET_W + 768),),
    ((C_GA + 1536, 512, HG_W + RET_W + 1536),),
)
FFN_TILE = 512
FFN_SUB = 256
CARRY_ROWS = 8
V7X_VMEM_LIMIT = 56 * 1024 * 1024

F32 = jnp.float32
BF16 = jnp.bfloat16


def _level_tables():
    c = CHUNK
    t = np.arange(c)[:, None]
    r = np.arange(c)[None, :]
    blocks = [r <= t, r > t]
    for l in range(N_LEVELS):
        same = (t >> l) == (r >> l)
        upper = ((t >> l) & 1) == 1
        blocks.append(np.where(upper, same & (r <= t), same & (r > t)))
    sel = np.concatenate(blocks, axis=0).astype(np.float32)
    sel2 = np.concatenate([sel, sel], axis=1)
    s = r
    level = np.full((c, c), N_LEVELS + 1, np.int32)
    level[np.broadcast_to(s == t, (c, c))] = N_LEVELS
    x = t ^ s
    for l in range(N_LEVELS):
        hit = (s < t) & ((x >> l) == 1)
        level[hit] = l
    return sel2, level


def _block_mask(rows, cols, rblk, cblk):
    r = np.arange(rows)[:, None] // rblk
    c = np.arange(cols)[None, :] // cblk
    return (r == c).astype(np.float32)


def _rope_tables(pos0, length):
    d = RET_DK
    inv_freq = ROPE_BASE ** (-jnp.arange(0, d, 2, dtype=F32) / d)
    pos = pos0 + jnp.arange(length, dtype=F32)
    ang = pos[:, None] * inv_freq[None, :]
    cos, sin = jnp.cos(ang), jnp.sin(ang)
    cos_full = jnp.tile(jnp.concatenate([cos, cos], axis=-1), (1, RET_HEADS))
    sin_signed = jnp.tile(jnp.concatenate([-sin, sin], axis=-1), (1, RET_HEADS))
    return cos_full, sin_signed


def _retention_tables(log_gamma):
    c = CHUNK
    t = jnp.arange(c, dtype=F32)
    delta = t[:, None] - t[None, :]
    decay = jnp.where(delta >= 0, jnp.exp(jnp.maximum(delta, 0.0) * log_gamma[:, None, None]), 0.0)
    dmask = jnp.transpose(decay, (1, 0, 2)).reshape(c, RET_HEADS * c)
    q_decay = jnp.exp((t + 1.0)[None, :] * log_gamma[:, None])
    k_decay = jnp.exp((c - 1.0 - t)[None, :] * log_gamma[:, None])
    qdec = jnp.repeat(q_decay.T, RET_DK, axis=1)
    kdec = jnp.repeat(k_decay.T, RET_DK, axis=1)
    chunk_decay = jnp.exp(c * log_gamma)
    cd = jnp.repeat(chunk_decay, RET_DK).reshape(RET_HEADS // RET_GROUP, MXU_W, 1)
    cd = jnp.broadcast_to(cd, (RET_HEADS // RET_GROUP, MXU_W, MXU_W))
    return dmask, qdec, kdec, cd


def _rms(x, g):
    ms = jnp.mean(x * x, axis=-1, keepdims=True)
    return x * lax.rsqrt(ms + EPS) * g


def _silu(x):
    return x * jax.nn.sigmoid(x)


def _dot(a, b):
    return jnp.dot(a, b, preferred_element_type=F32)


def _dot_nt(a, b):
    return lax.dot_general(a, b, (((1,), (1,)), ((), ())), preferred_element_type=F32)


def _dot_tn(a, b):
    return lax.dot_general(a, b, (((0,), (0,)), ((), ())), preferred_element_type=F32)


def _rotary(x, cos_full, sin_signed):
    swapped = jnp.concatenate(
        [pltpu.roll(x[:, RET_DK * h:RET_DK * (h + 1)], RET_DK // 2, 1) for h in range(RET_HEADS)], axis=1)
    return x * cos_full + swapped * sin_signed


def _hgrn_inputs(hq, hf, lb):
    q = _silu(hq)
    f = lb + (1.0 - lb) * jax.nn.sigmoid(hf)
    return q, f, 1.0 - f


def _merge_branches(x, o_hg, o_ret, u_hg, u_rg, u_gab, hm_hg, hm_ret, g_hg, g_ret, w_br_hg, w_br_ret, w_out):
    ms = _dot((o_hg * o_hg).astype(BF16), hm_hg)
    a_hg = o_hg * lax.rsqrt(ms + EPS) * g_hg * _silu(u_hg)
    y_hg = _dot(a_hg.astype(BF16), w_br_hg)
    ms = _dot((o_ret * o_ret).astype(BF16), hm_ret)
    a_ret = o_ret * lax.rsqrt(ms + EPS) * g_ret * _silu(u_rg)
    y_ret = _dot(a_ret.astype(BF16), w_br_ret)
    mixed = jax.nn.sigmoid(u_gab[:, :D_MODEL]) * y_hg + jax.nn.sigmoid(u_gab[:, D_MODEL:]) * y_ret
    return x + _dot(mixed.astype(BF16), w_out)


def _ffn_tail(h, ac, gate_half, p, w_down, g_ple, w_ple_gate, w_ple, g_final):
    h = h + _dot((jax.nn.gelu(ac) * gate_half).astype(BF16), w_down)
    gate = jax.nn.sigmoid(_dot(_rms(h, g_ple).astype(BF16), w_ple_gate))
    h = h + gate * _dot(p.astype(BF16), w_ple)
    return _rms(h, g_final)


def _mix_prompt_kernel(x_ref, cos_ref, sin_ref, w_in_ref, g_mix_ref, lb_ref, sel_ref, lvl_ref,
                       bd64_ref, bd64f_ref, bdret_ref, bd128f_ref, dmask_ref, qdec_ref, kdec_ref, cd_ref,
                       hm_hg_ref, hm_ret_ref, g_hg_ref, g_ret_ref, w_br_hg_ref, w_br_ret_ref, w_out_ref,
                       h_ref, st_hg_out_ref, st_ret_out_ref,
                       xn_s, q_s, kk_s, lfh_s, lfl_s, v_s, qr_s, kr_s, vr_s, o_hg_s, o_ret_s, gate_s, st_hg, st_ret):
    t_idx = pl.program_id(1)
    n_hg_groups = HG_HEADS // HG_GROUP
    n_ret_groups = RET_HEADS // RET_GROUP

    @pl.when(t_idx == 0)
    def _():
        st_hg[...] = jnp.zeros_like(st_hg)
        st_ret[...] = jnp.zeros_like(st_ret)

    subs = [slice(i * MIX_SUB, (i + 1) * MIX_SUB) for i in range(MIX_TILE // MIX_SUB)]
    chunk_sums = []
    for r in subs:
        xn = _rms(x_ref[r, :], g_mix_ref[...]).astype(BF16)
        xn_s[r, :] = xn

        u = _dot(xn, w_in_ref[:, C_HQ:C_HG])
        q, f, kk = _hgrn_inputs(u[:, 0:HG_W], u[:, HG_W:2 * HG_W], lb_ref[...])
        q_s[r, :] = q
        kk_s[r, :] = kk
        lf = jnp.log(f)
        lf_hi = lf.astype(BF16)
        lfh_s[r, :] = lf_hi
        lfl_s[r, :] = (lf - lf_hi.astype(F32)).astype(BF16)
        v_s[r, :] = u[:, 2 * HG_W:3 * HG_W].astype(BF16)
        chunk_sums += [jnp.sum(lf[c * CHUNK:(c + 1) * CHUNK, :], axis=0, keepdims=True)
                       for c in range(MIX_SUB // CHUNK)]

        u = _dot(xn, w_in_ref[:, C_RQ:C_RG])
        cos_full = cos_ref[r, :]
        sin_signed = sin_ref[r, :]
        qr_s[r, :] = _rotary(u[:, 0:RET_W], cos_full, sin_signed)
        kr_s[r, :] = _rotary(u[:, RET_W:2 * RET_W], cos_full, sin_signed) * (RET_DK ** -0.5)
        vr_s[r, :] = u[:, 2 * RET_W:3 * RET_W].astype(BF16)

    mild =jnp.min(functools.reduce(jnp.minimum, chunk_sums)) >= -FAST_DECAY_LIMIT

    def chunk_body(c, fast):
        rows = slice(c * CHUNK, (c + 1) * CHUNK)

        q = q_s[rows, :]
        kk = kk_s[rows, :]
        v = v_s[rows, :]
        lf2 = jnp.concatenate([lfh_s[rows, :], lfl_s[rows, :]], axis=0)
        x_cs = _dot(sel_ref[0:2 * CHUNK, :], lf2)
        e_cs = jnp.exp(x_cs)
        lvl = lvl_ref[...]
        bd64 = bd64_ref[...]
        bd64f = bd64f_ref[...]
        groups = [slice(MXU_W * g, MXU_W * (g + 1)) for g in range(n_hg_groups)]
        scores_of = []
        if fast:
            e_neg = jnp.exp(-x_cs[0:CHUNK, :])
            for cs in groups:
                qh = (q[:, cs] * e_cs[0:CHUNK, cs]).astype(BF16)
                kh = (kk[:, cs] * e_neg[:, cs]).astype(BF16)
                rhs = jnp.concatenate([kh] * HG_GROUP, axis=0) * bd64
                scores_of.append(jnp.where(lvl <= N_LEVELS, _dot_nt(qh, rhs), 0.0))
        else:
            ex = jnp.exp(_dot(sel_ref[2 * CHUNK:, :], lf2))
            for cs in groups:
                scores = jnp.zeros((CHUNK, MXU_W), F32)
                for l in range(N_LEVELS + 1):
                    if l < N_LEVELS:
                        e = ex[CHUNK * l:CHUNK * (l + 1), cs]
                        ql = (q[:, cs] * e).astype(BF16)
                        kl = (kk[:, cs] * e).astype(BF16)
                    else:
                        ql = q[:, cs].astype(BF16)
                        kl = kk[:, cs].astype(BF16)
                    rhs = jnp.concatenate([kl] * HG_GROUP, axis=0) * bd64
                    scores = jnp.where(lvl == l, _dot_nt(ql, rhs), scores)
                scores_of.append(scores)

        o_parts = []
        for g, cs in enumerate(groups):
            v_bd = jnp.concatenate([v[:, cs]] * HG_GROUP, axis=0) * bd64
            o_intra = _dot(scores_of[g].astype(BF16), v_bd)
            e_cum = e_cs[0:CHUNK, cs]
            e_suf = e_cs[CHUNK:2 * CHUNK, cs]
            st = st_hg[g]
            o_inter = _dot_nt((q[:, cs] * e_cum).astype(BF16), st.astype(BF16))
            o_parts.append(o_intra + o_inter)
            upd = _dot_tn(v[:, cs], (kk[:, cs] * e_suf).astype(BF16))
            st_hg[g] = st * e_cum[CHUNK - 1:CHUNK, :] + upd * bd64f
        o_hg_s[rows, :] = jnp.concatenate(o_parts, axis=1)

        qr = qr_s[rows, :]
        kr = kr_s[rows, :]
        vr = vr_s[rows, :]
        bdret = bdret_ref[...]
        k_bd = jnp.concatenate([kr.astype(BF16)] * RET_HEADS, axis=0) * bdret
        scores = _dot_nt(qr.astype(BF16), k_bd) * dmask_ref[...]
        v_bd = jnp.concatenate([vr] * RET_HEADS, axis=0) * bdret
        o_ret = _dot(scores.astype(BF16), v_bd)
        qd = (qr * qdec_ref[...]).astype(BF16)
        kd = (kr * kdec_ref[...]).astype(BF16)
        bd128f = bd128f_ref[...]
        o_parts = []
        for g in range(n_ret_groups):
            cs = slice(MXU_W * g, MXU_W * (g + 1))
            st = st_ret[g]
            o_parts.append(_dot(qd[:, cs], st.astype(BF16)))
            upd = _dot_tn(kd[:, cs], vr[:, cs])
            st_ret[g] = cd_ref[g] * st + upd * bd128f
        o_ret_s[rows, :] = o_ret + jnp.concatenate(o_parts, axis=1)

    def recurrences(fast):
        for c in range(MIX_TILE // CHUNK):
            chunk_body(c, fast)
            r = subs[c * CHUNK // MIX_SUB]
            for src, width, dst in GATE_PIECES[c % len(GATE_PIECES)]:
                gate_s[r, dst:dst + width] = _dot(xn_s[r, :], w_in_ref[:, src:src + width])

    pl.when(mild)(functools.partial(recurrences, True))
    pl.when(jnp.logical_not(mild))(functools.partial(recurrences, False))

    for r in subs:
        h_ref[r, :] = _merge_branches(
            x_ref[r, :], o_hg_s[r, :], o_ret_s[r, :],
            gate_s[r, 0:HG_W], gate_s[r, HG_W:HG_W + RET_W], gate_s[r, HG_W + RET_W:],
            hm_hg_ref[...], hm_ret_ref[...], g_hg_ref[...], g_ret_ref[...],
            w_br_hg_ref[...], w_br_ret_ref[...], w_out_ref[...])

    @pl.when(t_idx == pl.num_programs(1) - 1)
    def _():
        for g in range(n_hg_groups):
            st_t = st_hg[g].T
            for h in range(HG_GROUP):
                blk = slice(HG_FDIM * h, HG_FDIM * (h + 1))
                st_hg_out_ref[HG_GROUP * g + h] = st_t[blk, blk]
        for g in range(n_ret_groups):
            for h in range(RET_GROUP):
                blk = slice(RET_DK * h, RET_DK * (h + 1))
                st_ret_out_ref[RET_GROUP * g + h] = st_ret[g, blk, blk]


def _const_spec(shape):
    zeros = (0,) * len(shape)
    return pl.BlockSpec(shape, lambda *_: zeros, pipeline_mode=pl.Buffered(1))


def _mix_prompt(x, cos_full, sin_signed, w_in, g_mix, lb, consts, hm_hg, hm_ret, g_hg, g_ret,
                w_br_hg, w_br_ret, w_out):
    b, l, d = x.shape
    nt = l // MIX_TILE
    tile = lambda w: pl.BlockSpec((None, MIX_TILE, w), lambda i, j: (i, j, 0))
    table = lambda w: pl.BlockSpec((MIX_TILE, w), lambda i, j: (j, 0))
    n_hg_groups = HG_HEADS // HG_GROUP
    n_ret_groups = RET_HEADS // RET_GROUP
    const_args = (w_in, g_mix, lb) + tuple(consts) + (hm_hg, hm_ret, g_hg, g_ret, w_br_hg, w_br_ret, w_out)
    state_spec = lambda h, dk, dv: pl.BlockSpec((None, None, h, dk, dv), lambda i, j: (0, i, 0, 0, 0))
    return pl.pallas_call(
        _mix_prompt_kernel,
        grid=(b, nt),
        in_specs=[tile(d), table(RET_W), table(RET_W)] + [_const_spec(a.shape) for a in const_args],
        out_specs=[tile(d), state_spec(HG_HEADS, HG_FDIM, HG_IDIM), state_spec(RET_HEADS, RET_DK, RET_DV)],
        out_shape=[jax.ShapeDtypeStruct((b, l, d), F32),
                   jax.ShapeDtypeStruct((1, b, HG_HEADS, HG_FDIM, HG_IDIM), F32),
                   jax.ShapeDtypeStruct((1, b, RET_HEADS, RET_DK, RET_DV), F32)],
        scratch_shapes=[
            pltpu.VMEM((MIX_TILE, d), BF16),
            pltpu.VMEM((MIX_TILE, HG_W), F32),
            pltpu.VMEM((MIX_TILE, HG_W), F32),
            pltpu.VMEM((MIX_TILE, HG_W), BF16),
            pltpu.VMEM((MIX_TILE, HG_W), BF16),
            pltpu.VMEM((MIX_TILE, HG_W), BF16),
            pltpu.VMEM((MIX_TILE, RET_W), F32),
            pltpu.VMEM((MIX_TILE, RET_W), F32),
            pltpu.VMEM((MIX_TILE, RET_W), BF16),
            pltpu.VMEM((MIX_TILE, HG_W), F32),
            pltpu.VMEM((MIX_TILE, RET_W), F32),
            pltpu.VMEM((MIX_TILE, HG_W + RET_W + 2 * D_MODEL), F32),
            pltpu.VMEM((n_hg_groups, MXU_W, MXU_W), F32),
            pltpu.VMEM((n_ret_groups, MXU_W, MXU_W), F32),
        ],
        compiler_params=pltpu.CompilerParams(
            dimension_semantics=("arbitrary", "arbitrary"), vmem_limit_bytes=V7X_VMEM_LIMIT),
        name="mix_prompt",
    )(x, cos_full, sin_signed, *const_args)


def _ffn_prompt_kernel(h_ref, p_ref, g_ffn_ref, w_up_ref, conv_w_ref, conv_b_ref, w_down_ref,
                       g_ple_ref, w_ple_gate_ref, w_ple_ref, g_final_ref,
                       y_ref, conv_out_ref, a_ext, gate_s):
    t_idx = pl.program_id(1)
    tl = FFN_TILE

    @pl.when(t_idx == 0)
    def _():
        a_ext[0:CARRY_ROWS, :] = jnp.zeros((CARRY_ROWS, D_FF), F32)

    @pl.when(t_idx > 0)
    def _():
        a_ext[0:CARRY_ROWS, :] = a_ext[tl:tl + CARRY_ROWS, :]

    subs = [slice(i * FFN_SUB, (i + 1) * FFN_SUB) for i in range(tl // FFN_SUB)]
    for r in subs:
        xn = _rms(h_ref[r, :], g_ffn_ref[...]).astype(BF16)
        a_ext[CARRY_ROWS + r.start:CARRY_ROWS + r.stop, :] = _dot(xn, w_up_ref[:, 0:D_FF])
        gate_s[r, :] = _dot(xn, w_up_ref[:, D_FF:2 * D_FF])
    cw = conv_w_ref[...]
    for r in subs:
        ac = conv_b_ref[...]
        for j in range(CONV_W):
            off = CARRY_ROWS - (CONV_W - 1) + j
            ac = ac + a_ext[off + r.start:off + r.stop, :] * cw[j:j + 1, :]
        y_ref[r, :] = _ffn_tail(h_ref[r, :], ac, gate_s[r, :], p_ref[r, :], w_down_ref[...], g_ple_ref[...],
                                w_ple_gate_ref[...], w_ple_ref[...], g_final_ref[...])

    @pl.when(t_idx == pl.num_programs(1) - 1)
    def _():
        conv_out_ref[...] = a_ext[CARRY_ROWS + tl - (CONV_W - 1):CARRY_ROWS + tl, :]


def _ffn_prompt(h, p, g_ffn, w_up, conv_w, conv_b, w_down, g_ple, w_ple_gate, w_ple, g_final):
    b, l, d = h.shape
    nt = l // FFN_TILE
    tile = lambda w: pl.BlockSpec((None, FFN_TILE, w), lambda i, j: (i, j, 0))
    const_args = (g_ffn, w_up, conv_w, conv_b, w_down, g_ple, w_ple_gate, w_ple, g_final)
    return pl.pallas_call(
        _ffn_prompt_kernel,
        grid=(b, nt),
        in_specs=[tile(d), tile(PLE_DIM)] + [_const_spec(a.shape) for a in const_args],
        out_specs=[tile(d), pl.BlockSpec((None, CONV_W - 1, D_FF), lambda i, j: (i, 0, 0))],
        out_shape=[jax.ShapeDtypeStruct((b, l, d), F32),
                   jax.ShapeDtypeStruct((b, CONV_W - 1, D_FF), F32)],
        scratch_shapes=[pltpu.VMEM((FFN_TILE + CARRY_ROWS, D_FF), F32), pltpu.VMEM((FFN_TILE, D_FF), F32)],
        compiler_params=pltpu.CompilerParams(
            dimension_semantics=("arbitrary", "arbitrary"), vmem_limit_bytes=V7X_VMEM_LIMIT),
        name="ffn_prompt",
    )(h, p, *const_args)


def _proj_sample_kernel(x_ref, cos_ref, sin_ref, w_in_ref, g_mix_ref, lb_ref,
                        f_ref, kk_ref, q_ref, v_ref, kr_ref, qr_ref, vr_ref, gates_ref):
    xn = _rms(x_ref[...], g_mix_ref[...]).astype(BF16)
    u = _dot(xn, w_in_ref[:, C_HQ:C_HG])
    q, f, kk = _hgrn_inputs(u[:, 0:HG_W], u[:, HG_W:2 * HG_W], lb_ref[...])
    f_ref[...] = f.T
    kk_ref[...] = kk.T
    q_ref[...] = q.T
    v_ref[...] = u[:, 2 * HG_W:3 * HG_W].T
    u = _dot(xn, w_in_ref[:, C_RQ:C_RG])
    cos_full = cos_ref[...]
    sin_signed = sin_ref[...]
    qr_ref[...] = _rotary(u[:, 0:RET_W], cos_full, sin_signed)
    kr_ref[...] = _rotary(u[:, RET_W:2 * RET_W], cos_full, sin_signed) * (RET_DK ** -0.5)
    vr_ref[...] = u[:, 2 * RET_W:3 * RET_W]
    gates_ref[:, 0:HG_W] = _dot(xn, w_in_ref[:, C_HG:C_RQ])
    gates_ref[:, HG_W:HG_W + RET_W] = _dot(xn, w_in_ref[:, C_RG:C_GA])
    gates_ref[:, HG_W + RET_W:] = _dot(xn, w_in_ref[:, C_GA:C_END])


def _proj_sample(x, cos_full, sin_signed, w_in, g_mix, lb):
    n = x.shape[0]
    col = jax.ShapeDtypeStruct((HG_W, n), F32)
    row = jax.ShapeDtypeStruct((n, RET_W), F32)
    return pl.pallas_call(
        _proj_sample_kernel,
        out_shape=[col] * 4 + [row] * 3 + [jax.ShapeDtypeStruct((n, HG_W + RET_W + 2 * D_MODEL), F32)],
        compiler_params=pltpu.CompilerParams(vmem_limit_bytes=V7X_VMEM_LIMIT),
        name="proj_sample",
    )(x, cos_full, sin_signed, w_in, g_mix, lb)


def _state_tokens_minor_kernel(s_ref, a_t_ref, k_t_ref, q_t_ref, v_t_ref, s_out_ref, o_t_ref):
    dk, dv, _ = s_ref.shape
    row0 = pl.program_id(0) * dk
    v_blk = v_t_ref[...]

    def body(k, acc):
        s_new = a_t_ref[pl.ds(row0 + k, 1), :] * s_ref[k] + k_t_ref[pl.ds(row0 + k, 1), :] * v_blk
        s_out_ref[k] = s_new
        return acc + q_t_ref[pl.ds(row0 + k, 1), :] * s_new

    o_t_ref[...] = lax.fori_loop(0, dk, body, jnp.zeros(v_blk.shape, F32), unroll=SUBLANES)


def _state_update_tokens_minor(state_t, layer, a_t, k_t, q_t, v_t, *, name):
    _, heads, dk, dv, n = state_t.shape
    full = lambda a: pl.BlockSpec(a.shape, lambda h: (0, 0))
    per_head = pl.BlockSpec((dv, n), lambda h: (h, 0))
    slab = lambda lead: pl.BlockSpec((None, None, dk, dv, n), lambda h: (lead, h, 0, 0, 0))
    return pl.pallas_call(
        _state_tokens_minor_kernel,
        grid=(heads,),
        in_specs=[slab(layer), full(a_t), full(k_t), full(q_t), per_head],
        out_specs=[slab(0), per_head],
        out_shape=[jax.ShapeDtypeStruct((1,) + state_t.shape[1:], F32), jax.ShapeDtypeStruct(v_t.shape, F32)],
        compiler_params=pltpu.CompilerParams(
            dimension_semantics=("arbitrary",), vmem_limit_bytes=V7X_VMEM_LIMIT),
        name=name,
    )(state_t, a_t, k_t, q_t, v_t)


def _state_kernel(s_ref, a_ref, k_ref, q_ref, v_ref, s_out_ref, o_ref, *, dk, dv, kb):
    j = pl.program_id(0)
    n = s_ref.shape[0]

    @pl.when(j == 0)
    def _():
        o_ref[...] = jnp.zeros_like(o_ref)

    group0 = j * (kb // SUBLANES)
    head_row = pl.multiple_of((j * kb // dk) * dv, dv)
    v_blk = v_ref[pl.ds(head_row, dv), :]
    acc = jnp.zeros(v_blk.shape, F32)
    for g in range(kb // SUBLANES):
        rows = slice(SUBLANES * g, SUBLANES * (g + 1))
        s_t = s_ref[:, rows, :].reshape(n * SUBLANES, dv).T
        s_new = a_ref[pl.ds(group0 + g, 1), :] * s_t + k_ref[pl.ds(group0 + g, 1), :] * v_blk
        s_out_ref[:, rows, :] = s_new.T.reshape(n, SUBLANES, dv)
        acc = acc + q_ref[pl.ds(group0 + g, 1), :] * s_new
    o_ref[pl.ds(head_row, dv), :] += acc


def _state_update(state, layer, a_il, k_il, q_il, v_rep, *, kb, name):
    _, n, heads, dk, dv = state.shape
    per_head = dk // kb
    full = lambda a: pl.BlockSpec(a.shape, lambda j: (0, 0))
    slab = lambda lead: pl.BlockSpec((None, n, None, kb, dv), lambda j: (lead, 0, j // per_head, j % per_head, 0))
    return pl.pallas_call(
        functools.partial(_state_kernel, dk=dk, dv=dv, kb=kb),
        grid=(heads * per_head,),
        in_specs=[slab(layer), full(a_il), full(k_il), full(q_il), full(v_rep)],
        out_specs=[slab(0), full(v_rep)],
        out_shape=[jax.ShapeDtypeStruct((1,) + state.shape[1:], F32), jax.ShapeDtypeStruct(v_rep.shape, F32)],
        compiler_params=pltpu.CompilerParams(
            dimension_semantics=("arbitrary",), vmem_limit_bytes=V7X_VMEM_LIMIT),
        name=name,
    )(state, a_il, k_il, q_il, v_rep)


def _lane_group_sum(o_ref, sel):
    o = o_ref[...]
    hi = o.astype(BF16)
    lo = (o - hi.astype(F32)).astype(BF16)
    return _dot_nt(sel, hi) + _dot_nt(sel, lo)


def _tail_sample_kernel(x_ref, o_hg_t_ref, o_ret_ref, sel_ref, gates_ref, conv0_ref, p_ref,
                        hm_hg_ref, hm_ret_ref, g_hg_ref, g_ret_ref, w_br_hg_ref, w_br_ret_ref, w_out_ref,
                        g_ffn_ref, w_up_ref, conv_w_ref, conv_b_ref, w_down_ref,
                        g_ple_ref, w_ple_gate_ref, w_ple_ref, g_final_ref,
                        y_ref, conv_out_ref):
    h = _merge_branches(
        x_ref[...], o_hg_t_ref[...].T, _lane_group_sum(o_ret_ref, sel_ref[...]),
        gates_ref[:, 0:HG_W], gates_ref[:, HG_W:HG_W + RET_W], gates_ref[:, HG_W + RET_W:],
        hm_hg_ref[...], hm_ret_ref[...], g_hg_ref[...], g_ret_ref[...],
        w_br_hg_ref[...], w_br_ret_ref[...], w_out_ref[...])
    xn = _rms(h, g_ffn_ref[...]).astype(BF16)
    a = _dot(xn, w_up_ref[:, 0:D_FF])
    gate_half = _dot(xn, w_up_ref[:, D_FF:2 * D_FF])
    cw = conv_w_ref[...]
    prev = conv0_ref[:, 1, :]
    ac = conv_b_ref[...] + conv0_ref[:, 0, :] * cw[0:1, :] + prev * cw[1:2, :] + a * cw[2:3, :]
    conv_out_ref[0, :, 0, :] = prev
    conv_out_ref[0, :, 1, :] = a
    y_ref[...] = _ffn_tail(h, ac, gate_half, p_ref[...], w_down_ref[...], g_ple_ref[...],
                           w_ple_gate_ref[...], w_ple_ref[...], g_final_ref[...])


def _tail_sample(x, o_hg_t, o_ret, sel, gates, conv0, p, *weights):
    n = x.shape[0]
    return pl.pallas_call(
        _tail_sample_kernel,
        out_shape=[jax.ShapeDtypeStruct((n, D_MODEL), F32),
                   jax.ShapeDtypeStruct((1, n, CONV_W - 1, D_FF), F32)],
        compiler_params=pltpu.CompilerParams(vmem_limit_bytes=V7X_VMEM_LIMIT),
        name="tail_sample",
    )(x, o_hg_t, o_ret, sel, gates, conv0, p, *weights)


def kernel(x_prompt, x_sample, state_hgrn, state_ret, state_conv, p_prompt, p_sample, lb_logits, w_in, g_mix, g_hg_out, g_ret_out, w_br_hg, w_br_ret, w_out, g_ffn, w_up, conv_w, conv_b, w_down, g_ple, w_ple, w_ple_gate, g_final):
    assert DEPTH == 1 and x_prompt.shape == (BATCH, SEQ, D_MODEL) and x_sample.shape == (DEC_BATCH, DEC_SEQ, D_MODEL)
    i = 0
    row = lambda a: a.reshape(1, -1).astype(F32)
    log_gamma = jnp.log1p(-jnp.exp2(-5.0 - jnp.arange(RET_HEADS, dtype=F32)))
    lb = jnp.cumsum(jax.nn.softmax(lb_logits.astype(F32), axis=0), axis=0)[i].reshape(1, HG_W)

    sel2, level = _level_tables()
    consts = (
        jnp.asarray(sel2, BF16),
        jnp.asarray(np.tile(level, (1, HG_GROUP))),
        jnp.asarray(_block_mask(MXU_W, MXU_W, CHUNK, HG_FDIM), BF16),
        jnp.asarray(_block_mask(MXU_W, MXU_W, HG_IDIM, HG_FDIM), F32),
        jnp.asarray(_block_mask(RET_HEADS * CHUNK, RET_W, CHUNK, RET_DK), BF16),
        jnp.asarray(_block_mask(MXU_W, MXU_W, RET_DK, RET_DV), F32),
    ) + _retention_tables(log_gamma)
    hm_hg = jnp.asarray(_block_mask(HG_W, HG_W, HG_IDIM, HG_IDIM) / HG_IDIM, BF16)
    hm_ret = jnp.asarray(_block_mask(RET_W, RET_W, RET_DV, RET_DV) / RET_DV, BF16)

    w_in_b = w_in[i].astype(BF16)
    mix_w = (hm_hg, hm_ret, row(g_hg_out[i]), row(g_ret_out[i]),
             w_br_hg[i].astype(BF16), w_br_ret[i].astype(BF16), w_out[i].astype(BF16))
    ffn_w = (row(g_ffn[i]), w_up[i].astype(BF16), conv_w[i].astype(F32), row(conv_b[i]), w_down[i].astype(BF16),
             row(g_ple[i]), w_ple_gate[i].astype(BF16), w_ple[i].astype(BF16), row(g_final))

    cos_p, sin_p = _rope_tables(0.0, SEQ)
    h_p, hg_state_p, ret_state_p = _mix_prompt(x_prompt, cos_p, sin_p, w_in_b, row(g_mix[i]), lb, consts, *mix_w)
    y_prompt, conv_p = _ffn_prompt(h_p, p_prompt[i], *ffn_w)

    n = DEC_BATCH
    xs = x_sample.reshape(n, D_MODEL)
    cos_s, sin_s = _rope_tables(float(PAST_LEN), DEC_SEQ)
    f_t, kk_t, q_t, v_t, kr, qr, vr, gates = _proj_sample(xs, cos_s, sin_s, w_in_b, row(g_mix[i]), lb)
    hg_new_t, o_hg_t = _state_update_tokens_minor(
        jnp.transpose(state_hgrn, (0, 2, 3, 4, 1)), i, f_t, kk_t, q_t, v_t, name="state_hgrn")
    hg_new = jnp.transpose(hg_new_t, (0, 4, 1, 2, 3))
    interleave = lambda a: a.reshape(n, -1, SUBLANES).transpose(1, 0, 2).reshape(-1, n * SUBLANES)
    repeat = lambda a: jnp.repeat(a.T, SUBLANES, axis=1)
    gamma = jnp.broadcast_to(jnp.repeat(jnp.exp(log_gamma), RET_DK // SUBLANES)[:, None],
                             (RET_W // SUBLANES, n * SUBLANES))
    ret_new, o_ret = _state_update(
        state_ret, i, gamma, interleave(kr), interleave(qr), repeat(vr), kb=RET_DK // 2, name="state_ret")
    sel = jnp.asarray(_block_mask(n, n * SUBLANES, 1, SUBLANES), BF16)
    y_sample, conv_s = _tail_sample(
        xs, o_hg_t, o_ret, sel, gates, state_conv[i], p_sample[i].reshape(n, PLE_DIM),
        *mix_w, *ffn_w)

    return (y_prompt, y_sample.reshape(n, DEC_SEQ, D_MODEL),
            hg_state_p, ret_state_p, conv_p[None],
            hg_new, ret_new,
            conv_s)
```

```python
import functools

import numpy as np
import jax
import jax.numpy as jnp
from jax import lax
from jax.experimental import pallas as pl
from jax.experimental.pallas import tpu as pltpu

D_MODEL = 1024
BATCH = 8
SEQ = 2048
DEPTH = 1
DEC_BATCH = 128
DEC_SEQ = 1
PAST_LEN = 16384
HG_HEADS = 8
HG_FDIM = 64
HG_IDIM = 64
HG_W = HG_HEADS * HG_FDIM
RET_HEADS = 4
RET_DK = 128
RET_DV = 128
RET_W = RET_HEADS * RET_DK
D_FF = 2816
CONV_W = 3
PLE_DIM = 256
CHUNK = 64
ROPE_BASE = 10000.0
EPS = 1e-6

C_HQ, C_HF, C_HI, C_HG = 0, 512, 1024, 1536
C_RQ, C_RK, C_RV, C_RG = 2048, 2560, 3072, 3584
C_GA, C_GB, C_END = 4096, 5120, 6144

N_LEVELS = 6
FAST_DECAY_LIMIT = 80.0
HG_GROUP = 4
RET_GROUP = 2
MXU_W = 256
SUBLANES = 8

MIX_TILE = 512
MIX_SUB = 256
GATE_PIECES = (
    ((C_HG, HG_W, 0), (C_RG, RET_W, HG_W)),
    ((C_GA, 768, HG_W + RET_W),),
    ((C_GA + 768, 768, HG_W + RET_W + 768),),
    ((C_GA + 1536, 512, HG_W + RET_W + 1536),),
)
FFN_TILE = 512
FFN_SUB = 256
CARRY_ROWS = 8
V7X_VMEM_LIMIT = 56 * 1024 * 1024

F32 = jnp.float32
BF16 = jnp.bfloat16


def _level_tables():
    c = CHUNK
    t = np.arange(c)[:, None]
    r = np.arange(c)[None, :]
    blocks = [r <= t, r > t]
    for l in range(N_LEVELS):
        same = (t >> l) == (r >> l)
        upper = ((t >> l) & 1) == 1
        blocks.append(np.where(upper, same & (r <= t), same & (r > t)))
    sel = np.concatenate(blocks, axis=0).astype(np.float32)
    sel2 = np.concatenate([sel, sel], axis=1)
    s = r
    level = np.full((c, c), N_LEVELS + 1, np.int32)
    level[np.broadcast_to(s == t, (c, c))] = N_LEVELS
    x = t ^ s
    for l in range(N_LEVELS):
        hit = (s < t) & ((x >> l) == 1)
        level[hit] = l
    return sel2, level


def _block_mask(rows, cols, rblk, cblk):
    r = np.arange(rows)[:, None] // rblk
    c = np.arange(cols)[None, :] // cblk
    return (r == c).astype(np.float32)


def _rope_tables(pos0, length):
    d = RET_DK
    inv_freq = ROPE_BASE ** (-jnp.arange(0, d, 2, dtype=F32) / d)
    pos = pos0 + jnp.arange(length, dtype=F32)
    ang = pos[:, None] * inv_freq[None, :]
    cos, sin = jnp.cos(ang), jnp.sin(ang)
    cos_full = jnp.tile(jnp.concatenate([cos, cos], axis=-1), (1, RET_HEADS))
    sin_signed = jnp.tile(jnp.concatenate([-sin, sin], axis=-1), (1, RET_HEADS))
    return cos_full, sin_signed


def _retention_tables(log_gamma):
    c = MIX_SUB
    t = jnp.arange(c, dtype=F32)
    delta = t[:, None] - t[None, :]
    dmask = jnp.where(delta >= 0, jnp.exp(jnp.maximum(delta, 0.0) * log_gamma[:, None, None]), 0.0)
    q_decay = jnp.exp((t + 1.0)[None, :] * log_gamma[:, None])
    k_decay = jnp.exp((c - 1.0 - t)[None, :] * log_gamma[:, None])
    qdec = jnp.repeat(q_decay.T, RET_DK, axis=1)
    kdec = jnp.repeat(k_decay.T, RET_DK, axis=1)
    chunk_decay = jnp.exp(c * log_gamma)
    cd = jnp.repeat(chunk_decay, RET_DK).reshape(RET_HEADS // RET_GROUP, MXU_W, 1)
    cd = jnp.broadcast_to(cd, (RET_HEADS // RET_GROUP, MXU_W, MXU_W))
    return dmask, qdec, kdec, cd


def _rms(x, g):
    ms = jnp.mean(x * x, axis=-1, keepdims=True)
    return x * lax.rsqrt(ms + EPS) * g


def _silu(x):
    return x * jax.nn.sigmoid(x)


def _dot(a, b):
    return jnp.dot(a, b, preferred_element_type=F32)


def _dot_nt(a, b):
    return lax.dot_general(a, b, (((1,), (1,)), ((), ())), preferred_element_type=F32)


def _dot_tn(a, b):
    return lax.dot_general(a, b, (((0,), (0,)), ((), ())), preferred_element_type=F32)


def _rotary(x, cos_full, sin_signed):
    swapped = jnp.concatenate(
        [pltpu.roll(x[:, RET_DK * h:RET_DK * (h + 1)], RET_DK // 2, 1) for h in range(RET_HEADS)], axis=1)
    return x * cos_full + swapped * sin_signed


def _hgrn_inputs(hq, hf, lb):
    q = _silu(hq)
    f = lb + (1.0 - lb) * jax.nn.sigmoid(hf)
    return q, f, 1.0 - f


def _merge_branches(x, o_hg, o_ret, u_hg, u_rg, u_gab, hm_hg, hm_ret, g_hg, g_ret, w_br_hg, w_br_ret, w_out):
    ms = _dot((o_hg * o_hg).astype(BF16), hm_hg)
    a_hg = o_hg * lax.rsqrt(ms + EPS) * g_hg * _silu(u_hg)
    y_hg = _dot(a_hg.astype(BF16), w_br_hg)
    ms = _dot((o_ret * o_ret).astype(BF16), hm_ret)
    a_ret = o_ret * lax.rsqrt(ms + EPS) * g_ret * _silu(u_rg)
    y_ret = _dot(a_ret.astype(BF16), w_br_ret)
    mixed = jax.nn.sigmoid(u_gab[:, :D_MODEL]) * y_hg + jax.nn.sigmoid(u_gab[:, D_MODEL:]) * y_ret
    return x + _dot(mixed.astype(BF16), w_out)


def _ffn_tail(h, ac, gate_half, p, w_down, g_ple, w_ple_gate, w_ple, g_final):
    h = h + _dot((jax.nn.gelu(ac) * gate_half).astype(BF16), w_down)
    gate = jax.nn.sigmoid(_dot(_rms(h, g_ple).astype(BF16), w_ple_gate))
    h = h + gate * _dot(p.astype(BF16), w_ple)
    return _rms(h, g_final)


def _mix_prompt_kernel(x_ref, cos_ref, sin_ref, w_in_ref, g_mix_ref, lb_ref, sel_ref, lvl_ref,
                       bd64_ref, bd64f_ref, bd128f_ref, dmask_ref, qdec_ref, kdec_ref, cd_ref,
                       hm_hg_ref, hm_ret_ref, g_hg_ref, g_ret_ref, w_br_hg_ref, w_br_ret_ref, w_out_ref,
                       h_ref, st_hg_out_ref, st_ret_out_ref,
                       xn_s, q_s, kk_s, lfh_s, lfl_s, v_s, o_hg_s, o_ret_s, gate_s, st_hg, st_ret):
    t_idx = pl.program_id(1)
    n_hg_groups = HG_HEADS // HG_GROUP
    n_ret_groups = RET_HEADS // RET_GROUP

    @pl.when(t_idx == 0)
    def _():
        st_hg[...] = jnp.zeros_like(st_hg)
        st_ret[...] = jnp.zeros_like(st_ret)

    subs = [slice(i * MIX_SUB, (i + 1) * MIX_SUB) for i in range(MIX_TILE // MIX_SUB)]
    chunk_sums = []
    for r in subs:
        xn = _rms(x_ref[r, :], g_mix_ref[...]).astype(BF16)
        xn_s[r, :] = xn

        u = _dot(xn, w_in_ref[:, C_HQ:C_HG])
        q, f, kk = _hgrn_inputs(u[:, 0:HG_W], u[:, HG_W:2 * HG_W], lb_ref[...])
        q_s[r, :] = q
        kk_s[r, :] = kk
        lf = jnp.log(f)
        lf_hi = lf.astype(BF16)
        lfh_s[r, :] = lf_hi
        lfl_s[r, :] = (lf - lf_hi.astype(F32)).astype(BF16)
        v_s[r, :] = u[:, 2 * HG_W:3 * HG_W].astype(BF16)
        chunk_sums += [jnp.sum(lf[c * CHUNK:(c + 1) * CHUNK, :], axis=0, keepdims=True)
                       for c in range(MIX_SUB // CHUNK)]

        u = _dot(xn, w_in_ref[:, C_RQ:C_RG])
        cos_full = cos_ref[r, :]
        sin_signed = sin_ref[r, :]
        qr = _rotary(u[:, 0:RET_W], cos_full, sin_signed)
        kr = _rotary(u[:, RET_W:2 * RET_W], cos_full, sin_signed) * (RET_DK ** -0.5)
        vr = u[:, 2 * RET_W:3 * RET_W].astype(BF16)
        qrb = qr.astype(BF16)
        krb = kr.astype(BF16)
        o_heads = []
        for h in range(RET_HEADS):
            hs = slice(RET_DK * h, RET_DK * (h + 1))
            scores = _dot_nt(qrb[:, hs], krb[:, hs]) * dmask_ref[h]
            o_heads.append(_dot(scores.astype(BF16), vr[:, hs]))
        qd = (qr * qdec_ref[...]).astype(BF16)
        kd = (kr * kdec_ref[...]).astype(BF16)
        bd128f = bd128f_ref[...]
        o_inter = []
        for g in range(n_ret_groups):
            cs = slice(MXU_W * g, MXU_W * (g + 1))
            st = st_ret[g]
            o_inter.append(_dot(qd[:, cs], st.astype(BF16)))
            st_ret[g] = cd_ref[g] * st + _dot_tn(kd[:, cs], vr[:, cs]) * bd128f
        o_ret_s[r, :] = jnp.concatenate(o_heads, axis=1) + jnp.concatenate(o_inter, axis=1)

    mild =jnp.min(functools.reduce(jnp.minimum, chunk_sums)) >= -FAST_DECAY_LIMIT

    def chunk_body(c, fast):
        rows = slice(c * CHUNK, (c + 1) * CHUNK)

        q = q_s[rows, :]
        kk = kk_s[rows, :]
        v = v_s[rows, :]
        lf2 = jnp.concatenate([lfh_s[rows, :], lfl_s[rows, :]], axis=0)
        x_cs = _dot(sel_ref[0:2 * CHUNK, :], lf2)
        e_cs = jnp.exp(x_cs)
        lvl = lvl_ref[...]
        bd64 = bd64_ref[...]
        bd64f = bd64f_ref[...]
        groups = [slice(MXU_W * g, MXU_W * (g + 1)) for g in range(n_hg_groups)]
        scores_of = []
        if fast:
            e_neg = jnp.exp(-x_cs[0:CHUNK, :])
            for cs in groups:
                qh = (q[:, cs] * e_cs[0:CHUNK, cs]).astype(BF16)
                kh = (kk[:, cs] * e_neg[:, cs]).astype(BF16)
                rhs = jnp.concatenate([kh] * HG_GROUP, axis=0) * bd64
                scores_of.append(jnp.where(lvl <= N_LEVELS, _dot_nt(qh, rhs), 0.0))
        else:
            ex = jnp.exp(_dot(sel_ref[2 * CHUNK:, :], lf2))
            for cs in groups:
                scores = jnp.zeros((CHUNK, MXU_W), F32)
                for l in range(N_LEVELS + 1):
                    if l < N_LEVELS:
                        e = ex[CHUNK * l:CHUNK * (l + 1), cs]
                        ql = (q[:, cs] * e).astype(BF16)
                        kl = (kk[:, cs] * e).astype(BF16)
                    else:
                        ql = q[:, cs].astype(BF16)
                        kl = kk[:, cs].astype(BF16)
                    rhs = jnp.concatenate([kl] * HG_GROUP, axis=0) * bd64
                    scores = jnp.where(lvl == l, _dot_nt(ql, rhs), scores)
                scores_of.append(scores)

        o_parts = []
        for g, cs in enumerate(groups):
            v_bd = jnp.concatenate([v[:, cs]] * HG_GROUP, axis=0) * bd64
            o_intra = _dot(scores_of[g].astype(BF16), v_bd)
            e_cum = e_cs[0:CHUNK, cs]
            e_suf = e_cs[CHUNK:2 * CHUNK, cs]
            st = st_hg[g]
            o_inter = _dot_nt((q[:, cs] * e_cum).astype(BF16), st.astype(BF16))
            o_parts.append(o_intra + o_inter)
            upd = _dot_tn(v[:, cs], (kk[:, cs] * e_suf).astype(BF16))
            st_hg[g] = st * e_cum[CHUNK - 1:CHUNK, :] + upd * bd64f
        o_hg_s[rows, :] = jnp.concatenate(o_parts, axis=1)

    def recurrences(fast):
        for c in range(MIX_TILE // CHUNK):
            chunk_body(c, fast)
            r = subs[c * CHUNK // MIX_SUB]
            for src, width, dst in GATE_PIECES[c % len(GATE_PIECES)]:
                gate_s[r, dst:dst + width] = _dot(xn_s[r, :], w_in_ref[:, src:src + width])

    pl.when(mild)(functools.partial(recurrences, True))
    pl.when(jnp.logical_not(mild))(functools.partial(recurrences, False))

    for r in subs:
        h_ref[r, :] = _merge_branches(
            x_ref[r, :], o_hg_s[r, :], o_ret_s[r, :],
            gate_s[r, 0:HG_W], gate_s[r, HG_W:HG_W + RET_W], gate_s[r, HG_W + RET_W:],
            hm_hg_ref[...], hm_ret_ref[...], g_hg_ref[...], g_ret_ref[...],
            w_br_hg_ref[...], w_br_ret_ref[...], w_out_ref[...])

    @pl.when(t_idx == pl.num_programs(1) - 1)
    def _():
        for g in range(n_hg_groups):
            st_t = st_hg[g].T
            for h in range(HG_GROUP):
                blk = slice(HG_FDIM * h, HG_FDIM * (h + 1))
                st_hg_out_ref[HG_GROUP * g + h] = st_t[blk, blk]
        for g in range(n_ret_groups):
            for h in range(RET_GROUP):
                blk = slice(RET_DK * h, RET_DK * (h + 1))
                st_ret_out_ref[RET_GROUP * g + h] = st_ret[g, blk, blk]


def _const_spec(shape):
    zeros = (0,) * len(shape)
    return pl.BlockSpec(shape, lambda *_: zeros, pipeline_mode=pl.Buffered(1))


def _mix_prompt(x, cos_full, sin_signed, w_in, g_mix, lb, consts, hm_hg, hm_ret, g_hg, g_ret,
                w_br_hg, w_br_ret, w_out):
    b, l, d = x.shape
    nt = l // MIX_TILE
    tile = lambda w: pl.BlockSpec((None, MIX_TILE, w), lambda i, j: (i, j, 0))
    table = lambda w: pl.BlockSpec((MIX_TILE, w), lambda i, j: (j, 0))
    n_hg_groups = HG_HEADS // HG_GROUP
    n_ret_groups = RET_HEADS // RET_GROUP
    const_args = (w_in, g_mix, lb) + tuple(consts) + (hm_hg, hm_ret, g_hg, g_ret, w_br_hg, w_br_ret, w_out)
    state_spec = lambda h, dk, dv: pl.BlockSpec((None, None, h, dk, dv), lambda i, j: (0, i, 0, 0, 0))
    return pl.pallas_call(
        _mix_prompt_kernel,
        grid=(b, nt),
        in_specs=[tile(d), table(RET_W), table(RET_W)] + [_const_spec(a.shape) for a in const_args],
        out_specs=[tile(d), state_spec(HG_HEADS, HG_FDIM, HG_IDIM), state_spec(RET_HEADS, RET_DK, RET_DV)],
        out_shape=[jax.ShapeDtypeStruct((b, l, d), F32),
                   jax.ShapeDtypeStruct((1, b, HG_HEADS, HG_FDIM, HG_IDIM), F32),
                   jax.ShapeDtypeStruct((1, b, RET_HEADS, RET_DK, RET_DV), F32)],
        scratch_shapes=[
            pltpu.VMEM((MIX_TILE, d), BF16),
            pltpu.VMEM((MIX_TILE, HG_W), F32),
            pltpu.VMEM((MIX_TILE, HG_W), F32),
            pltpu.VMEM((MIX_TILE, HG_W), BF16),
            pltpu.VMEM((MIX_TILE, HG_W), BF16),
            pltpu.VMEM((MIX_TILE, HG_W), BF16),
            pltpu.VMEM((MIX_TILE, HG_W), F32),
            pltpu.VMEM((MIX_TILE, RET_W), F32),
            pltpu.VMEM((MIX_TILE, HG_W + RET_W + 2 * D_MODEL), F32),
            pltpu.VMEM((n_hg_groups, MXU_W, MXU_W), F32),
            pltpu.VMEM((n_ret_groups, MXU_W, MXU_W), F32),
        ],
        compiler_params=pltpu.CompilerParams(
            dimension_semantics=("arbitrary", "arbitrary"), vmem_limit_bytes=V7X_VMEM_LIMIT),
        name="mix_prompt",
    )(x, cos_full, sin_signed, *const_args)


def _ffn_prompt_kernel(h_ref, p_ref, g_ffn_ref, w_up_ref, conv_w_ref, conv_b_ref, w_down_ref,
                       g_ple_ref, w_ple_gate_ref, w_ple_ref, g_final_ref,
                       y_ref, conv_out_ref, a_ext, gate_s):
    t_idx = pl.program_id(1)
    tl = FFN_TILE

    @pl.when(t_idx == 0)
    def _():
        a_ext[0:CARRY_ROWS, :] = jnp.zeros((CARRY_ROWS, D_FF), F32)

    @pl.when(t_idx > 0)
    def _():
        a_ext[0:CARRY_ROWS, :] = a_ext[tl:tl + CARRY_ROWS, :]

    subs = [slice(i * FFN_SUB, (i + 1) * FFN_SUB) for i in range(tl // FFN_SUB)]
    for r in subs:
        xn = _rms(h_ref[r, :], g_ffn_ref[...]).astype(BF16)
        a_ext[CARRY_ROWS + r.start:CARRY_ROWS + r.stop, :] = _dot(xn, w_up_ref[:, 0:D_FF])
        gate_s[r, :] = _dot(xn, w_up_ref[:, D_FF:2 * D_FF])
    cw = conv_w_ref[...]
    for r in subs:
        ac = conv_b_ref[...]
        for j in range(CONV_W):
            off = CARRY_ROWS - (CONV_W - 1) + j
            ac = ac + a_ext[off + r.start:off + r.stop, :] * cw[j:j + 1, :]
        y_ref[r, :] = _ffn_tail(h_ref[r, :], ac, gate_s[r, :], p_ref[r, :], w_down_ref[...], g_ple_ref[...],
                                w_ple_gate_ref[...], w_ple_ref[...], g_final_ref[...])

    @pl.when(t_idx == pl.num_programs(1) - 1)
    def _():
        conv_out_ref[...] = a_ext[CARRY_ROWS + tl - (CONV_W - 1):CARRY_ROWS + tl, :]


def _ffn_prompt(h, p, g_ffn, w_up, conv_w, conv_b, w_down, g_ple, w_ple_gate, w_ple, g_final):
    b, l, d = h.shape
    nt = l // FFN_TILE
    tile = lambda w: pl.BlockSpec((None, FFN_TILE, w), lambda i, j: (i, j, 0))
    const_args = (g_ffn, w_up, conv_w, conv_b, w_down, g_ple, w_ple_gate, w_ple, g_final)
    return pl.pallas_call(
        _ffn_prompt_kernel,
        grid=(b, nt),
        in_specs=[tile(d), tile(PLE_DIM)] + [_const_spec(a.shape) for a in const_args],
        out_specs=[tile(d), pl.BlockSpec((None, CONV_W - 1, D_FF), lambda i, j: (i, 0, 0))],
        out_shape=[jax.ShapeDtypeStruct((b, l, d), F32),
                   jax.ShapeDtypeStruct((b, CONV_W - 1, D_FF), F32)],
        scratch_shapes=[pltpu.VMEM((FFN_TILE + CARRY_ROWS, D_FF), F32), pltpu.VMEM((FFN_TILE, D_FF), F32)],
        compiler_params=pltpu.CompilerParams(
            dimension_semantics=("arbitrary", "arbitrary"), vmem_limit_bytes=V7X_VMEM_LIMIT),
        name="ffn_prompt",
    )(h, p, *const_args)


def _proj_sample_kernel(x_ref, cos_ref, sin_ref, w_in_ref, g_mix_ref, lb_ref,
                        f_ref, kk_ref, q_ref, v_ref, kr_ref, qr_ref, vr_ref, gates_ref):
    xn = _rms(x_ref[...], g_mix_ref[...]).astype(BF16)
    u = _dot(xn, w_in_ref[:, C_HQ:C_HG])
    q, f, kk = _hgrn_inputs(u[:, 0:HG_W], u[:, HG_W:2 * HG_W], lb_ref[...])
    f_ref[...] = f.T
    kk_ref[...] = kk.T
    q_ref[...] = q.T
    v_ref[...] = u[:, 2 * HG_W:3 * HG_W].T
    u = _dot(xn, w_in_ref[:, C_RQ:C_RG])
    cos_full = cos_ref[...]
    sin_signed = sin_ref[...]
    qr_ref[...] = _rotary(u[:, 0:RET_W], cos_full, sin_signed)
    kr_ref[...] = _rotary(u[:, RET_W:2 * RET_W], cos_full, sin_signed) * (RET_DK ** -0.5)
    vr_ref[...] = u[:, 2 * RET_W:3 * RET_W]
    gates_ref[:, 0:HG_W] = _dot(xn, w_in_ref[:, C_HG:C_RQ])
    gates_ref[:, HG_W:HG_W + RET_W] = _dot(xn, w_in_ref[:, C_RG:C_GA])
    gates_ref[:, HG_W + RET_W:] = _dot(xn, w_in_ref[:, C_GA:C_END])


def _proj_sample(x, cos_full, sin_signed, w_in, g_mix, lb):
    n = x.shape[0]
    col = jax.ShapeDtypeStruct((HG_W, n), F32)
    row = jax.ShapeDtypeStruct((n, RET_W), F32)
    return pl.pallas_call(
        _proj_sample_kernel,
        out_shape=[col] * 4 + [row] * 3 + [jax.ShapeDtypeStruct((n, HG_W + RET_W + 2 * D_MODEL), F32)],
        compiler_params=pltpu.CompilerParams(vmem_limit_bytes=V7X_VMEM_LIMIT),
        name="proj_sample",
    )(x, cos_full, sin_signed, w_in, g_mix, lb)


def _state_tokens_minor_kernel(s_ref, a_t_ref, k_t_ref, q_t_ref, v_t_ref, s_out_ref, o_t_ref):
    dk, dv, _ = s_ref.shape
    row0 = pl.program_id(0) * dk
    v_blk = v_t_ref[...]

    def body(k, acc):
        s_new = a_t_ref[pl.ds(row0 + k, 1), :] * s_ref[k] + k_t_ref[pl.ds(row0 + k, 1), :] * v_blk
        s_out_ref[k] = s_new
        return acc + q_t_ref[pl.ds(row0 + k, 1), :] * s_new

    o_t_ref[...] = lax.fori_loop(0, dk, body, jnp.zeros(v_blk.shape, F32), unroll=SUBLANES)


def _state_update_tokens_minor(state_t, layer, a_t, k_t, q_t, v_t, *, name):
    _, heads, dk, dv, n = state_t.shape
    full = lambda a: pl.BlockSpec(a.shape, lambda h: (0, 0))
    per_head = pl.BlockSpec((dv, n), lambda h: (h, 0))
    slab = lambda lead: pl.BlockSpec((None, None, dk, dv, n), lambda h: (lead, h, 0, 0, 0))
    return pl.pallas_call(
        _state_tokens_minor_kernel,
        grid=(heads,),
        in_specs=[slab(layer), full(a_t), full(k_t), full(q_t), per_head],
        out_specs=[slab(0), per_head],
        out_shape=[jax.ShapeDtypeStruct((1,) + state_t.shape[1:], F32), jax.ShapeDtypeStruct(v_t.shape, F32)],
        compiler_params=pltpu.CompilerParams(
            dimension_semantics=("arbitrary",), vmem_limit_bytes=V7X_VMEM_LIMIT),
        name=name,
    )(state_t, a_t, k_t, q_t, v_t)


def _state_kernel(s_ref, a_ref, k_ref, q_ref, v_ref, s_out_ref, o_ref, *, dk, dv, kb):
    j = pl.program_id(0)
    n = s_ref.shape[0]

    @pl.when(j == 0)
    def _():
        o_ref[...] = jnp.zeros_like(o_ref)

    group0 = j * (kb // SUBLANES)
    head_row = pl.multiple_of((j * kb // dk) * dv, dv)
    v_blk = v_ref[pl.ds(head_row, dv), :]
    acc = jnp.zeros(v_blk.shape, F32)
    for g in range(kb // SUBLANES):
        rows = slice(SUBLANES * g, SUBLANES * (g + 1))
        s_t = s_ref[:, rows, :].reshape(n * SUBLANES, dv).T
        s_new = a_ref[pl.ds(group0 + g, 1), :] * s_t + k_ref[pl.ds(group0 + g, 1), :] * v_blk
        s_out_ref[:, rows, :] = s_new.T.reshape(n, SUBLANES, dv)
        acc = acc + q_ref[pl.ds(group0 + g, 1), :] * s_new
    o_ref[pl.ds(head_row, dv), :] += acc


def _state_update(state, layer, a_il, k_il, q_il, v_rep, *, kb, name):
    _, n, heads, dk, dv = state.shape
    per_head = dk // kb
    full = lambda a: pl.BlockSpec(a.shape, lambda j: (0, 0))
    slab = lambda lead: pl.BlockSpec((None, n, None, kb, dv), lambda j: (lead, 0, j // per_head, j % per_head, 0))
    return pl.pallas_call(
        functools.partial(_state_kernel, dk=dk, dv=dv, kb=kb),
        grid=(heads * per_head,),
        in_specs=[slab(layer), full(a_il), full(k_il), full(q_il), full(v_rep)],
        out_specs=[slab(0), full(v_rep)],
        out_shape=[jax.ShapeDtypeStruct((1,) + state.shape[1:], F32), jax.ShapeDtypeStruct(v_rep.shape, F32)],
        compiler_params=pltpu.CompilerParams(
            dimension_semantics=("arbitrary",), vmem_limit_bytes=V7X_VMEM_LIMIT),
        name=name,
    )(state, a_il, k_il, q_il, v_rep)


def _lane_group_sum(o_ref, sel):
    o = o_ref[...]
    hi = o.astype(BF16)
    lo = (o - hi.astype(F32)).astype(BF16)
    return _dot_nt(sel, hi) + _dot_nt(sel, lo)


def _tail_sample_kernel(x_ref, o_hg_t_ref, o_ret_ref, sel_ref, gates_ref, conv0_ref, p_ref,
                        hm_hg_ref, hm_ret_ref, g_hg_ref, g_ret_ref, w_br_hg_ref, w_br_ret_ref, w_out_ref,
                        g_ffn_ref, w_up_ref, conv_w_ref, conv_b_ref, w_down_ref,
                        g_ple_ref, w_ple_gate_ref, w_ple_ref, g_final_ref,
                        y_ref, conv_out_ref):
    h = _merge_branches(
        x_ref[...], o_hg_t_ref[...].T, _lane_group_sum(o_ret_ref, sel_ref[...]),
        gates_ref[:, 0:HG_W], gates_ref[:, HG_W:HG_W + RET_W], gates_ref[:, HG_W + RET_W:],
        hm_hg_ref[...], hm_ret_ref[...], g_hg_ref[...], g_ret_ref[...],
        w_br_hg_ref[...], w_br_ret_ref[...], w_out_ref[...])
    xn = _rms(h, g_ffn_ref[...]).astype(BF16)
    a = _dot(xn, w_up_ref[:, 0:D_FF])
    gate_half = _dot(xn, w_up_ref[:, D_FF:2 * D_FF])
    cw = conv_w_ref[...]
    prev = conv0_ref[:, 1, :]
    ac = conv_b_ref[...] + conv0_ref[:, 0, :] * cw[0:1, :] + prev * cw[1:2, :] + a * cw[2:3, :]
    conv_out_ref[0, :, 0, :] = prev
    conv_out_ref[0, :, 1, :] = a
    y_ref[...] = _ffn_tail(h, ac, gate_half, p_ref[...], w_down_ref[...], g_ple_ref[...],
                           w_ple_gate_ref[...], w_ple_ref[...], g_final_ref[...])


def _tail_sample(x, o_hg_t, o_ret, sel, gates, conv0, p, *weights):
    n = x.shape[0]
    return pl.pallas_call(
        _tail_sample_kernel,
        out_shape=[jax.ShapeDtypeStruct((n, D_MODEL), F32),
                   jax.ShapeDtypeStruct((1, n, CONV_W - 1, D_FF), F32)],
        compiler_params=pltpu.CompilerParams(vmem_limit_bytes=V7X_VMEM_LIMIT),
        name="tail_sample",
    )(x, o_hg_t, o_ret, sel, gates, conv0, p, *weights)


def kernel(x_prompt, x_sample, state_hgrn, state_ret, state_conv, p_prompt, p_sample, lb_logits, w_in, g_mix, g_hg_out, g_ret_out, w_br_hg, w_br_ret, w_out, g_ffn, w_up, conv_w, conv_b, w_down, g_ple, w_ple, w_ple_gate, g_final):
    assert DEPTH == 1 and x_prompt.shape == (BATCH, SEQ, D_MODEL) and x_sample.shape == (DEC_BATCH, DEC_SEQ, D_MODEL)
    i = 0
    row = lambda a: a.reshape(1, -1).astype(F32)
    log_gamma = jnp.log1p(-jnp.exp2(-5.0 - jnp.arange(RET_HEADS, dtype=F32)))
    lb = jnp.cumsum(jax.nn.softmax(lb_logits.astype(F32), axis=0), axis=0)[i].reshape(1, HG_W)

    sel2, level = _level_tables()
    consts = (
        jnp.asarray(sel2, BF16),
        jnp.asarray(np.tile(level, (1, HG_GROUP))),
        jnp.asarray(_block_mask(MXU_W, MXU_W, CHUNK, HG_FDIM), BF16),
        jnp.asarray(_block_mask(MXU_W, MXU_W, HG_IDIM, HG_FDIM), F32),
        jnp.asarray(_block_mask(MXU_W, MXU_W, RET_DK, RET_DV), F32),
    ) + _retention_tables(log_gamma)
    hm_hg = jnp.asarray(_block_mask(HG_W, HG_W, HG_IDIM, HG_IDIM) / HG_IDIM, BF16)
    hm_ret = jnp.asarray(_block_mask(RET_W, RET_W, RET_DV, RET_DV) / RET_DV, BF16)

    w_in_b = w_in[i].astype(BF16)
    mix_w = (hm_hg, hm_ret, row(g_hg_out[i]), row(g_ret_out[i]),
             w_br_hg[i].astype(BF16), w_br_ret[i].astype(BF16), w_out[i].astype(BF16))
    ffn_w = (row(g_ffn[i]), w_up[i].astype(BF16), conv_w[i].astype(F32), row(conv_b[i]), w_down[i].astype(BF16),
             row(g_ple[i]), w_ple_gate[i].astype(BF16), w_ple[i].astype(BF16), row(g_final))

    cos_p, sin_p = _rope_tables(0.0, SEQ)
    h_p, hg_state_p, ret_state_p = _mix_prompt(x_prompt, cos_p, sin_p, w_in_b, row(g_mix[i]), lb, consts, *mix_w)
    y_prompt, conv_p = _ffn_prompt(h_p, p_prompt[i], *ffn_w)

    n = DEC_BATCH
    xs = x_sample.reshape(n, D_MODEL)
    cos_s, sin_s = _rope_tables(float(PAST_LEN), DEC_SEQ)
    f_t, kk_t, q_t, v_t, kr, qr, vr, gates = _proj_sample(xs, cos_s, sin_s, w_in_b, row(g_mix[i]), lb)
    hg_new_t, o_hg_t = _state_update_tokens_minor(
        jnp.transpose(state_hgrn, (0, 2, 3, 4, 1)), i, f_t, kk_t, q_t, v_t, name="state_hgrn")
    hg_new = jnp.transpose(hg_new_t, (0, 4, 1, 2, 3))
    interleave = lambda a: a.reshape(n, -1, SUBLANES).transpose(1, 0, 2).reshape(-1, n * SUBLANES)
    repeat = lambda a: jnp.repeat(a.T, SUBLANES, axis=1)
    gamma = jnp.broadcast_to(jnp.repeat(jnp.exp(log_gamma), RET_DK // SUBLANES)[:, None],
                             (RET_W // SUBLANES, n * SUBLANES))
    ret_new, o_ret = _state_update(
        state_ret, i, gamma, interleave(kr), interleave(qr), repeat(vr), kb=RET_DK // 2, name="state_ret")
    sel = jnp.asarray(_block_mask(n, n * SUBLANES, 1, SUBLANES), BF16)
    y_sample, conv_s = _tail_sample(
        xs, o_hg_t, o_ret, sel, gates, state_conv[i], p_sample[i].reshape(n, PLE_DIM),
        *mix_w, *ffn_w)

    return (y_prompt, y_sample.reshape(n, DEC_SEQ, D_MODEL),
            hg_state_p, ret_state_p, conv_p[None],
            hg_new, ret_new,
            conv_s)
```

```python
import functools

import numpy as np
import jax
import jax.numpy as jnp
from jax import lax
from jax.experimental import pallas as pl
from jax.experimental.pallas import tpu as pltpu

D_MODEL = 1024
BATCH = 8
SEQ = 2048
DEPTH = 1
DEC_BATCH = 128
DEC_SEQ = 1
PAST_LEN = 16384
HG_HEADS = 8
HG_FDIM = 64
HG_IDIM = 64
HG_W = HG_HEADS * HG_FDIM
RET_HEADS = 4
RET_DK = 128
RET_DV = 128
RET_W = RET_HEADS * RET_DK
D_FF = 2816
CONV_W = 3
PLE_DIM = 256
CHUNK = 64
ROPE_BASE = 10000.0
EPS = 1e-6

C_HQ, C_HF, C_HI, C_HG = 0, 512, 1024, 1536
C_RQ, C_RK, C_RV, C_RG = 2048, 2560, 3072, 3584
C_GA, C_GB, C_END = 4096, 5120, 6144

N_LEVELS = 6
FAST_CHUNK = 128
FAST_DECAY_LIMIT = 80.0
HG_GROUP = 4
RET_GROUP = 2
MXU_W = 256
SUBLANES = 8

MIX_TILE = 256
MIX_SUB = 256
GATE_PIECES = (
    ((C_HG, HG_W, 0), (C_RG, RET_W, HG_W)),
    ((C_GA, 768, HG_W + RET_W),),
    ((C_GA + 768, 768, HG_W + RET_W + 768),),
    ((C_GA + 1536, 512, HG_W + RET_W + 1536),),
)
FFN_TILE = 512
FFN_SUB = 256
CARRY_ROWS = 8
V7X_VMEM_LIMIT = 56 * 1024 * 1024

F32 = jnp.float32
BF16 = jnp.bfloat16


def _level_tables():
    c = CHUNK
    t = np.arange(c)[:, None]
    r = np.arange(c)[None, :]
    blocks = [r <= t, r > t]
    for l in range(N_LEVELS):
        same = (t >> l) == (r >> l)
        upper = ((t >> l) & 1) == 1
        blocks.append(np.where(upper, same & (r <= t), same & (r > t)))
    sel = np.concatenate(blocks, axis=0).astype(np.float32)
    sel2 = np.concatenate([sel, sel], axis=1)
    s = r
    level = np.full((c, c), N_LEVELS + 1, np.int32)
    level[np.broadcast_to(s == t, (c, c))] = N_LEVELS
    x = t ^ s
    for l in range(N_LEVELS):
        hit = (s < t) & ((x >> l) == 1)
        level[hit] = l
    return sel2, level


def _block_mask(rows, cols, rblk, cblk):
    r = np.arange(rows)[:, None] // rblk
    c = np.arange(cols)[None, :] // cblk
    return (r == c).astype(np.float32)


def _rope_tables(pos0, length):
    d = RET_DK
    inv_freq = ROPE_BASE ** (-jnp.arange(0, d, 2, dtype=F32) / d)
    pos = pos0 + jnp.arange(length, dtype=F32)
    ang = pos[:, None] * inv_freq[None, :]
    cos, sin = jnp.cos(ang), jnp.sin(ang)
    cos_full = jnp.tile(jnp.concatenate([cos, cos], axis=-1), (1, RET_HEADS))
    sin_signed = jnp.tile(jnp.concatenate([-sin, sin], axis=-1), (1, RET_HEADS))
    return cos_full, sin_signed


def _log_gamma():
    return np.log1p(-np.exp2(-5.0 - np.arange(RET_HEADS, dtype=np.float32))).astype(np.float32)


def _retention_tables():
    c = MIX_SUB
    log_gamma = _log_gamma()
    t = np.arange(c, dtype=np.float32)
    delta = t[:, None] - t[None, :]
    dmask = np.where(delta >= 0, np.exp(np.maximum(delta, 0.0) * log_gamma[:, None, None]), 0.0)
    q_decay = np.exp((t + 1.0)[None, :] * log_gamma[:, None])
    k_decay = np.exp((c - 1.0 - t)[None, :] * log_gamma[:, None])
    qdec = np.repeat(q_decay.T, RET_DK, axis=1)
    kdec = np.repeat(k_decay.T, RET_DK, axis=1)
    chunk_decay = np.exp(np.float32(c) * log_gamma)
    cd = np.repeat(chunk_decay, RET_DK).reshape(RET_HEADS // RET_GROUP, MXU_W, 1)
    cd = np.broadcast_to(cd, (RET_HEADS // RET_GROUP, MXU_W, MXU_W))
    return tuple(jnp.asarray(a, F32) for a in (dmask, qdec, kdec, cd))


def _rms(x, g):
    ms = jnp.mean(x * x, axis=-1, keepdims=True)
    return x * lax.rsqrt(ms + EPS) * g


def _silu(x):
    return x * jax.nn.sigmoid(x)


def _dot(a, b):
    return jnp.dot(a, b, preferred_element_type=F32)


def _dot_nt(a, b):
    return lax.dot_general(a, b, (((1,), (1,)), ((), ())), preferred_element_type=F32)


def _dot_tn(a, b):
    return lax.dot_general(a, b, (((0,), (0,)), ((), ())), preferred_element_type=F32)


def _rotary(x, cos_full, sin_signed):
    swapped = jnp.concatenate(
        [pltpu.roll(x[:, RET_DK * h:RET_DK * (h + 1)], RET_DK // 2, 1) for h in range(RET_HEADS)], axis=1)
    return x * cos_full + swapped * sin_signed


def _hgrn_inputs(hq, hf, lb):
    q = _silu(hq)
    f = lb + (1.0 - lb) * jax.nn.sigmoid(hf)
    return q, f, 1.0 - f


def _merge_branches(x, o_hg, o_ret, u_hg, u_rg, u_gab, hm_hg, hm_ret, g_hg, g_ret, w_br_hg, w_br_ret, w_out):
    ms = _dot((o_hg * o_hg).astype(BF16), hm_hg)
    a_hg = o_hg * lax.rsqrt(ms + EPS) * g_hg * _silu(u_hg)
    y_hg = _dot(a_hg.astype(BF16), w_br_hg)
    ms = _dot((o_ret * o_ret).astype(BF16), hm_ret)
    a_ret = o_ret * lax.rsqrt(ms + EPS) * g_ret * _silu(u_rg)
    y_ret = _dot(a_ret.astype(BF16), w_br_ret)
    mixed = jax.nn.sigmoid(u_gab[:, :D_MODEL]) * y_hg + jax.nn.sigmoid(u_gab[:, D_MODEL:]) * y_ret
    return x + _dot(mixed.astype(BF16), w_out)


def _ffn_tail(h, ac, gate_half, p, w_down, g_ple, w_ple_gate, w_ple, g_final):
    h = h + _dot((jax.nn.gelu(ac) * gate_half).astype(BF16), w_down)
    gate = jax.nn.sigmoid(_dot(_rms(h, g_ple).astype(BF16), w_ple_gate))
    h = h + gate * _dot(p.astype(BF16), w_ple)
    return _rms(h, g_final)


def _mix_prompt_kernel(x_ref, cos_ref, sin_ref, w_in_ref, g_mix_ref, lb_ref, sel_ref, tri_ref, hmask_ref, causal_ref, lvl_ref,
                       bd64_ref, bd64f_ref, bd128f_ref, dmask_ref, qdec_ref, kdec_ref, cd_ref,
                       hm_hg_ref, hm_ret_ref, g_hg_ref, g_ret_ref, w_br_hg_ref, w_br_ret_ref, w_out_ref,
                       h_ref, st_hg_out_ref, st_ret_out_ref,
                       xn_s, q_s, kk_s, lfh_s, lfl_s, v_s, vbd_s, qh_s, qi_s, kbd_s, ks_s, alast_s, o_hg_s, o_ret_s, gate_s, st_hg, st_ret):
    t_idx = pl.program_id(1)
    n_hg_groups = HG_HEADS // HG_GROUP
    n_ret_groups = RET_HEADS // RET_GROUP

    @pl.when(t_idx == 0)
    def _():
        st_hg[...] = jnp.zeros_like(st_hg)
        st_ret[...] = jnp.zeros_like(st_ret)

    subs = [slice(i * MIX_SUB, (i + 1) * MIX_SUB) for i in range(MIX_TILE // MIX_SUB)]
    chunk_sums = []
    for r in subs:
        xn = _rms(x_ref[r, :], g_mix_ref[...]).astype(BF16)
        xn_s[r, :] = xn

        u = _dot(xn, w_in_ref[:, C_HQ:C_HG])
        q, f, kk = _hgrn_inputs(u[:, 0:HG_W], u[:, HG_W:2 * HG_W], lb_ref[...])
        q_s[r, :] = q
        kk_s[r, :] = kk
        lf = jnp.log(f)
        lf_hi = lf.astype(BF16)
        lfh_s[r, :] = lf_hi
        lfl_s[r, :] = (lf - lf_hi.astype(F32)).astype(BF16)
        v = u[:, 2 * HG_W:3 * HG_W]
        v_s[r, :] = v.astype(BF16)
        for h in range(HG_GROUP):
            vbd_s[h, r, :] = (v * hmask_ref[h:h + 1, :]).astype(BF16)
        chunk_sums += [jnp.sum(lf[c * CHUNK:(c + 1) * CHUNK, :], axis=0, keepdims=True)
                       for c in range(MIX_SUB // CHUNK)]

        u = _dot(xn, w_in_ref[:, C_RQ:C_RG])
        cos_full = cos_ref[r, :]
        sin_signed = sin_ref[r, :]
        qr = _rotary(u[:, 0:RET_W], cos_full, sin_signed)
        kr = _rotary(u[:, RET_W:2 * RET_W], cos_full, sin_signed) * (RET_DK ** -0.5)
        vr = u[:, 2 * RET_W:3 * RET_W].astype(BF16)
        qrb = qr.astype(BF16)
        krb = kr.astype(BF16)
        o_heads = []
        for h in range(RET_HEADS):
            hs = slice(RET_DK * h, RET_DK * (h + 1))
            scores = _dot_nt(qrb[:, hs], krb[:, hs]) * dmask_ref[h]
            o_heads.append(_dot(scores.astype(BF16), vr[:, hs]))
        qd = (qr * qdec_ref[...]).astype(BF16)
        kd = (kr * kdec_ref[...]).astype(BF16)
        bd128f = bd128f_ref[...]
        o_inter = []
        for g in range(n_ret_groups):
            cs = slice(MXU_W * g, MXU_W * (g + 1))
            st = st_ret[g]
            o_inter.append(_dot(qd[:, cs], st.astype(BF16)))
            st_ret[g] = cd_ref[g] * st + _dot_tn(kd[:, cs], vr[:, cs]) * bd128f
        o_ret_s[r, :] = jnp.concatenate(o_heads, axis=1) + jnp.concatenate(o_inter, axis=1)

    mild = jnp.min(functools.reduce(jnp.minimum, chunk_sums)) >= -FAST_DECAY_LIMIT

    groups = [slice(MXU_W * g, MXU_W * (g + 1)) for g in range(n_hg_groups)]
    bd_tile = lambda ref, rows, cs: jnp.concatenate([ref[h, rows, cs] for h in range(HG_GROUP)], axis=0)

    def prepare_fast():
        per_sub = MIX_SUB // FAST_CHUNK
        rep = lambda rows: jnp.concatenate([jnp.broadcast_to(e, (FAST_CHUNK, HG_W)) for e in rows], axis=0)
        for i, r in enumerate(subs):
            lf2 = jnp.concatenate([lfh_s[r, :], lfl_s[r, :]], axis=0)
            cum = _dot(tri_ref[...], lf2)
            mids = [cum[c * FAST_CHUNK + FAST_CHUNK // 2 - 1:c * FAST_CHUNK + FAST_CHUNK // 2, :] for c in range(per_sub)]
            ends = [cum[(c + 1) * FAST_CHUNK - 1:(c + 1) * FAST_CHUNK, :] for c in range(per_sub)]
            mid = rep(mids)
            kk = kk_s[r, :]
            q = q_s[r, :]
            qh_s[r, :] = (q * jnp.exp(cum - mid)).astype(BF16)
            qi_s[r, :] = (q * jnp.exp(cum)).astype(BF16)
            k_mid = kk * jnp.exp(mid - cum)
            for h in range(HG_GROUP):
                kbd_s[h, r, :] = (k_mid * hmask_ref[h:h + 1, :]).astype(BF16)
            ks_s[r, :] = (kk * jnp.exp(rep(ends) - cum)).astype(BF16)
            for c, e in enumerate(ends):
                ci = i * per_sub + c
                alast_s[ci:ci + 1, :] = jnp.exp(e)

    def chunk_fast(c):
        rows = slice(c * FAST_CHUNK, (c + 1) * FAST_CHUNK)
        causal = causal_ref[...] != 0
        bd64f = bd64f_ref[...]
        o_parts = []
        for g, cs in enumerate(groups):
            scores = jnp.where(causal, _dot_nt(qh_s[rows, cs], bd_tile(kbd_s, rows, cs)), 0.0)
            st = st_hg[g]
            o_parts.append(_dot(scores.astype(BF16), bd_tile(vbd_s, rows, cs)) + _dot_nt(qi_s[rows, cs], st.astype(BF16)))
            st_hg[g] = st * alast_s[c:c + 1, cs] + _dot_tn(v_s[rows, cs], ks_s[rows, cs]) * bd64f
        o_hg_s[rows, :] = jnp.concatenate(o_parts, axis=1)

    def chunk_safe(c):
        rows = slice(c * CHUNK, (c + 1) * CHUNK)
        q = q_s[rows, :]
        kk = kk_s[rows, :]
        lf2 = jnp.concatenate([lfh_s[rows, :], lfl_s[rows, :]], axis=0)
        ex = jnp.exp(_dot(sel_ref[...], lf2))
        lvl = lvl_ref[...]
        bd64 = bd64_ref[...]
        bd64f = bd64f_ref[...]
        o_parts = []
        for g, cs in enumerate(groups):
            scores = jnp.zeros((CHUNK, MXU_W), F32)
            for l in range(N_LEVELS + 1):
                if l < N_LEVELS:
                    e = ex[CHUNK * (l + 2):CHUNK * (l + 3), cs]
                    ql = (q[:, cs] * e).astype(BF16)
                    kl = (kk[:, cs] * e).astype(BF16)
                else:
                    ql = q[:, cs].astype(BF16)
                    kl = kk[:, cs].astype(BF16)
                rhs = jnp.concatenate([kl] * HG_GROUP, axis=0) * bd64
                scores = jnp.where(lvl == l, _dot_nt(ql, rhs), scores)
            e_cum = ex[0:CHUNK, cs]
            e_suf = ex[CHUNK:2 * CHUNK, cs]
            st = st_hg[g]
            o_inter = _dot_nt((q[:, cs] * e_cum).astype(BF16), st.astype(BF16))
            o_parts.append(_dot(scores.astype(BF16), bd_tile(vbd_s, rows, cs)) + o_inter)
            upd = _dot_tn(v_s[rows, cs], (kk[:, cs] * e_suf).astype(BF16))
            st_hg[g] = st * e_cum[CHUNK - 1:CHUNK, :] + upd * bd64f
        o_hg_s[rows, :] = jnp.concatenate(o_parts, axis=1)

    def recurrences(fast):
        if fast:
            prepare_fast()
        for c in range(MIX_TILE // CHUNK):
            if not fast:
                chunk_safe(c)
            elif c % (FAST_CHUNK // CHUNK) == 0:
                chunk_fast(c * CHUNK // FAST_CHUNK)
            r = subs[c * CHUNK // MIX_SUB]
            for src, width, dst in GATE_PIECES[c % len(GATE_PIECES)]:
                gate_s[r, dst:dst + width] = _dot(xn_s[r, :], w_in_ref[:, src:src + width])

    pl.when(mild)(functools.partial(recurrences, True))
    pl.when(jnp.logical_not(mild))(functools.partial(recurrences, False))

    for r in subs:
        h_ref[r, :] = _merge_branches(
            x_ref[r, :], o_hg_s[r, :], o_ret_s[r, :],
            gate_s[r, 0:HG_W], gate_s[r, HG_W:HG_W + RET_W], gate_s[r, HG_W + RET_W:],
            hm_hg_ref[...], hm_ret_ref[...], g_hg_ref[...], g_ret_ref[...],
            w_br_hg_ref[...], w_br_ret_ref[...], w_out_ref[...])

    @pl.when(t_idx == pl.num_programs(1) - 1)
    def _():
        for g in range(n_hg_groups):
            st_t = st_hg[g].T
            for h in range(HG_GROUP):
                blk = slice(HG_FDIM * h, HG_FDIM * (h + 1))
                st_hg_out_ref[HG_GROUP * g + h] = st_t[blk, blk]
        for g in range(n_ret_groups):
            for h in range(RET_GROUP):
                blk = slice(RET_DK * h, RET_DK * (h + 1))
                st_ret_out_ref[RET_GROUP * g + h] = st_ret[g, blk, blk]


def _const_spec(shape):
    zeros = (0,) * len(shape)
    return pl.BlockSpec(shape, lambda *_: zeros, pipeline_mode=pl.Buffered(1))


def _mix_prompt(x, cos_full, sin_signed, w_in, g_mix, lb, consts, hm_hg, hm_ret, g_hg, g_ret,
                w_br_hg, w_br_ret, w_out):
    b, l, d = x.shape
    nt = l // MIX_TILE
    tile = lambda w: pl.BlockSpec((None, MIX_TILE, w), lambda i, j: (i, j, 0))
    table = lambda w: pl.BlockSpec((MIX_TILE, w), lambda i, j: (j, 0))
    n_hg_groups = HG_HEADS // HG_GROUP
    n_ret_groups = RET_HEADS // RET_GROUP
    const_args = (w_in, g_mix, lb) + tuple(consts) + (hm_hg, hm_ret, g_hg, g_ret, w_br_hg, w_br_ret, w_out)
    state_spec = lambda h, dk, dv: pl.BlockSpec((None, None, h, dk, dv), lambda i, j: (0, i, 0, 0, 0))
    return pl.pallas_call(
        _mix_prompt_kernel,
        grid=(b, nt),
        in_specs=[tile(d), table(RET_W), table(RET_W)] + [_const_spec(a.shape) for a in const_args],
        out_specs=[tile(d), state_spec(HG_HEADS, HG_FDIM, HG_IDIM), state_spec(RET_HEADS, RET_DK, RET_DV)],
        out_shape=[jax.ShapeDtypeStruct((b, l, d), F32),
                   jax.ShapeDtypeStruct((1, b, HG_HEADS, HG_FDIM, HG_IDIM), F32),
                   jax.ShapeDtypeStruct((1, b, RET_HEADS, RET_DK, RET_DV), F32)],
        scratch_shapes=[
            pltpu.VMEM((MIX_TILE, d), BF16),
            pltpu.VMEM((MIX_TILE, HG_W), F32),
            pltpu.VMEM((MIX_TILE, HG_W), F32),
            pltpu.VMEM((MIX_TILE, HG_W), BF16),
            pltpu.VMEM((MIX_TILE, HG_W), BF16),
            pltpu.VMEM((MIX_TILE, HG_W), BF16),
            pltpu.VMEM((HG_GROUP, MIX_TILE, HG_W), BF16),
            pltpu.VMEM((MIX_TILE, HG_W), BF16),
            pltpu.VMEM((MIX_TILE, HG_W), BF16),
            pltpu.VMEM((HG_GROUP, MIX_TILE, HG_W), BF16),
            pltpu.VMEM((MIX_TILE, HG_W), BF16),
            pltpu.VMEM((max(MIX_TILE // CHUNK, SUBLANES), HG_W), F32),
            pltpu.VMEM((MIX_TILE, HG_W), F32),
            pltpu.VMEM((MIX_TILE, RET_W), F32),
            pltpu.VMEM((MIX_TILE, HG_W + RET_W + 2 * D_MODEL), F32),
            pltpu.VMEM((n_hg_groups, MXU_W, MXU_W), F32),
            pltpu.VMEM((n_ret_groups, MXU_W, MXU_W), F32),
        ],
        compiler_params=pltpu.CompilerParams(
            dimension_semantics=("arbitrary", "arbitrary"), vmem_limit_bytes=V7X_VMEM_LIMIT),
        name="mix_prompt",
    )(x, cos_full, sin_signed, *const_args)


def _ffn_prompt_kernel(h_ref, p_ref, g_ffn_ref, w_up_ref, conv_w_ref, conv_b_ref, w_down_ref,
                       g_ple_ref, w_ple_gate_ref, w_ple_ref, g_final_ref,
                       y_ref, conv_out_ref, a_ext, gate_s):
    t_idx = pl.program_id(1)
    tl = FFN_TILE

    @pl.when(t_idx == 0)
    def _():
        a_ext[0:CARRY_ROWS, :] = jnp.zeros((CARRY_ROWS, D_FF), F32)

    @pl.when(t_idx > 0)
    def _():
        a_ext[0:CARRY_ROWS, :] = a_ext[tl:tl + CARRY_ROWS, :]

    subs = [slice(i * FFN_SUB, (i + 1) * FFN_SUB) for i in range(tl // FFN_SUB)]
    for r in subs:
        xn = _rms(h_ref[r, :], g_ffn_ref[...]).astype(BF16)
        a_ext[CARRY_ROWS + r.start:CARRY_ROWS + r.stop, :] = _dot(xn, w_up_ref[:, 0:D_FF])
        gate_s[r, :] = _dot(xn, w_up_ref[:, D_FF:2 * D_FF])
    cw = conv_w_ref[...]
    for r in subs:
        ac = conv_b_ref[...]
        for j in range(CONV_W):
            off = CARRY_ROWS - (CONV_W - 1) + j
            ac = ac + a_ext[off + r.start:off + r.stop, :] * cw[j:j + 1, :]
        y_ref[r, :] = _ffn_tail(h_ref[r, :], ac, gate_s[r, :], p_ref[r, :], w_down_ref[...], g_ple_ref[...],
                                w_ple_gate_ref[...], w_ple_ref[...], g_final_ref[...])

    @pl.when(t_idx == pl.num_programs(1) - 1)
    def _():
        conv_out_ref[...] = a_ext[CARRY_ROWS + tl - (CONV_W - 1):CARRY_ROWS + tl, :]


def _ffn_prompt(h, p, g_ffn, w_up, conv_w, conv_b, w_down, g_ple, w_ple_gate, w_ple, g_final):
    b, l, d = h.shape
    nt = l // FFN_TILE
    tile = lambda w: pl.BlockSpec((None, FFN_TILE, w), lambda i, j: (i, j, 0))
    const_args = (g_ffn, w_up, conv_w, conv_b, w_down, g_ple, w_ple_gate, w_ple, g_final)
    return pl.pallas_call(
        _ffn_prompt_kernel,
        grid=(b, nt),
        in_specs=[tile(d), tile(PLE_DIM)] + [_const_spec(a.shape) for a in const_args],
        out_specs=[tile(d), pl.BlockSpec((None, CONV_W - 1, D_FF), lambda i, j: (i, 0, 0))],
        out_shape=[jax.ShapeDtypeStruct((b, l, d), F32),
                   jax.ShapeDtypeStruct((b, CONV_W - 1, D_FF), F32)],
        scratch_shapes=[pltpu.VMEM((FFN_TILE + CARRY_ROWS, D_FF), F32), pltpu.VMEM((FFN_TILE, D_FF), F32)],
        compiler_params=pltpu.CompilerParams(
            dimension_semantics=("arbitrary", "arbitrary"), vmem_limit_bytes=V7X_VMEM_LIMIT),
        name="ffn_prompt",
    )(h, p, *const_args)


def _proj_sample_kernel(x_ref, cos_ref, sin_ref, w_in_ref, g_mix_ref, lb_ref,
                        f_ref, kk_ref, q_ref, v_ref, kr_ref, qr_ref, vr_ref, gates_ref):
    xn = _rms(x_ref[...], g_mix_ref[...]).astype(BF16)
    u = _dot(xn, w_in_ref[:, C_HQ:C_HG])
    q, f, kk = _hgrn_inputs(u[:, 0:HG_W], u[:, HG_W:2 * HG_W], lb_ref[...])
    f_ref[...] = f.T
    kk_ref[...] = kk.T
    q_ref[...] = q.T
    v_ref[...] = u[:, 2 * HG_W:3 * HG_W].T
    u = _dot(xn, w_in_ref[:, C_RQ:C_RG])
    cos_full = cos_ref[...]
    sin_signed = sin_ref[...]
    qr_ref[...] = _rotary(u[:, 0:RET_W], cos_full, sin_signed)
    kr_ref[...] = _rotary(u[:, RET_W:2 * RET_W], cos_full, sin_signed) * (RET_DK ** -0.5)
    vr_ref[...] = u[:, 2 * RET_W:3 * RET_W]
    gates_ref[:, 0:HG_W] = _dot(xn, w_in_ref[:, C_HG:C_RQ])
    gates_ref[:, HG_W:HG_W + RET_W] = _dot(xn, w_in_ref[:, C_RG:C_GA])
    gates_ref[:, HG_W + RET_W:] = _dot(xn, w_in_ref[:, C_GA:C_END])


def _proj_sample(x, cos_full, sin_signed, w_in, g_mix, lb):
    n = x.shape[0]
    col = jax.ShapeDtypeStruct((HG_W, n), F32)
    row = jax.ShapeDtypeStruct((n, RET_W), F32)
    return pl.pallas_call(
        _proj_sample_kernel,
        out_shape=[col] * 4 + [row] * 3 + [jax.ShapeDtypeStruct((n, HG_W + RET_W + 2 * D_MODEL), F32)],
        compiler_params=pltpu.CompilerParams(vmem_limit_bytes=V7X_VMEM_LIMIT),
        name="proj_sample",
    )(x, cos_full, sin_signed, w_in, g_mix, lb)


def _state_tokens_minor_kernel(s_ref, a_t_ref, k_t_ref, q_t_ref, v_t_ref, s_out_ref, o_t_ref):
    dk, dv, _ = s_ref.shape
    row0 = pl.program_id(0) * dk
    v_blk = v_t_ref[...]

    def body(k, acc):
        s_new = a_t_ref[pl.ds(row0 + k, 1), :] * s_ref[k] + k_t_ref[pl.ds(row0 + k, 1), :] * v_blk
        s_out_ref[k] = s_new
        return acc + q_t_ref[pl.ds(row0 + k, 1), :] * s_new

    o_t_ref[...] = lax.fori_loop(0, dk, body, jnp.zeros(v_blk.shape, F32), unroll=SUBLANES)


def _state_update_tokens_minor(state_t, layer, a_t, k_t, q_t, v_t, *, name):
    _, heads, dk, dv, n = state_t.shape
    full = lambda a: pl.BlockSpec(a.shape, lambda h: (0, 0))
    per_head = pl.BlockSpec((dv, n), lambda h: (h, 0))
    slab = lambda lead: pl.BlockSpec((None, None, dk, dv, n), lambda h: (lead, h, 0, 0, 0))
    return pl.pallas_call(
        _state_tokens_minor_kernel,
        grid=(heads,),
        in_specs=[slab(layer), full(a_t), full(k_t), full(q_t), per_head],
        out_specs=[slab(0), per_head],
        out_shape=[jax.ShapeDtypeStruct((1,) + state_t.shape[1:], F32), jax.ShapeDtypeStruct(v_t.shape, F32)],
        compiler_params=pltpu.CompilerParams(
            dimension_semantics=("arbitrary",), vmem_limit_bytes=V7X_VMEM_LIMIT),
        name=name,
    )(state_t, a_t, k_t, q_t, v_t)


def _state_kernel(s_ref, a_ref, k_ref, q_ref, v_ref, s_out_ref, o_ref, *, dk, dv, kb):
    j = pl.program_id(0)
    n = s_ref.shape[0]

    @pl.when(j == 0)
    def _():
        o_ref[...] = jnp.zeros_like(o_ref)

    group0 = j * (kb // SUBLANES)
    head_row = pl.multiple_of((j * kb // dk) * dv, dv)
    v_blk = v_ref[pl.ds(head_row, dv), :]
    acc = jnp.zeros(v_blk.shape, F32)
    for g in range(kb // SUBLANES):
        rows = slice(SUBLANES * g, SUBLANES * (g + 1))
        s_t = s_ref[:, rows, :].reshape(n * SUBLANES, dv).T
        s_new = a_ref[pl.ds(group0 + g, 1), :] * s_t + k_ref[pl.ds(group0 + g, 1), :] * v_blk
        s_out_ref[:, rows, :] = s_new.T.reshape(n, SUBLANES, dv)
        acc = acc + q_ref[pl.ds(group0 + g, 1), :] * s_new
    o_ref[pl.ds(head_row, dv), :] += acc


def _state_update(state, layer, a_il, k_il, q_il, v_rep, *, kb, name):
    _, n, heads, dk, dv = state.shape
    per_head = dk // kb
    full = lambda a: pl.BlockSpec(a.shape, lambda j: (0, 0))
    slab = lambda lead: pl.BlockSpec((None, n, None, kb, dv), lambda j: (lead, 0, j // per_head, j % per_head, 0))
    return pl.pallas_call(
        functools.partial(_state_kernel, dk=dk, dv=dv, kb=kb),
        grid=(heads * per_head,),
        in_specs=[slab(layer), full(a_il), full(k_il), full(q_il), full(v_rep)],
        out_specs=[slab(0), full(v_rep)],
        out_shape=[jax.ShapeDtypeStruct((1,) + state.shape[1:], F32), jax.ShapeDtypeStruct(v_rep.shape, F32)],
        compiler_params=pltpu.CompilerParams(
            dimension_semantics=("arbitrary",), vmem_limit_bytes=V7X_VMEM_LIMIT),
        name=name,
    )(state, a_il, k_il, q_il, v_rep)


def _lane_group_sum(o_ref, sel):
    o = o_ref[...]
    hi = o.astype(BF16)
    lo = (o - hi.astype(F32)).astype(BF16)
    return _dot_nt(sel, hi) + _dot_nt(sel, lo)


def _tail_sample_kernel(x_ref, o_hg_t_ref, o_ret_ref, sel_ref, gates_ref, conv0_ref, p_ref,
                        hm_hg_ref, hm_ret_ref, g_hg_ref, g_ret_ref, w_br_hg_ref, w_br_ret_ref, w_out_ref,
                        g_ffn_ref, w_up_ref, conv_w_ref, conv_b_ref, w_down_ref,
                        g_ple_ref, w_ple_gate_ref, w_ple_ref, g_final_ref,
                        y_ref, conv_out_ref):
    h = _merge_branches(
        x_ref[...], o_hg_t_ref[...].T, _lane_group_sum(o_ret_ref, sel_ref[...]),
        gates_ref[:, 0:HG_W], gates_ref[:, HG_W:HG_W + RET_W], gates_ref[:, HG_W + RET_W:],
        hm_hg_ref[...], hm_ret_ref[...], g_hg_ref[...], g_ret_ref[...],
        w_br_hg_ref[...], w_br_ret_ref[...], w_out_ref[...])
    xn = _rms(h, g_ffn_ref[...]).astype(BF16)
    a = _dot(xn, w_up_ref[:, 0:D_FF])
    gate_half = _dot(xn, w_up_ref[:, D_FF:2 * D_FF])
    cw = conv_w_ref[...]
    prev = conv0_ref[:, 1, :]
    ac = conv_b_ref[...] + conv0_ref[:, 0, :] * cw[0:1, :] + prev * cw[1:2, :] + a * cw[2:3, :]
    conv_out_ref[0, :, 0, :] = prev
    conv_out_ref[0, :, 1, :] = a
    y_ref[...] = _ffn_tail(h, ac, gate_half, p_ref[...], w_down_ref[...], g_ple_ref[...],
                           w_ple_gate_ref[...], w_ple_ref[...], g_final_ref[...])


def _tail_sample(x, o_hg_t, o_ret, sel, gates, conv0, p, *weights):
    n = x.shape[0]
    return pl.pallas_call(
        _tail_sample_kernel,
        out_shape=[jax.ShapeDtypeStruct((n, D_MODEL), F32),
                   jax.ShapeDtypeStruct((1, n, CONV_W - 1, D_FF), F32)],
        compiler_params=pltpu.CompilerParams(vmem_limit_bytes=V7X_VMEM_LIMIT),
        name="tail_sample",
    )(x, o_hg_t, o_ret, sel, gates, conv0, p, *weights)


def kernel(x_prompt, x_sample, state_hgrn, state_ret, state_conv, p_prompt, p_sample, lb_logits, w_in, g_mix, g_hg_out, g_ret_out, w_br_hg, w_br_ret, w_out, g_ffn, w_up, conv_w, conv_b, w_down, g_ple, w_ple, w_ple_gate, g_final):
    assert DEPTH == 1 and x_prompt.shape == (BATCH, SEQ, D_MODEL) and x_sample.shape == (DEC_BATCH, DEC_SEQ, D_MODEL)
    i = 0
    row = lambda a: a.reshape(1, -1).astype(F32)
    lb = jnp.cumsum(jax.nn.softmax(lb_logits.astype(F32), axis=0), axis=0)[i].reshape(1, HG_W)

    sel2, level = _level_tables()
    tri = np.tril(_block_mask(MIX_SUB, MIX_SUB, FAST_CHUNK, FAST_CHUNK))
    causal = np.tile(np.tril(np.ones((FAST_CHUNK, FAST_CHUNK), np.int32)), (1, HG_GROUP))
    hmask = (np.arange(HG_W)[None, :] // HG_FDIM) % HG_GROUP == np.arange(HG_GROUP)[:, None]
    consts = (
        jnp.asarray(sel2, BF16),
        jnp.asarray(np.concatenate([tri, tri], axis=1), BF16),
        jnp.asarray(hmask, F32),
        jnp.asarray(causal),
        jnp.asarray(np.tile(level, (1, HG_GROUP))),
        jnp.asarray(_block_mask(MXU_W, MXU_W, CHUNK, HG_FDIM), BF16),
        jnp.asarray(_block_mask(MXU_W, MXU_W, HG_IDIM, HG_FDIM), F32),
        jnp.asarray(_block_mask(MXU_W, MXU_W, RET_DK, RET_DV), F32),
    ) + _retention_tables()
    hm_hg = jnp.asarray(_block_mask(HG_W, HG_W, HG_IDIM, HG_IDIM) / HG_IDIM, BF16)
    hm_ret = jnp.asarray(_block_mask(RET_W, RET_W, RET_DV, RET_DV) / RET_DV, BF16)

    w_in_b = w_in[i].astype(BF16)
    mix_w = (hm_hg, hm_ret, row(g_hg_out[i]), row(g_ret_out[i]),
             w_br_hg[i].astype(BF16), w_br_ret[i].astype(BF16), w_out[i].astype(BF16))
    ffn_w = (row(g_ffn[i]), w_up[i].astype(BF16), conv_w[i].astype(F32), row(conv_b[i]), w_down[i].astype(BF16),
             row(g_ple[i]), w_ple_gate[i].astype(BF16), w_ple[i].astype(BF16), row(g_final))

    cos_p, sin_p = _rope_tables(0.0, SEQ)
    h_p, hg_state_p, ret_state_p = _mix_prompt(x_prompt, cos_p, sin_p, w_in_b, row(g_mix[i]), lb, consts, *mix_w)
    y_prompt, conv_p = _ffn_prompt(h_p, p_prompt[i], *ffn_w)

    n = DEC_BATCH
    xs = x_sample.reshape(n, D_MODEL)
    cos_s, sin_s = _rope_tables(float(PAST_LEN), DEC_SEQ)
    f_t, kk_t, q_t, v_t, kr, qr, vr, gates = _proj_sample(xs, cos_s, sin_s, w_in_b, row(g_mix[i]), lb)
    hg_new_t, o_hg_t = _state_update_tokens_minor(
        jnp.transpose(state_hgrn, (0, 2, 3, 4, 1)), i, f_t, kk_t, q_t, v_t, name="state_hgrn")
    hg_new = jnp.transpose(hg_new_t, (0, 4, 1, 2, 3))
    interleave = lambda a: a.reshape(n, -1, SUBLANES).transpose(1, 0, 2).reshape(-1, n * SUBLANES)
    repeat = lambda a: jnp.repeat(a.T, SUBLANES, axis=1)
    gamma = jnp.asarray(np.broadcast_to(np.repeat(np.exp(_log_gamma()), RET_DK // SUBLANES)[:, None],
                                        (RET_W // SUBLANES, n * SUBLANES)))
    ret_new, o_ret = _state_update(
        state_ret, i, gamma, interleave(kr), interleave(qr), repeat(vr), kb=RET_DK // 2, name="state_ret")
    sel = jnp.asarray(_block_mask(n, n * SUBLANES, 1, SUBLANES), BF16)
    y_sample, conv_s = _tail_sample(
        xs, o_hg_t, o_ret, sel, gates, state_conv[i], p_sample[i].reshape(n, PLE_DIM),
        *mix_w, *ffn_w)

    return (y_prompt, y_sample.reshape(n, DEC_SEQ, D_MODEL),
            hg_state_p, ret_state_p, conv_p[None],
            hg_new, ret_new,
            conv_s)
```

```python
import functools

import numpy as np
import jax
import jax.numpy as jnp
from jax import lax
from jax.experimental import pallas as pl
from jax.experimental.pallas import tpu as pltpu

D_MODEL = 1024
BATCH = 8
SEQ = 2048
DEPTH = 1
DEC_BATCH = 128
DEC_SEQ = 1
PAST_LEN = 16384
HG_HEADS = 8
HG_FDIM = 64
HG_IDIM = 64
HG_W = HG_HEADS * HG_FDIM
RET_HEADS = 4
RET_DK = 128
RET_DV = 128
RET_W = RET_HEADS * RET_DK
D_FF = 2816
CONV_W = 3
PLE_DIM = 256
CHUNK = 64
ROPE_BASE = 10000.0
EPS = 1e-6

C_HQ, C_HF, C_HI, C_HG = 0, 512, 1024, 1536
C_RQ, C_RK, C_RV, C_RG = 2048, 2560, 3072, 3584
C_GA, C_GB, C_END = 4096, 5120, 6144

N_LEVELS = 6
FAST_CHUNK = 128
FAST_DECAY_LIMIT = 80.0
HG_GROUP = 4
RET_GROUP = 2
MXU_W = 256
SUBLANES = 8

MIX_TILE = 256
MIX_SUB = 256
GATE_PIECES = (
    ((C_HG, HG_W, 0), (C_RG, RET_W, HG_W)),
    ((C_GA, 768, HG_W + RET_W),),
    ((C_GA + 768, 768, HG_W + RET_W + 768),),
    ((C_GA + 1536, 512, HG_W + RET_W + 1536),),
)
FFN_TILE = 512
FFN_SUB = 256
CARRY_ROWS = 8
V7X_VMEM_LIMIT = 56 * 1024 * 1024

F32 = jnp.float32
BF16 = jnp.bfloat16


def _level_tables():
    c = CHUNK
    t = np.arange(c)[:, None]
    r = np.arange(c)[None, :]
    blocks = [r <= t, r > t]
    for l in range(N_LEVELS):
        same = (t >> l) == (r >> l)
        upper = ((t >> l) & 1) == 1
        blocks.append(np.where(upper, same & (r <= t), same & (r > t)))
    sel = np.concatenate(blocks, axis=0).astype(np.float32)
    sel2 = np.concatenate([sel, sel], axis=1)
    s = r
    level = np.full((c, c), N_LEVELS + 1, np.int32)
    level[np.broadcast_to(s == t, (c, c))] = N_LEVELS
    x = t ^ s
    for l in range(N_LEVELS):
        hit = (s < t) & ((x >> l) == 1)
        level[hit] = l
    return sel2, level


def _block_mask(rows, cols, rblk, cblk):
    r = np.arange(rows)[:, None] // rblk
    c = np.arange(cols)[None, :] // cblk
    return (r == c).astype(np.float32)


def _rope_tables(pos0, length):
    d = RET_DK
    inv_freq = ROPE_BASE ** (-np.arange(0, d, 2, dtype=np.float64) / d)
    pos = pos0 + np.arange(length, dtype=np.float64)
    ang = pos[:, None] * inv_freq[None, :]
    cos, sin = np.cos(ang), np.sin(ang)
    cos_full = np.tile(np.concatenate([cos, cos], axis=-1), (1, RET_HEADS))
    sin_signed = np.tile(np.concatenate([-sin, sin], axis=-1), (1, RET_HEADS))
    return jnp.asarray(cos_full, F32), jnp.asarray(sin_signed, F32)


def _log_gamma():
    return np.log1p(-np.exp2(-5.0 - np.arange(RET_HEADS, dtype=np.float32))).astype(np.float32)


def _retention_tables():
    c = MIX_SUB
    log_gamma = _log_gamma()
    t = np.arange(c, dtype=np.float32)
    delta = t[:, None] - t[None, :]
    dmask = np.where(delta >= 0, np.exp(np.maximum(delta, 0.0) * log_gamma[:, None, None]), 0.0)
    q_decay = np.exp((t + 1.0)[None, :] * log_gamma[:, None])
    k_decay = np.exp((c - 1.0 - t)[None, :] * log_gamma[:, None])
    qdec = np.repeat(q_decay.T, RET_DK, axis=1)
    kdec = np.repeat(k_decay.T, RET_DK, axis=1)
    chunk_decay = np.exp(np.float32(c) * log_gamma)
    cd = np.repeat(chunk_decay, RET_DK).reshape(RET_HEADS // RET_GROUP, MXU_W, 1)
    cd = np.broadcast_to(cd, (RET_HEADS // RET_GROUP, MXU_W, MXU_W))
    return tuple(jnp.asarray(a, F32) for a in (dmask, qdec, kdec, cd))


def _rms(x, g):
    ms = jnp.mean(x * x, axis=-1, keepdims=True)
    return x * lax.rsqrt(ms + EPS) * g


def _silu(x):
    return x * jax.nn.sigmoid(x)


def _dot(a, b):
    return jnp.dot(a, b, preferred_element_type=F32)


def _dot_nt(a, b):
    return lax.dot_general(a, b, (((1,), (1,)), ((), ())), preferred_element_type=F32)


def _dot_tn(a, b):
    return lax.dot_general(a, b, (((0,), (0,)), ((), ())), preferred_element_type=F32)


def _rotary(x, cos_full, sin_signed):
    swapped = jnp.concatenate(
        [pltpu.roll(x[:, RET_DK * h:RET_DK * (h + 1)], RET_DK // 2, 1) for h in range(RET_HEADS)], axis=1)
    return x * cos_full + swapped * sin_signed


def _hgrn_inputs(hq, hf, lb):
    q = _silu(hq)
    f = lb + (1.0 - lb) * jax.nn.sigmoid(hf)
    return q, f, 1.0 - f


def _merge_branches(x, o_hg, o_ret, u_hg, u_rg, u_gab, hm_hg, hm_ret, g_hg, g_ret, w_br_hg, w_br_ret, w_out):
    ms = _dot((o_hg * o_hg).astype(BF16), hm_hg)
    a_hg = o_hg * lax.rsqrt(ms + EPS) * g_hg * _silu(u_hg)
    y_hg = _dot(a_hg.astype(BF16), w_br_hg)
    ms = _dot((o_ret * o_ret).astype(BF16), hm_ret)
    a_ret = o_ret * lax.rsqrt(ms + EPS) * g_ret * _silu(u_rg)
    y_ret = _dot(a_ret.astype(BF16), w_br_ret)
    mixed = jax.nn.sigmoid(u_gab[:, :D_MODEL]) * y_hg + jax.nn.sigmoid(u_gab[:, D_MODEL:]) * y_ret
    return x + _dot(mixed.astype(BF16), w_out)


def _ffn_tail(h, ac, gate_half, p, w_down, g_ple, w_ple_gate, w_ple, g_final):
    h = h + _dot((jax.nn.gelu(ac) * gate_half).astype(BF16), w_down)
    gate = jax.nn.sigmoid(_dot(_rms(h, g_ple).astype(BF16), w_ple_gate))
    h = h + gate * _dot(p.astype(BF16), w_ple)
    return _rms(h, g_final)


def _mix_prompt_kernel(x_ref, cos_ref, sin_ref, w_in_ref, g_mix_ref, lb_ref, sel_ref, tri_ref, hmask_ref, causal_ref, lvl_ref,
                       bd64_ref, bd64f_ref, bd128f_ref, dmask_ref, qdec_ref, kdec_ref, cd_ref,
                       hm_hg_ref, hm_ret_ref, g_hg_ref, g_ret_ref, w_br_hg_ref, w_br_ret_ref, w_out_ref,
                       h_ref, st_hg_out_ref, st_ret_out_ref,
                       xn_s, q_s, kk_s, lfh_s, lfl_s, v_s, vbd_s, qh_s, qi_s, kbd_s, ks_s, alast_s, o_hg_s, o_ret_s, gate_s, st_hg, st_ret):
    t_idx = pl.program_id(1)
    n_hg_groups = HG_HEADS // HG_GROUP
    n_ret_groups = RET_HEADS // RET_GROUP

    @pl.when(t_idx == 0)
    def _():
        st_hg[...] = jnp.zeros_like(st_hg)
        st_ret[...] = jnp.zeros_like(st_ret)

    subs = [slice(i * MIX_SUB, (i + 1) * MIX_SUB) for i in range(MIX_TILE // MIX_SUB)]
    chunk_sums = []
    for r in subs:
        xn = _rms(x_ref[r, :], g_mix_ref[...]).astype(BF16)
        xn_s[r, :] = xn

        u = _dot(xn, w_in_ref[:, C_HQ:C_HG])
        q, f, kk = _hgrn_inputs(u[:, 0:HG_W], u[:, HG_W:2 * HG_W], lb_ref[...])
        q_s[r, :] = q
        kk_s[r, :] = kk
        lf = jnp.log(f)
        lf_hi = lf.astype(BF16)
        lfh_s[r, :] = lf_hi
        lfl_s[r, :] = (lf - lf_hi.astype(F32)).astype(BF16)
        v = u[:, 2 * HG_W:3 * HG_W]
        v_s[r, :] = v.astype(BF16)
        for h in range(HG_GROUP):
            vbd_s[h, r, :] = (v * hmask_ref[h:h + 1, :]).astype(BF16)
        chunk_sums += [jnp.sum(lf[c * CHUNK:(c + 1) * CHUNK, :], axis=0, keepdims=True)
                       for c in range(MIX_SUB // CHUNK)]

        u = _dot(xn, w_in_ref[:, C_RQ:C_RG])
        cos_full = cos_ref[r, :]
        sin_signed = sin_ref[r, :]
        qr = _rotary(u[:, 0:RET_W], cos_full, sin_signed)
        kr = _rotary(u[:, RET_W:2 * RET_W], cos_full, sin_signed) * (RET_DK ** -0.5)
        vr = u[:, 2 * RET_W:3 * RET_W].astype(BF16)
        qrb = qr.astype(BF16)
        krb = kr.astype(BF16)
        o_heads = []
        for h in range(RET_HEADS):
            hs = slice(RET_DK * h, RET_DK * (h + 1))
            scores = _dot_nt(qrb[:, hs], krb[:, hs]) * dmask_ref[h]
            o_heads.append(_dot(scores.astype(BF16), vr[:, hs]))
        qd = (qr * qdec_ref[...]).astype(BF16)
        kd = (kr * kdec_ref[...]).astype(BF16)
        bd128f = bd128f_ref[...]
        o_inter = []
        for g in range(n_ret_groups):
            cs = slice(MXU_W * g, MXU_W * (g + 1))
            st = st_ret[g]
            o_inter.append(_dot(qd[:, cs], st.astype(BF16)))
            st_ret[g] = cd_ref[g] * st + _dot_tn(kd[:, cs], vr[:, cs]) * bd128f
        o_ret_s[r, :] = jnp.concatenate(o_heads, axis=1) + jnp.concatenate(o_inter, axis=1)

    mild = jnp.min(functools.reduce(jnp.minimum, chunk_sums)) >= -FAST_DECAY_LIMIT

    groups = [slice(MXU_W * g, MXU_W * (g + 1)) for g in range(n_hg_groups)]
    bd_tile = lambda ref, rows, cs: jnp.concatenate([ref[h, rows, cs] for h in range(HG_GROUP)], axis=0)

    def prepare_fast():
        per_sub = MIX_SUB // FAST_CHUNK
        rep = lambda rows: jnp.concatenate([jnp.broadcast_to(e, (FAST_CHUNK, HG_W)) for e in rows], axis=0)
        for i, r in enumerate(subs):
            lf2 = jnp.concatenate([lfh_s[r, :], lfl_s[r, :]], axis=0)
            cum = _dot(tri_ref[...], lf2)
            mids = [cum[c * FAST_CHUNK + FAST_CHUNK // 2 - 1:c * FAST_CHUNK + FAST_CHUNK // 2, :] for c in range(per_sub)]
            ends = [cum[(c + 1) * FAST_CHUNK - 1:(c + 1) * FAST_CHUNK, :] for c in range(per_sub)]
            mid = rep(mids)
            kk = kk_s[r, :]
            q = q_s[r, :]
            qh_s[r, :] = (q * jnp.exp(cum - mid)).astype(BF16)
            qi_s[r, :] = (q * jnp.exp(cum)).astype(BF16)
            k_mid = kk * jnp.exp(mid - cum)
            for h in range(HG_GROUP):
                kbd_s[h, r, :] = (k_mid * hmask_ref[h:h + 1, :]).astype(BF16)
            ks_s[r, :] = (kk * jnp.exp(rep(ends) - cum)).astype(BF16)
            for c, e in enumerate(ends):
                ci = i * per_sub + c
                alast_s[ci:ci + 1, :] = jnp.exp(e)

    def chunk_fast(c):
        rows = slice(c * FAST_CHUNK, (c + 1) * FAST_CHUNK)
        causal = causal_ref[...] != 0
        bd64f = bd64f_ref[...]
        o_parts = []
        for g, cs in enumerate(groups):
            scores = jnp.where(causal, _dot_nt(qh_s[rows, cs], bd_tile(kbd_s, rows, cs)), 0.0)
            st = st_hg[g]
            o_parts.append(_dot(scores.astype(BF16), bd_tile(vbd_s, rows, cs)) + _dot_nt(qi_s[rows, cs], st.astype(BF16)))
            st_hg[g] = st * alast_s[c:c + 1, cs] + _dot_tn(v_s[rows, cs], ks_s[rows, cs]) * bd64f
        o_hg_s[rows, :] = jnp.concatenate(o_parts, axis=1)

    def chunk_safe(c):
        rows = slice(c * CHUNK, (c + 1) * CHUNK)
        q = q_s[rows, :]
        kk = kk_s[rows, :]
        lf2 = jnp.concatenate([lfh_s[rows, :], lfl_s[rows, :]], axis=0)
        ex = jnp.exp(_dot(sel_ref[...], lf2))
        lvl = lvl_ref[...]
        bd64 = bd64_ref[...]
        bd64f = bd64f_ref[...]
        o_parts = []
        for g, cs in enumerate(groups):
            scores = jnp.zeros((CHUNK, MXU_W), F32)
            for l in range(N_LEVELS + 1):
                if l < N_LEVELS:
                    e = ex[CHUNK * (l + 2):CHUNK * (l + 3), cs]
                    ql = (q[:, cs] * e).astype(BF16)
                    kl = (kk[:, cs] * e).astype(BF16)
                else:
                    ql = q[:, cs].astype(BF16)
                    kl = kk[:, cs].astype(BF16)
                rhs = jnp.concatenate([kl] * HG_GROUP, axis=0) * bd64
                scores = jnp.where(lvl == l, _dot_nt(ql, rhs), scores)
            e_cum = ex[0:CHUNK, cs]
            e_suf = ex[CHUNK:2 * CHUNK, cs]
            st = st_hg[g]
            o_inter = _dot_nt((q[:, cs] * e_cum).astype(BF16), st.astype(BF16))
            o_parts.append(_dot(scores.astype(BF16), bd_tile(vbd_s, rows, cs)) + o_inter)
            upd = _dot_tn(v_s[rows, cs], (kk[:, cs] * e_suf).astype(BF16))
            st_hg[g] = st * e_cum[CHUNK - 1:CHUNK, :] + upd * bd64f
        o_hg_s[rows, :] = jnp.concatenate(o_parts, axis=1)

    def recurrences(fast):
        if fast:
            prepare_fast()
        for c in range(MIX_TILE // CHUNK):
            if not fast:
                chunk_safe(c)
            elif c % (FAST_CHUNK // CHUNK) == 0:
                chunk_fast(c * CHUNK // FAST_CHUNK)
            r = subs[c * CHUNK // MIX_SUB]
            for src, width, dst in GATE_PIECES[c % len(GATE_PIECES)]:
                gate_s[r, dst:dst + width] = _dot(xn_s[r, :], w_in_ref[:, src:src + width])

    pl.when(mild)(functools.partial(recurrences, True))
    pl.when(jnp.logical_not(mild))(functools.partial(recurrences, False))

    for r in subs:
        h_ref[r, :] = _merge_branches(
            x_ref[r, :], o_hg_s[r, :], o_ret_s[r, :],
            gate_s[r, 0:HG_W], gate_s[r, HG_W:HG_W + RET_W], gate_s[r, HG_W + RET_W:],
            hm_hg_ref[...], hm_ret_ref[...], g_hg_ref[...], g_ret_ref[...],
            w_br_hg_ref[...], w_br_ret_ref[...], w_out_ref[...])

    @pl.when(t_idx == pl.num_programs(1) - 1)
    def _():
        for g in range(n_hg_groups):
            st_t = st_hg[g].T
            for h in range(HG_GROUP):
                blk = slice(HG_FDIM * h, HG_FDIM * (h + 1))
                st_hg_out_ref[HG_GROUP * g + h] = st_t[blk, blk]
        for g in range(n_ret_groups):
            for h in range(RET_GROUP):
                blk = slice(RET_DK * h, RET_DK * (h + 1))
                st_ret_out_ref[RET_GROUP * g + h] = st_ret[g, blk, blk]


def _const_spec(shape):
    zeros = (0,) * len(shape)
    return pl.BlockSpec(shape, lambda *_: zeros, pipeline_mode=pl.Buffered(1))


def _mix_prompt(x, cos_full, sin_signed, w_in, g_mix, lb, consts, hm_hg, hm_ret, g_hg, g_ret,
                w_br_hg, w_br_ret, w_out):
    b, l, d = x.shape
    nt = l // MIX_TILE
    tile = lambda w: pl.BlockSpec((None, MIX_TILE, w), lambda i, j: (i, j, 0))
    table = lambda w: pl.BlockSpec((MIX_TILE, w), lambda i, j: (j, 0))
    n_hg_groups = HG_HEADS // HG_GROUP
    n_ret_groups = RET_HEADS // RET_GROUP
    const_args = (w_in, g_mix, lb) + tuple(consts) + (hm_hg, hm_ret, g_hg, g_ret, w_br_hg, w_br_ret, w_out)
    state_spec = lambda h, dk, dv: pl.BlockSpec((None, None, h, dk, dv), lambda i, j: (0, i, 0, 0, 0))
    return pl.pallas_call(
        _mix_prompt_kernel,
        grid=(b, nt),
        in_specs=[tile(d), table(RET_W), table(RET_W)] + [_const_spec(a.shape) for a in const_args],
        out_specs=[tile(d), state_spec(HG_HEADS, HG_FDIM, HG_IDIM), state_spec(RET_HEADS, RET_DK, RET_DV)],
        out_shape=[jax.ShapeDtypeStruct((b, l, d), F32),
                   jax.ShapeDtypeStruct((1, b, HG_HEADS, HG_FDIM, HG_IDIM), F32),
                   jax.ShapeDtypeStruct((1, b, RET_HEADS, RET_DK, RET_DV), F32)],
        scratch_shapes=[
            pltpu.VMEM((MIX_TILE, d), BF16),
            pltpu.VMEM((MIX_TILE, HG_W), F32),
            pltpu.VMEM((MIX_TILE, HG_W), F32),
            pltpu.VMEM((MIX_TILE, HG_W), BF16),
            pltpu.VMEM((MIX_TILE, HG_W), BF16),
            pltpu.VMEM((MIX_TILE, HG_W), BF16),
            pltpu.VMEM((HG_GROUP, MIX_TILE, HG_W), BF16),
            pltpu.VMEM((MIX_TILE, HG_W), BF16),
            pltpu.VMEM((MIX_TILE, HG_W), BF16),
            pltpu.VMEM((HG_GROUP, MIX_TILE, HG_W), BF16),
            pltpu.VMEM((MIX_TILE, HG_W), BF16),
            pltpu.VMEM((max(MIX_TILE // CHUNK, SUBLANES), HG_W), F32),
            pltpu.VMEM((MIX_TILE, HG_W), F32),
            pltpu.VMEM((MIX_TILE, RET_W), F32),
            pltpu.VMEM((MIX_TILE, HG_W + RET_W + 2 * D_MODEL), F32),
            pltpu.VMEM((n_hg_groups, MXU_W, MXU_W), F32),
            pltpu.VMEM((n_ret_groups, MXU_W, MXU_W), F32),
        ],
        compiler_params=pltpu.CompilerParams(
            dimension_semantics=("arbitrary", "arbitrary"), vmem_limit_bytes=V7X_VMEM_LIMIT),
        name="mix_prompt",
    )(x, cos_full, sin_signed, *const_args)


def _ffn_prompt_kernel(h_ref, p_ref, g_ffn_ref, w_up_ref, conv_w_ref, conv_b_ref, w_down_ref,
                       g_ple_ref, w_ple_gate_ref, w_ple_ref, g_final_ref,
                       y_ref, conv_out_ref, a_ext, gate_s):
    t_idx = pl.program_id(1)
    tl = FFN_TILE

    @pl.when(t_idx == 0)
    def _():
        a_ext[0:CARRY_ROWS, :] = jnp.zeros((CARRY_ROWS, D_FF), F32)

    @pl.when(t_idx > 0)
    def _():
        a_ext[0:CARRY_ROWS, :] = a_ext[tl:tl + CARRY_ROWS, :]

    subs = [slice(i * FFN_SUB, (i + 1) * FFN_SUB) for i in range(tl // FFN_SUB)]
    for r in subs:
        xn = _rms(h_ref[r, :], g_ffn_ref[...]).astype(BF16)
        a_ext[CARRY_ROWS + r.start:CARRY_ROWS + r.stop, :] = _dot(xn, w_up_ref[:, 0:D_FF])
        gate_s[r, :] = _dot(xn, w_up_ref[:, D_FF:2 * D_FF])
    cw = conv_w_ref[...]
    for r in subs:
        ac = conv_b_ref[...]
        for j in range(CONV_W):
            off = CARRY_ROWS - (CONV_W - 1) + j
            ac = ac + a_ext[off + r.start:off + r.stop, :] * cw[j:j + 1, :]
        y_ref[r, :] = _ffn_tail(h_ref[r, :], ac, gate_s[r, :], p_ref[r, :], w_down_ref[...], g_ple_ref[...],
                                w_ple_gate_ref[...], w_ple_ref[...], g_final_ref[...])

    @pl.when(t_idx == pl.num_programs(1) - 1)
    def _():
        conv_out_ref[...] = a_ext[CARRY_ROWS + tl - (CONV_W - 1):CARRY_ROWS + tl, :]


def _ffn_prompt(h, p, g_ffn, w_up, conv_w, conv_b, w_down, g_ple, w_ple_gate, w_ple, g_final):
    b, l, d = h.shape
    nt = l // FFN_TILE
    tile = lambda w: pl.BlockSpec((None, FFN_TILE, w), lambda i, j: (i, j, 0))
    const_args = (g_ffn, w_up, conv_w, conv_b, w_down, g_ple, w_ple_gate, w_ple, g_final)
    return pl.pallas_call(
        _ffn_prompt_kernel,
        grid=(b, nt),
        in_specs=[tile(d), tile(PLE_DIM)] + [_const_spec(a.shape) for a in const_args],
        out_specs=[tile(d), pl.BlockSpec((None, CONV_W - 1, D_FF), lambda i, j: (i, 0, 0))],
        out_shape=[jax.ShapeDtypeStruct((b, l, d), F32),
                   jax.ShapeDtypeStruct((b, CONV_W - 1, D_FF), F32)],
        scratch_shapes=[pltpu.VMEM((FFN_TILE + CARRY_ROWS, D_FF), F32), pltpu.VMEM((FFN_TILE, D_FF), F32)],
        compiler_params=pltpu.CompilerParams(
            dimension_semantics=("arbitrary", "arbitrary"), vmem_limit_bytes=V7X_VMEM_LIMIT),
        name="ffn_prompt",
    )(h, p, *const_args)


def _proj_sample_kernel(x_ref, cos_ref, sin_ref, w_in_ref, g_mix_ref, lb_ref,
                        f_ref, kk_ref, q_ref, v_ref, kr_ref, qr_ref, vr_ref, gates_ref):
    xn = _rms(x_ref[...], g_mix_ref[...]).astype(BF16)
    u = _dot(xn, w_in_ref[:, C_HQ:C_HG])
    q, f, kk = _hgrn_inputs(u[:, 0:HG_W], u[:, HG_W:2 * HG_W], lb_ref[...])
    f_ref[...] = f.T
    kk_ref[...] = kk.T
    q_ref[...] = q.T
    v_ref[...] = u[:, 2 * HG_W:3 * HG_W].T
    u = _dot(xn, w_in_ref[:, C_RQ:C_RG])
    cos_full = cos_ref[...]
    sin_signed = sin_ref[...]
    qr_ref[...] = _rotary(u[:, 0:RET_W], cos_full, sin_signed)
    kr_ref[...] = _rotary(u[:, RET_W:2 * RET_W], cos_full, sin_signed) * (RET_DK ** -0.5)
    vr_ref[...] = u[:, 2 * RET_W:3 * RET_W]
    gates_ref[:, 0:HG_W] = _dot(xn, w_in_ref[:, C_HG:C_RQ])
    gates_ref[:, HG_W:HG_W + RET_W] = _dot(xn, w_in_ref[:, C_RG:C_GA])
    gates_ref[:, HG_W + RET_W:] = _dot(xn, w_in_ref[:, C_GA:C_END])


def _proj_sample(x, cos_full, sin_signed, w_in, g_mix, lb):
    n = x.shape[0]
    col = jax.ShapeDtypeStruct((HG_W, n), F32)
    row = jax.ShapeDtypeStruct((n, RET_W), F32)
    return pl.pallas_call(
        _proj_sample_kernel,
        out_shape=[col] * 4 + [row] * 3 + [jax.ShapeDtypeStruct((n, HG_W + RET_W + 2 * D_MODEL), F32)],
        compiler_params=pltpu.CompilerParams(vmem_limit_bytes=V7X_VMEM_LIMIT),
        name="proj_sample",
    )(x, cos_full, sin_signed, w_in, g_mix, lb)


def _state_tokens_minor_kernel(s_ref, a_t_ref, k_t_ref, q_t_ref, v_t_ref, s_out_ref, o_t_ref):
    dk, dv, _ = s_ref.shape
    row0 = pl.program_id(0) * dk
    v_blk = v_t_ref[...]

    def body(k, acc):
        s_new = a_t_ref[pl.ds(row0 + k, 1), :] * s_ref[k] + k_t_ref[pl.ds(row0 + k, 1), :] * v_blk
        s_out_ref[k] = s_new
        return acc + q_t_ref[pl.ds(row0 + k, 1), :] * s_new

    o_t_ref[...] = lax.fori_loop(0, dk, body, jnp.zeros(v_blk.shape, F32), unroll=SUBLANES)


def _state_update_tokens_minor(state_t, layer, a_t, k_t, q_t, v_t, *, name):
    _, heads, dk, dv, n = state_t.shape
    full = lambda a: pl.BlockSpec(a.shape, lambda h: (0, 0))
    per_head = pl.BlockSpec((dv, n), lambda h: (h, 0))
    slab = lambda lead: pl.BlockSpec((None, None, dk, dv, n), lambda h: (lead, h, 0, 0, 0))
    return pl.pallas_call(
        _state_tokens_minor_kernel,
        grid=(heads,),
        in_specs=[slab(layer), full(a_t), full(k_t), full(q_t), per_head],
        out_specs=[slab(0), per_head],
        out_shape=[jax.ShapeDtypeStruct((1,) + state_t.shape[1:], F32), jax.ShapeDtypeStruct(v_t.shape, F32)],
        compiler_params=pltpu.CompilerParams(
            dimension_semantics=("arbitrary",), vmem_limit_bytes=V7X_VMEM_LIMIT),
        name=name,
    )(state_t, a_t, k_t, q_t, v_t)


def _state_kernel(s_ref, a_ref, k_ref, q_ref, v_ref, s_out_ref, o_ref, *, dk, dv, kb):
    j = pl.program_id(0)
    n = s_ref.shape[0]

    @pl.when(j == 0)
    def _():
        o_ref[...] = jnp.zeros_like(o_ref)

    group0 = j * (kb // SUBLANES)
    head_row = pl.multiple_of((j * kb // dk) * dv, dv)
    v_blk = v_ref[pl.ds(head_row, dv), :]
    acc = jnp.zeros(v_blk.shape, F32)
    for g in range(kb // SUBLANES):
        rows = slice(SUBLANES * g, SUBLANES * (g + 1))
        s_t = s_ref[:, rows, :].reshape(n * SUBLANES, dv).T
        s_new = a_ref[pl.ds(group0 + g, 1), :] * s_t + k_ref[pl.ds(group0 + g, 1), :] * v_blk
        s_out_ref[:, rows, :] = s_new.T.reshape(n, SUBLANES, dv)
        acc = acc + q_ref[pl.ds(group0 + g, 1), :] * s_new
    o_ref[pl.ds(head_row, dv), :] += acc


def _state_update(state, layer, a_il, k_il, q_il, v_rep, *, kb, name):
    _, n, heads, dk, dv = state.shape
    per_head = dk // kb
    full = lambda a: pl.BlockSpec(a.shape, lambda j: (0, 0))
    slab = lambda lead: pl.BlockSpec((None, n, None, kb, dv), lambda j: (lead, 0, j // per_head, j % per_head, 0))
    return pl.pallas_call(
        functools.partial(_state_kernel, dk=dk, dv=dv, kb=kb),
        grid=(heads * per_head,),
        in_specs=[slab(layer), full(a_il), full(k_il), full(q_il), full(v_rep)],
        out_specs=[slab(0), full(v_rep)],
        out_shape=[jax.ShapeDtypeStruct((1,) + state.shape[1:], F32), jax.ShapeDtypeStruct(v_rep.shape, F32)],
        compiler_params=pltpu.CompilerParams(
            dimension_semantics=("arbitrary",), vmem_limit_bytes=V7X_VMEM_LIMIT),
        name=name,
    )(state, a_il, k_il, q_il, v_rep)


def _lane_group_sum(o_ref, sel):
    o = o_ref[...]
    hi = o.astype(BF16)
    lo = (o - hi.astype(F32)).astype(BF16)
    return _dot_nt(sel, hi) + _dot_nt(sel, lo)


def _tail_sample_kernel(x_ref, o_hg_t_ref, o_ret_ref, sel_ref, gates_ref, conv0_ref, p_ref,
                        hm_hg_ref, hm_ret_ref, g_hg_ref, g_ret_ref, w_br_hg_ref, w_br_ret_ref, w_out_ref,
                        g_ffn_ref, w_up_ref, conv_w_ref, conv_b_ref, w_down_ref,
                        g_ple_ref, w_ple_gate_ref, w_ple_ref, g_final_ref,
                        y_ref, conv_out_ref):
    h = _merge_branches(
        x_ref[...], o_hg_t_ref[...].T, _lane_group_sum(o_ret_ref, sel_ref[...]),
        gates_ref[:, 0:HG_W], gates_ref[:, HG_W:HG_W + RET_W], gates_ref[:, HG_W + RET_W:],
        hm_hg_ref[...], hm_ret_ref[...], g_hg_ref[...], g_ret_ref[...],
        w_br_hg_ref[...], w_br_ret_ref[...], w_out_ref[...])
    xn = _rms(h, g_ffn_ref[...]).astype(BF16)
    a = _dot(xn, w_up_ref[:, 0:D_FF])
    gate_half = _dot(xn, w_up_ref[:, D_FF:2 * D_FF])
    cw = conv_w_ref[...]
    prev = conv0_ref[:, 1, :]
    ac = conv_b_ref[...] + conv0_ref[:, 0, :] * cw[0:1, :] + prev * cw[1:2, :] + a * cw[2:3, :]
    conv_out_ref[0, :, 0, :] = prev
    conv_out_ref[0, :, 1, :] = a
    y_ref[...] = _ffn_tail(h, ac, gate_half, p_ref[...], w_down_ref[...], g_ple_ref[...],
                           w_ple_gate_ref[...], w_ple_ref[...], g_final_ref[...])


def _tail_sample(x, o_hg_t, o_ret, sel, gates, conv0, p, *weights):
    n = x.shape[0]
    return pl.pallas_call(
        _tail_sample_kernel,
        out_shape=[jax.ShapeDtypeStruct((n, D_MODEL), F32),
                   jax.ShapeDtypeStruct((1, n, CONV_W - 1, D_FF), F32)],
        compiler_params=pltpu.CompilerParams(vmem_limit_bytes=V7X_VMEM_LIMIT),
        name="tail_sample",
    )(x, o_hg_t, o_ret, sel, gates, conv0, p, *weights)


def kernel(x_prompt, x_sample, state_hgrn, state_ret, state_conv, p_prompt, p_sample, lb_logits, w_in, g_mix, g_hg_out, g_ret_out, w_br_hg, w_br_ret, w_out, g_ffn, w_up, conv_w, conv_b, w_down, g_ple, w_ple, w_ple_gate, g_final):
    assert DEPTH == 1 and x_prompt.shape == (BATCH, SEQ, D_MODEL) and x_sample.shape == (DEC_BATCH, DEC_SEQ, D_MODEL)
    i = 0
    row = lambda a: a.reshape(1, -1).astype(F32)
    lb = jnp.cumsum(jax.nn.softmax(lb_logits.astype(F32), axis=0), axis=0)[i].reshape(1, HG_W)

    sel2, level = _level_tables()
    tri = np.tril(_block_mask(MIX_SUB, MIX_SUB, FAST_CHUNK, FAST_CHUNK))
    causal = np.tile(np.tril(np.ones((FAST_CHUNK, FAST_CHUNK), np.int32)), (1, HG_GROUP))
    hmask = (np.arange(HG_W)[None, :] // HG_FDIM) % HG_GROUP == np.arange(HG_GROUP)[:, None]
    consts = (
        jnp.asarray(sel2, BF16),
        jnp.asarray(np.concatenate([tri, tri], axis=1), BF16),
        jnp.asarray(hmask, F32),
        jnp.asarray(causal),
        jnp.asarray(np.tile(level, (1, HG_GROUP))),
        jnp.asarray(_block_mask(MXU_W, MXU_W, CHUNK, HG_FDIM), BF16),
        jnp.asarray(_block_mask(MXU_W, MXU_W, HG_IDIM, HG_FDIM), F32),
        jnp.asarray(_block_mask(MXU_W, MXU_W, RET_DK, RET_DV), F32),
    ) + _retention_tables()
    hm_hg = jnp.asarray(_block_mask(HG_W, HG_W, HG_IDIM, HG_IDIM) / HG_IDIM, BF16)
    hm_ret = jnp.asarray(_block_mask(RET_W, RET_W, RET_DV, RET_DV) / RET_DV, BF16)

    w_in_b = w_in[i].astype(BF16)
    mix_w = (hm_hg, hm_ret, row(g_hg_out[i]), row(g_ret_out[i]),
             w_br_hg[i].astype(BF16), w_br_ret[i].astype(BF16), w_out[i].astype(BF16))
    ffn_w = (row(g_ffn[i]), w_up[i].astype(BF16), conv_w[i].astype(F32), row(conv_b[i]), w_down[i].astype(BF16),
             row(g_ple[i]), w_ple_gate[i].astype(BF16), w_ple[i].astype(BF16), row(g_final))

    cos_p, sin_p = _rope_tables(0.0, SEQ)
    h_p, hg_state_p, ret_state_p = _mix_prompt(x_prompt, cos_p, sin_p, w_in_b, row(g_mix[i]), lb, consts, *mix_w)
    y_prompt, conv_p = _ffn_prompt(h_p, p_prompt[i], *ffn_w)

    n = DEC_BATCH
    xs = x_sample.reshape(n, D_MODEL)
    cos_s, sin_s = _rope_tables(float(PAST_LEN), DEC_SEQ)
    f_t, kk_t, q_t, v_t, kr, qr, vr, gates = _proj_sample(xs, cos_s, sin_s, w_in_b, row(g_mix[i]), lb)
    hg_new_t, o_hg_t = _state_update_tokens_minor(
        jnp.transpose(state_hgrn, (0, 2, 3, 4, 1)), i, f_t, kk_t, q_t, v_t, name="state_hgrn")
    hg_new = jnp.transpose(hg_new_t, (0, 4, 1, 2, 3))
    interleave = lambda a: a.reshape(n, -1, SUBLANES).transpose(1, 0, 2).reshape(-1, n * SUBLANES)
    repeat = lambda a: jnp.repeat(a.T, SUBLANES, axis=1)
    gamma = jnp.asarray(np.broadcast_to(np.repeat(np.exp(_log_gamma()), RET_DK // SUBLANES)[:, None],
                                        (RET_W // SUBLANES, n * SUBLANES)))
    ret_new, o_ret = _state_update(
        state_ret, i, gamma, interleave(kr), interleave(qr), repeat(vr), kb=RET_DK // 2, name="state_ret")
    sel = jnp.asarray(_block_mask(n, n * SUBLANES, 1, SUBLANES), BF16)
    y_sample, conv_s = _tail_sample(
        xs, o_hg_t, o_ret, sel, gates, state_conv[i], p_sample[i].reshape(n, PLE_DIM),
        *mix_w, *ffn_w)

    return (y_prompt, y_sample.reshape(n, DEC_SEQ, D_MODEL),
            hg_state_p, ret_state_p, conv_p[None],
            hg_new, ret_new,
            conv_s)
```

```python
import functools

import numpy as np
import jax
import jax.numpy as jnp
from jax import lax
from jax.experimental import pallas as pl
from jax.experimental.pallas import tpu as pltpu

D_MODEL = 1024
BATCH = 8
SEQ = 2048
DEPTH = 1
DEC_BATCH = 128
DEC_SEQ = 1
PAST_LEN = 16384
HG_HEADS = 8
HG_FDIM = 64
HG_IDIM = 64
HG_W = HG_HEADS * HG_FDIM
RET_HEADS = 4
RET_DK = 128
RET_DV = 128
RET_W = RET_HEADS * RET_DK
D_FF = 2816
CONV_W = 3
PLE_DIM = 256
CHUNK = 64
ROPE_BASE = 10000.0
EPS = 1e-6

C_HQ, C_HF, C_HI, C_HG = 0, 512, 1024, 1536
C_RQ, C_RK, C_RV, C_RG = 2048, 2560, 3072, 3584
C_GA, C_GB, C_END = 4096, 5120, 6144

N_LEVELS = 6
FAST_CHUNK = 128
FAST_DECAY_LIMIT = 80.0
HG_GROUP = 4
RET_GROUP = 2
MXU_W = 256
SUBLANES = 8
LANES = 128

MIX_TILE = 256
MIX_SUB = 256
GATE_PIECES = (
    ((C_HG, HG_W, 0), (C_RG, RET_W, HG_W)),
    ((C_GA, 768, HG_W + RET_W),),
    ((C_GA + 768, 768, HG_W + RET_W + 768),),
    ((C_GA + 1536, 512, HG_W + RET_W + 1536),),
)
FFN_TILE = 512
FFN_SUB = 256
CARRY_ROWS = 8
V7X_VMEM_LIMIT = 56 * 1024 * 1024

F32 = jnp.float32
BF16 = jnp.bfloat16


def _level_tables():
    c = CHUNK
    t = np.arange(c)[:, None]
    r = np.arange(c)[None, :]
    blocks = [r <= t, r > t]
    for l in range(N_LEVELS):
        same = (t >> l) == (r >> l)
        upper = ((t >> l) & 1) == 1
        blocks.append(np.where(upper, same & (r <= t), same & (r > t)))
    sel = np.concatenate(blocks, axis=0).astype(np.float32)
    sel2 = np.concatenate([sel, sel], axis=1)
    s = r
    level = np.full((c, c), N_LEVELS + 1, np.int32)
    level[np.broadcast_to(s == t, (c, c))] = N_LEVELS
    x = t ^ s
    for l in range(N_LEVELS):
        hit = (s < t) & ((x >> l) == 1)
        level[hit] = l
    return sel2, level


def _block_mask(rows, cols, rblk, cblk):
    r = np.arange(rows)[:, None] // rblk
    c = np.arange(cols)[None, :] // cblk
    return (r == c).astype(np.float32)


def _rope_tables(pos0, length):
    d = RET_DK
    inv_freq = ROPE_BASE ** (-np.arange(0, d, 2, dtype=np.float64) / d)
    pos = pos0 + np.arange(length, dtype=np.float64)
    ang = pos[:, None] * inv_freq[None, :]
    cos, sin = np.cos(ang), np.sin(ang)
    cos_full = np.tile(np.concatenate([cos, cos], axis=-1), (1, RET_HEADS))
    sin_signed = np.tile(np.concatenate([-sin, sin], axis=-1), (1, RET_HEADS))
    return jnp.asarray(cos_full, F32), jnp.asarray(sin_signed, F32)


def _log_gamma():
    return np.log1p(-np.exp2(-5.0 - np.arange(RET_HEADS, dtype=np.float32))).astype(np.float32)


def _retention_tables():
    c = MIX_SUB
    log_gamma = _log_gamma()
    t = np.arange(c, dtype=np.float32)
    delta = t[:, None] - t[None, :]
    dmask = np.where(delta >= 0, np.exp(np.maximum(delta, 0.0) * log_gamma[:, None, None]), 0.0)
    q_decay = np.exp((t + 1.0)[None, :] * log_gamma[:, None])
    k_decay = np.exp((c - 1.0 - t)[None, :] * log_gamma[:, None])
    qdec = np.repeat(q_decay.T, RET_DK, axis=1)
    kdec = np.repeat(k_decay.T, RET_DK, axis=1)
    chunk_decay = np.exp(np.float32(c) * log_gamma)
    cd = np.repeat(chunk_decay, RET_DK).reshape(RET_HEADS // RET_GROUP, MXU_W, 1)
    cd = np.broadcast_to(cd, (RET_HEADS // RET_GROUP, MXU_W, MXU_W))
    return tuple(jnp.asarray(a, F32) for a in (dmask, qdec, kdec, cd))


def _rms(x, g):
    ms = jnp.mean(x * x, axis=-1, keepdims=True)
    return x * lax.rsqrt(ms + EPS) * g


def _silu(x):
    return x * jax.nn.sigmoid(x)


def _dot(a, b):
    return jnp.dot(a, b, preferred_element_type=F32)


def _dot_nt(a, b):
    return lax.dot_general(a, b, (((1,), (1,)), ((), ())), preferred_element_type=F32)


def _dot_tn(a, b):
    return lax.dot_general(a, b, (((0,), (0,)), ((), ())), preferred_element_type=F32)


def _rotary(x, cos_full, sin_signed):
    swapped = jnp.concatenate(
        [pltpu.roll(x[:, RET_DK * h:RET_DK * (h + 1)], RET_DK // 2, 1) for h in range(RET_HEADS)], axis=1)
    return x * cos_full + swapped * sin_signed


def _hgrn_inputs(hq, hf, lb):
    q = _silu(hq)
    f = lb + (1.0 - lb) * jax.nn.sigmoid(hf)
    return q, f, 1.0 - f


def _head_mean_square(o, head_w):
    parts = []
    for g in range(o.shape[1] // LANES):
        sq = o[:, LANES * g:LANES * (g + 1)]
        sq = sq * sq
        if head_w == LANES:
            ms = jnp.broadcast_to(jnp.sum(sq, axis=-1, keepdims=True), sq.shape)
        else:
            low = lax.broadcasted_iota(jnp.int32, sq.shape, 1) < head_w
            lo = jnp.sum(jnp.where(low, sq, 0.0), axis=-1, keepdims=True)
            hi = jnp.sum(jnp.where(low, 0.0, sq), axis=-1, keepdims=True)
            ms = jnp.where(low, lo, hi)
        parts.append(ms * (1.0 / head_w))
    return jnp.concatenate(parts, axis=1)


def _merge_branches(x, o_hg, o_ret, u_hg, u_rg, u_gab, g_hg, g_ret, w_br_hg, w_br_ret, w_out):
    a_hg = o_hg * lax.rsqrt(_head_mean_square(o_hg, HG_IDIM) + EPS) * g_hg * _silu(u_hg)
    y_hg = _dot(a_hg.astype(BF16), w_br_hg)
    a_ret = o_ret * lax.rsqrt(_head_mean_square(o_ret, RET_DV) + EPS) * g_ret * _silu(u_rg)
    y_ret = _dot(a_ret.astype(BF16), w_br_ret)
    mixed = jax.nn.sigmoid(u_gab[:, :D_MODEL]) * y_hg + jax.nn.sigmoid(u_gab[:, D_MODEL:]) * y_ret
    return x + _dot(mixed.astype(BF16), w_out)


def _ffn_tail(h, ac, gate_half, p, w_down, g_ple, w_ple_gate, w_ple, g_final):
    h = h + _dot((jax.nn.gelu(ac) * gate_half).astype(BF16), w_down)
    gate = jax.nn.sigmoid(_dot(_rms(h, g_ple).astype(BF16), w_ple_gate))
    h = h + gate * _dot(p.astype(BF16), w_ple)
    return _rms(h, g_final)


def _mix_prompt_kernel(x_ref, cos_ref, sin_ref, w_in_ref, g_mix_ref, lb_ref, sel_ref, tri_ref, hmask_ref, causal_ref, lvl_ref,
                       bd64_ref, bd64f_ref, bd128f_ref, dmask_ref, qdec_ref, kdec_ref, cd_ref,
                       g_hg_ref, g_ret_ref, w_br_hg_ref, w_br_ret_ref, w_out_ref,
                       h_ref, st_hg_out_ref, st_ret_out_ref,
                       xn_s, q_s, kk_s, lfh_s, lfl_s, v_s, vbd_s, qh_s, qi_s, kbd_s, ks_s, alast_s, o_hg_s, o_ret_s, gate_s, st_hg, st_ret):
    t_idx = pl.program_id(1)
    n_hg_groups = HG_HEADS // HG_GROUP
    n_ret_groups = RET_HEADS // RET_GROUP

    @pl.when(t_idx == 0)
    def _():
        st_hg[...] = jnp.zeros_like(st_hg)
        st_ret[...] = jnp.zeros_like(st_ret)

    subs = [slice(i * MIX_SUB, (i + 1) * MIX_SUB) for i in range(MIX_TILE // MIX_SUB)]
    chunk_sums = []
    for r in subs:
        xn = _rms(x_ref[r, :], g_mix_ref[...]).astype(BF16)
        xn_s[r, :] = xn

        u = _dot(xn, w_in_ref[:, C_HQ:C_HG])
        q, f, kk = _hgrn_inputs(u[:, 0:HG_W], u[:, HG_W:2 * HG_W], lb_ref[...])
        q_s[r, :] = q
        kk_s[r, :] = kk
        lf = jnp.log(f)
        lf_hi = lf.astype(BF16)
        lfh_s[r, :] = lf_hi
        lfl_s[r, :] = (lf - lf_hi.astype(F32)).astype(BF16)
        v = u[:, 2 * HG_W:3 * HG_W]
        v_s[r, :] = v.astype(BF16)
        for h in range(HG_GROUP):
            vbd_s[h, r, :] = (v * hmask_ref[h:h + 1, :]).astype(BF16)
        chunk_sums += [jnp.sum(lf[c * CHUNK:(c + 1) * CHUNK, :], axis=0, keepdims=True)
                       for c in range(MIX_SUB // CHUNK)]

        u = _dot(xn, w_in_ref[:, C_RQ:C_RG])
        cos_full = cos_ref[r, :]
        sin_signed = sin_ref[r, :]
        qr = _rotary(u[:, 0:RET_W], cos_full, sin_signed)
        kr = _rotary(u[:, RET_W:2 * RET_W], cos_full, sin_signed) * (RET_DK ** -0.5)
        vr = u[:, 2 * RET_W:3 * RET_W].astype(BF16)
        qrb = qr.astype(BF16)
        krb = kr.astype(BF16)
        o_heads = []
        for h in range(RET_HEADS):
            hs = slice(RET_DK * h, RET_DK * (h + 1))
            scores = _dot_nt(qrb[:, hs], krb[:, hs]) * dmask_ref[h]
            o_heads.append(_dot(scores.astype(BF16), vr[:, hs]))
        qd = (qr * qdec_ref[...]).astype(BF16)
        kd = (kr * kdec_ref[...]).astype(BF16)
        bd128f = bd128f_ref[...]
        o_inter = []
        for g in range(n_ret_groups):
            cs = slice(MXU_W * g, MXU_W * (g + 1))
            st = st_ret[g]
            o_inter.append(_dot(qd[:, cs], st.astype(BF16)))
            st_ret[g] = cd_ref[g] * st + _dot_tn(kd[:, cs], vr[:, cs]) * bd128f
        o_ret_s[r, :] = jnp.concatenate(o_heads, axis=1) + jnp.concatenate(o_inter, axis=1)

    mild = jnp.min(functools.reduce(jnp.minimum, chunk_sums)) >= -FAST_DECAY_LIMIT

    groups = [slice(MXU_W * g, MXU_W * (g + 1)) for g in range(n_hg_groups)]
    bd_tile = lambda ref, rows, cs: jnp.concatenate([ref[h, rows, cs] for h in range(HG_GROUP)], axis=0)

    def prepare_fast():
        per_sub = MIX_SUB // FAST_CHUNK
        rep = lambda rows: jnp.concatenate([jnp.broadcast_to(e, (FAST_CHUNK, HG_W)) for e in rows], axis=0)
        for i, r in enumerate(subs):
            lf2 = jnp.concatenate([lfh_s[r, :], lfl_s[r, :]], axis=0)
            cum = _dot(tri_ref[...], lf2)
            mids = [cum[c * FAST_CHUNK + FAST_CHUNK // 2 - 1:c * FAST_CHUNK + FAST_CHUNK // 2, :] for c in range(per_sub)]
            ends = [cum[(c + 1) * FAST_CHUNK - 1:(c + 1) * FAST_CHUNK, :] for c in range(per_sub)]
            mid = rep(mids)
            kk = kk_s[r, :]
            q = q_s[r, :]
            qh_s[r, :] = (q * jnp.exp(cum - mid)).astype(BF16)
            qi_s[r, :] = (q * jnp.exp(cum)).astype(BF16)
            k_mid = kk * jnp.exp(mid - cum)
            for h in range(HG_GROUP):
                kbd_s[h, r, :] = (k_mid * hmask_ref[h:h + 1, :]).astype(BF16)
            ks_s[r, :] = (kk * jnp.exp(rep(ends) - cum)).astype(BF16)
            for c, e in enumerate(ends):
                ci = i * per_sub + c
                alast_s[ci:ci + 1, :] = jnp.exp(e)

    def chunk_fast(c):
        rows = slice(c * FAST_CHUNK, (c + 1) * FAST_CHUNK)
        causal = causal_ref[...] != 0
        bd64f = bd64f_ref[...]
        o_parts = []
        for g, cs in enumerate(groups):
            scores = jnp.where(causal, _dot_nt(qh_s[rows, cs], bd_tile(kbd_s, rows, cs)), 0.0)
            st = st_hg[g]
            o_parts.append(_dot(scores.astype(BF16), bd_tile(vbd_s, rows, cs)) + _dot_nt(qi_s[rows, cs], st.astype(BF16)))
            st_hg[g] = st * alast_s[c:c + 1, cs] + _dot_tn(v_s[rows, cs], ks_s[rows, cs]) * bd64f
        o_hg_s[rows, :] = jnp.concatenate(o_parts, axis=1)

    def chunk_safe(c):
        rows = slice(c * CHUNK, (c + 1) * CHUNK)
        q = q_s[rows, :]
        kk = kk_s[rows, :]
        lf2 = jnp.concatenate([lfh_s[rows, :], lfl_s[rows, :]], axis=0)
        ex = jnp.exp(_dot(sel_ref[...], lf2))
        lvl = lvl_ref[...]
        bd64 = bd64_ref[...]
        bd64f = bd64f_ref[...]
        o_parts = []
        for g, cs in enumerate(groups):
            scores = jnp.zeros((CHUNK, MXU_W), F32)
            for l in range(N_LEVELS + 1):
                if l < N_LEVELS:
                    e = ex[CHUNK * (l + 2):CHUNK * (l + 3), cs]
                    ql = (q[:, cs] * e).astype(BF16)
                    kl = (kk[:, cs] * e).astype(BF16)
                else:
                    ql = q[:, cs].astype(BF16)
                    kl = kk[:, cs].astype(BF16)
                rhs = jnp.concatenate([kl] * HG_GROUP, axis=0) * bd64
                scores = jnp.where(lvl == l, _dot_nt(ql, rhs), scores)
            e_cum = ex[0:CHUNK, cs]
            e_suf = ex[CHUNK:2 * CHUNK, cs]
            st = st_hg[g]
            o_inter = _dot_nt((q[:, cs] * e_cum).astype(BF16), st.astype(BF16))
            o_parts.append(_dot(scores.astype(BF16), bd_tile(vbd_s, rows, cs)) + o_inter)
            upd = _dot_tn(v_s[rows, cs], (kk[:, cs] * e_suf).astype(BF16))
            st_hg[g] = st * e_cum[CHUNK - 1:CHUNK, :] + upd * bd64f
        o_hg_s[rows, :] = jnp.concatenate(o_parts, axis=1)

    def recurrences(fast):
        if fast:
            prepare_fast()
        for c in range(MIX_TILE // CHUNK):
            if not fast:
                chunk_safe(c)
            elif c % (FAST_CHUNK // CHUNK) == 0:
                chunk_fast(c * CHUNK // FAST_CHUNK)
            r = subs[c * CHUNK // MIX_SUB]
            for src, width, dst in GATE_PIECES[c % len(GATE_PIECES)]:
                gate_s[r, dst:dst + width] = _dot(xn_s[r, :], w_in_ref[:, src:src + width])

    pl.when(mild)(functools.partial(recurrences, True))
    pl.when(jnp.logical_not(mild))(functools.partial(recurrences, False))

    for r in subs:
        h_ref[r, :] = _merge_branches(
            x_ref[r, :], o_hg_s[r, :], o_ret_s[r, :],
            gate_s[r, 0:HG_W], gate_s[r, HG_W:HG_W + RET_W], gate_s[r, HG_W + RET_W:],
            g_hg_ref[...], g_ret_ref[...],
            w_br_hg_ref[...], w_br_ret_ref[...], w_out_ref[...])

    @pl.when(t_idx == pl.num_programs(1) - 1)
    def _():
        for g in range(n_hg_groups):
            st_t = st_hg[g].T
            for h in range(HG_GROUP):
                blk = slice(HG_FDIM * h, HG_FDIM * (h + 1))
                st_hg_out_ref[HG_GROUP * g + h] = st_t[blk, blk]
        for g in range(n_ret_groups):
            for h in range(RET_GROUP):
                blk = slice(RET_DK * h, RET_DK * (h + 1))
                st_ret_out_ref[RET_GROUP * g + h] = st_ret[g, blk, blk]


def _const_spec(shape):
    zeros = (0,) * len(shape)
    return pl.BlockSpec(shape, lambda *_: zeros, pipeline_mode=pl.Buffered(1))


def _mix_prompt(x, cos_full, sin_signed, w_in, g_mix, lb, consts, g_hg, g_ret,
                w_br_hg, w_br_ret, w_out):
    b, l, d = x.shape
    nt = l // MIX_TILE
    tile = lambda w: pl.BlockSpec((None, MIX_TILE, w), lambda i, j: (i, j, 0))
    table = lambda w: pl.BlockSpec((MIX_TILE, w), lambda i, j: (j, 0))
    n_hg_groups = HG_HEADS // HG_GROUP
    n_ret_groups = RET_HEADS // RET_GROUP
    const_args = (w_in, g_mix, lb) + tuple(consts) + (g_hg, g_ret, w_br_hg, w_br_ret, w_out)
    state_spec = lambda h, dk, dv: pl.BlockSpec((None, None, h, dk, dv), lambda i, j: (0, i, 0, 0, 0))
    return pl.pallas_call(
        _mix_prompt_kernel,
        grid=(b, nt),
        in_specs=[tile(d), table(RET_W), table(RET_W)] + [_const_spec(a.shape) for a in const_args],
        out_specs=[tile(d), state_spec(HG_HEADS, HG_FDIM, HG_IDIM), state_spec(RET_HEADS, RET_DK, RET_DV)],
        out_shape=[jax.ShapeDtypeStruct((b, l, d), F32),
                   jax.ShapeDtypeStruct((1, b, HG_HEADS, HG_FDIM, HG_IDIM), F32),
                   jax.ShapeDtypeStruct((1, b, RET_HEADS, RET_DK, RET_DV), F32)],
        scratch_shapes=[
            pltpu.VMEM((MIX_TILE, d), BF16),
            pltpu.VMEM((MIX_TILE, HG_W), F32),
            pltpu.VMEM((MIX_TILE, HG_W), F32),
            pltpu.VMEM((MIX_TILE, HG_W), BF16),
            pltpu.VMEM((MIX_TILE, HG_W), BF16),
            pltpu.VMEM((MIX_TILE, HG_W), BF16),
            pltpu.VMEM((HG_GROUP, MIX_TILE, HG_W), BF16),
            pltpu.VMEM((MIX_TILE, HG_W), BF16),
            pltpu.VMEM((MIX_TILE, HG_W), BF16),
            pltpu.VMEM((HG_GROUP, MIX_TILE, HG_W), BF16),
            pltpu.VMEM((MIX_TILE, HG_W), BF16),
            pltpu.VMEM((max(MIX_TILE // CHUNK, SUBLANES), HG_W), F32),
            pltpu.VMEM((MIX_TILE, HG_W), F32),
            pltpu.VMEM((MIX_TILE, RET_W), F32),
            pltpu.VMEM((MIX_TILE, HG_W + RET_W + 2 * D_MODEL), F32),
            pltpu.VMEM((n_hg_groups, MXU_W, MXU_W), F32),
            pltpu.VMEM((n_ret_groups, MXU_W, MXU_W), F32),
        ],
        compiler_params=pltpu.CompilerParams(
            dimension_semantics=("arbitrary", "arbitrary"), vmem_limit_bytes=V7X_VMEM_LIMIT),
        name="mix_prompt",
    )(x, cos_full, sin_signed, *const_args)


def _ffn_prompt_kernel(h_ref, p_ref, g_ffn_ref, w_up_ref, conv_w_ref, conv_b_ref, w_down_ref,
                       g_ple_ref, w_ple_gate_ref, w_ple_ref, g_final_ref,
                       y_ref, conv_out_ref, a_ext, gate_s):
    t_idx = pl.program_id(1)
    tl = FFN_TILE

    @pl.when(t_idx == 0)
    def _():
        a_ext[0:CARRY_ROWS, :] = jnp.zeros((CARRY_ROWS, D_FF), F32)

    @pl.when(t_idx > 0)
    def _():
        a_ext[0:CARRY_ROWS, :] = a_ext[tl:tl + CARRY_ROWS, :]

    subs = [slice(i * FFN_SUB, (i + 1) * FFN_SUB) for i in range(tl // FFN_SUB)]
    for r in subs:
        xn = _rms(h_ref[r, :], g_ffn_ref[...]).astype(BF16)
        a_ext[CARRY_ROWS + r.start:CARRY_ROWS + r.stop, :] = _dot(xn, w_up_ref[:, 0:D_FF])
        gate_s[r, :] = _dot(xn, w_up_ref[:, D_FF:2 * D_FF])
    cw = conv_w_ref[...]
    for r in subs:
        ac = conv_b_ref[...]
        for j in range(CONV_W):
            off = CARRY_ROWS - (CONV_W - 1) + j
            ac = ac + a_ext[off + r.start:off + r.stop, :] * cw[j:j + 1, :]
        y_ref[r, :] = _ffn_tail(h_ref[r, :], ac, gate_s[r, :], p_ref[r, :], w_down_ref[...], g_ple_ref[...],
                                w_ple_gate_ref[...], w_ple_ref[...], g_final_ref[...])

    @pl.when(t_idx == pl.num_programs(1) - 1)
    def _():
        conv_out_ref[...] = a_ext[CARRY_ROWS + tl - (CONV_W - 1):CARRY_ROWS + tl, :]


def _ffn_prompt(h, p, g_ffn, w_up, conv_w, conv_b, w_down, g_ple, w_ple_gate, w_ple, g_final):
    b, l, d = h.shape
    nt = l // FFN_TILE
    tile = lambda w: pl.BlockSpec((None, FFN_TILE, w), lambda i, j: (i, j, 0))
    const_args = (g_ffn, w_up, conv_w, conv_b, w_down, g_ple, w_ple_gate, w_ple, g_final)
    return pl.pallas_call(
        _ffn_prompt_kernel,
        grid=(b, nt),
        in_specs=[tile(d), tile(PLE_DIM)] + [_const_spec(a.shape) for a in const_args],
        out_specs=[tile(d), pl.BlockSpec((None, CONV_W - 1, D_FF), lambda i, j: (i, 0, 0))],
        out_shape=[jax.ShapeDtypeStruct((b, l, d), F32),
                   jax.ShapeDtypeStruct((b, CONV_W - 1, D_FF), F32)],
        scratch_shapes=[pltpu.VMEM((FFN_TILE + CARRY_ROWS, D_FF), F32), pltpu.VMEM((FFN_TILE, D_FF), F32)],
        compiler_params=pltpu.CompilerParams(
            dimension_semantics=("arbitrary", "arbitrary"), vmem_limit_bytes=V7X_VMEM_LIMIT),
        name="ffn_prompt",
    )(h, p, *const_args)


def _proj_sample_kernel(x_ref, cos_ref, sin_ref, w_in_ref, g_mix_ref, lb_ref,
                        f_ref, kk_ref, q_ref, v_ref, kr_ref, qr_ref, vr_ref, gates_ref):
    xn = _rms(x_ref[...], g_mix_ref[...]).astype(BF16)
    u = _dot(xn, w_in_ref[:, C_HQ:C_HG])
    q, f, kk = _hgrn_inputs(u[:, 0:HG_W], u[:, HG_W:2 * HG_W], lb_ref[...])
    f_ref[...] = f.T
    kk_ref[...] = kk.T
    q_ref[...] = q.T
    v_ref[...] = u[:, 2 * HG_W:3 * HG_W].T
    u = _dot(xn, w_in_ref[:, C_RQ:C_RG])
    cos_full = cos_ref[...]
    sin_signed = sin_ref[...]
    qr_ref[...] = _rotary(u[:, 0:RET_W], cos_full, sin_signed)
    kr_ref[...] = _rotary(u[:, RET_W:2 * RET_W], cos_full, sin_signed) * (RET_DK ** -0.5)
    vr_ref[...] = u[:, 2 * RET_W:3 * RET_W]
    gates_ref[:, 0:HG_W] = _dot(xn, w_in_ref[:, C_HG:C_RQ])
    gates_ref[:, HG_W:HG_W + RET_W] = _dot(xn, w_in_ref[:, C_RG:C_GA])
    gates_ref[:, HG_W + RET_W:] = _dot(xn, w_in_ref[:, C_GA:C_END])


def _proj_sample(x, cos_full, sin_signed, w_in, g_mix, lb):
    n = x.shape[0]
    col = jax.ShapeDtypeStruct((HG_W, n), F32)
    row = jax.ShapeDtypeStruct((n, RET_W), F32)
    return pl.pallas_call(
        _proj_sample_kernel,
        out_shape=[col] * 4 + [row] * 3 + [jax.ShapeDtypeStruct((n, HG_W + RET_W + 2 * D_MODEL), F32)],
        compiler_params=pltpu.CompilerParams(vmem_limit_bytes=V7X_VMEM_LIMIT),
        name="proj_sample",
    )(x, cos_full, sin_signed, w_in, g_mix, lb)


def _state_tokens_minor_kernel(s_ref, a_t_ref, k_t_ref, q_t_ref, v_t_ref, s_out_ref, o_t_ref):
    dk, dv, _ = s_ref.shape
    row0 = pl.program_id(0) * dk
    v_blk = v_t_ref[...]

    def body(k, acc):
        s_new = a_t_ref[pl.ds(row0 + k, 1), :] * s_ref[k] + k_t_ref[pl.ds(row0 + k, 1), :] * v_blk
        s_out_ref[k] = s_new
        return acc + q_t_ref[pl.ds(row0 + k, 1), :] * s_new

    o_t_ref[...] = lax.fori_loop(0, dk, body, jnp.zeros(v_blk.shape, F32), unroll=SUBLANES)


def _state_update_tokens_minor(state_t, layer, a_t, k_t, q_t, v_t, *, name):
    _, heads, dk, dv, n = state_t.shape
    full = lambda a: pl.BlockSpec(a.shape, lambda h: (0, 0))
    per_head = pl.BlockSpec((dv, n), lambda h: (h, 0))
    slab = lambda lead: pl.BlockSpec((None, None, dk, dv, n), lambda h: (lead, h, 0, 0, 0))
    return pl.pallas_call(
        _state_tokens_minor_kernel,
        grid=(heads,),
        in_specs=[slab(layer), full(a_t), full(k_t), full(q_t), per_head],
        out_specs=[slab(0), per_head],
        out_shape=[jax.ShapeDtypeStruct((1,) + state_t.shape[1:], F32), jax.ShapeDtypeStruct(v_t.shape, F32)],
        compiler_params=pltpu.CompilerParams(
            dimension_semantics=("arbitrary",), vmem_limit_bytes=V7X_VMEM_LIMIT),
        name=name,
    )(state_t, a_t, k_t, q_t, v_t)


def _state_kernel(s_ref, a_ref, k_ref, q_ref, v_ref, s_out_ref, o_ref, *, dk, dv, kb):
    j = pl.program_id(0)
    n = s_ref.shape[0]

    @pl.when(j == 0)
    def _():
        o_ref[...] = jnp.zeros_like(o_ref)

    group0 = j * (kb // SUBLANES)
    head_row = pl.multiple_of((j * kb // dk) * dv, dv)
    v_blk = v_ref[pl.ds(head_row, dv), :]
    acc = jnp.zeros(v_blk.shape, F32)
    for g in range(kb // SUBLANES):
        rows = slice(SUBLANES * g, SUBLANES * (g + 1))
        s_t = s_ref[:, rows, :].reshape(n * SUBLANES, dv).T
        s_new = a_ref[pl.ds(group0 + g, 1), :] * s_t + k_ref[pl.ds(group0 + g, 1), :] * v_blk
        s_out_ref[:, rows, :] = s_new.T.reshape(n, SUBLANES, dv)
        acc = acc + q_ref[pl.ds(group0 + g, 1), :] * s_new
    o_ref[pl.ds(head_row, dv), :] += acc


def _state_update(state, layer, a_il, k_il, q_il, v_rep, *, kb, name):
    _, n, heads, dk, dv = state.shape
    per_head = dk // kb
    full = lambda a: pl.BlockSpec(a.shape, lambda j: (0, 0))
    slab = lambda lead: pl.BlockSpec((None, n, None, kb, dv), lambda j: (lead, 0, j // per_head, j % per_head, 0))
    return pl.pallas_call(
        functools.partial(_state_kernel, dk=dk, dv=dv, kb=kb),
        grid=(heads * per_head,),
        in_specs=[slab(layer), full(a_il), full(k_il), full(q_il), full(v_rep)],
        out_specs=[slab(0), full(v_rep)],
        out_shape=[jax.ShapeDtypeStruct((1,) + state.shape[1:], F32), jax.ShapeDtypeStruct(v_rep.shape, F32)],
        compiler_params=pltpu.CompilerParams(
            dimension_semantics=("arbitrary",), vmem_limit_bytes=V7X_VMEM_LIMIT),
        name=name,
    )(state, a_il, k_il, q_il, v_rep)


def _lane_group_sum(o_ref, sel):
    o = o_ref[...]
    hi = o.astype(BF16)
    lo = (o - hi.astype(F32)).astype(BF16)
    return _dot_nt(sel, hi) + _dot_nt(sel, lo)


def _tail_sample_kernel(x_ref, o_hg_t_ref, o_ret_ref, sel_ref, gates_ref, conv0_ref, p_ref,
                        g_hg_ref, g_ret_ref, w_br_hg_ref, w_br_ret_ref, w_out_ref,
                        g_ffn_ref, w_up_ref, conv_w_ref, conv_b_ref, w_down_ref,
                        g_ple_ref, w_ple_gate_ref, w_ple_ref, g_final_ref,
                        y_ref, conv_out_ref):
    h = _merge_branches(
        x_ref[...], o_hg_t_ref[...].T, _lane_group_sum(o_ret_ref, sel_ref[...]),
        gates_ref[:, 0:HG_W], gates_ref[:, HG_W:HG_W + RET_W], gates_ref[:, HG_W + RET_W:],
        g_hg_ref[...], g_ret_ref[...],
        w_br_hg_ref[...], w_br_ret_ref[...], w_out_ref[...])
    xn = _rms(h, g_ffn_ref[...]).astype(BF16)
    a = _dot(xn, w_up_ref[:, 0:D_FF])
    gate_half = _dot(xn, w_up_ref[:, D_FF:2 * D_FF])
    cw = conv_w_ref[...]
    prev = conv0_ref[:, 1, :]
    ac = conv_b_ref[...] + conv0_ref[:, 0, :] * cw[0:1, :] + prev * cw[1:2, :] + a * cw[2:3, :]
    conv_out_ref[0, :, 0, :] = prev
    conv_out_ref[0, :, 1, :] = a
    y_ref[...] = _ffn_tail(h, ac, gate_half, p_ref[...], w_down_ref[...], g_ple_ref[...],
                           w_ple_gate_ref[...], w_ple_ref[...], g_final_ref[...])


def _tail_sample(x, o_hg_t, o_ret, sel, gates, conv0, p, *weights):
    n = x.shape[0]
    return pl.pallas_call(
        _tail_sample_kernel,
        out_shape=[jax.ShapeDtypeStruct((n, D_MODEL), F32),
                   jax.ShapeDtypeStruct((1, n, CONV_W - 1, D_FF), F32)],
        compiler_params=pltpu.CompilerParams(vmem_limit_bytes=V7X_VMEM_LIMIT),
        name="tail_sample",
    )(x, o_hg_t, o_ret, sel, gates, conv0, p, *weights)


def kernel(x_prompt, x_sample, state_hgrn, state_ret, state_conv, p_prompt, p_sample, lb_logits, w_in, g_mix, g_hg_out, g_ret_out, w_br_hg, w_br_ret, w_out, g_ffn, w_up, conv_w, conv_b, w_down, g_ple, w_ple, w_ple_gate, g_final):
    assert DEPTH == 1 and x_prompt.shape == (BATCH, SEQ, D_MODEL) and x_sample.shape == (DEC_BATCH, DEC_SEQ, D_MODEL)
    i = 0
    row = lambda a: a.reshape(1, -1).astype(F32)
    lb = jnp.cumsum(jax.nn.softmax(lb_logits.astype(F32), axis=0), axis=0)[i].reshape(1, HG_W)

    sel2, level = _level_tables()
    tri = np.tril(_block_mask(MIX_SUB, MIX_SUB, FAST_CHUNK, FAST_CHUNK))
    causal = np.tile(np.tril(np.ones((FAST_CHUNK, FAST_CHUNK), np.int32)), (1, HG_GROUP))
    hmask = (np.arange(HG_W)[None, :] // HG_FDIM) % HG_GROUP == np.arange(HG_GROUP)[:, None]
    consts = (
        jnp.asarray(sel2, BF16),
        jnp.asarray(np.concatenate([tri, tri], axis=1), BF16),
        jnp.asarray(hmask, F32),
        jnp.asarray(causal),
        jnp.asarray(np.tile(level, (1, HG_GROUP))),
        jnp.asarray(_block_mask(MXU_W, MXU_W, CHUNK, HG_FDIM), BF16),
        jnp.asarray(_block_mask(MXU_W, MXU_W, HG_IDIM, HG_FDIM), F32),
        jnp.asarray(_block_mask(MXU_W, MXU_W, RET_DK, RET_DV), F32),
    ) + _retention_tables()

    w_in_b = w_in[i].astype(BF16)
    mix_w = (row(g_hg_out[i]), row(g_ret_out[i]),
             w_br_hg[i].astype(BF16), w_br_ret[i].astype(BF16), w_out[i].astype(BF16))
    ffn_w = (row(g_ffn[i]), w_up[i].astype(BF16), conv_w[i].astype(F32), row(conv_b[i]), w_down[i].astype(BF16),
             row(g_ple[i]), w_ple_gate[i].astype(BF16), w_ple[i].astype(BF16), row(g_final))

    cos_p, sin_p = _rope_tables(0.0, SEQ)
    h_p, hg_state_p, ret_state_p = _mix_prompt(x_prompt, cos_p, sin_p, w_in_b, row(g_mix[i]), lb, consts, *mix_w)
    y_prompt, conv_p = _ffn_prompt(h_p, p_prompt[i], *ffn_w)

    n = DEC_BATCH
    xs = x_sample.reshape(n, D_MODEL)
    cos_s, sin_s = _rope_tables(float(PAST_LEN), DEC_SEQ)
    f_t, kk_t, q_t, v_t, kr, qr, vr, gates = _proj_sample(xs, cos_s, sin_s, w_in_b, row(g_mix[i]), lb)
    hg_new_t, o_hg_t = _state_update_tokens_minor(
        jnp.transpose(state_hgrn, (0, 2, 3, 4, 1)), i, f_t, kk_t, q_t, v_t, name="state_hgrn")
    hg_new = jnp.transpose(hg_new_t, (0, 4, 1, 2, 3))
    interleave = lambda a: a.reshape(n, -1, SUBLANES).transpose(1, 0, 2).reshape(-1, n * SUBLANES)
    repeat = lambda a: jnp.repeat(a.T, SUBLANES, axis=1)
    gamma = jnp.asarray(np.broadcast_to(np.repeat(np.exp(_log_gamma()), RET_DK // SUBLANES)[:, None],
                                        (RET_W // SUBLANES, n * SUBLANES)))
    ret_new, o_ret = _state_update(
        state_ret, i, gamma, interleave(kr), interleave(qr), repeat(vr), kb=RET_DK // 2, name="state_ret")
    sel = jnp.asarray(_block_mask(n, n * SUBLANES, 1, SUBLANES), BF16)
    y_sample, conv_s = _tail_sample(
        xs, o_hg_t, o_ret, sel, gates, state_conv[i], p_sample[i].reshape(n, PLE_DIM),
        *mix_w, *ffn_w)

    return (y_prompt, y_sample.reshape(n, DEC_SEQ, D_MODEL),
            hg_state_p, ret_state_p, conv_p[None],
            hg_new, ret_new,
            conv_s)
```

```python
import functools

import numpy as np
import jax
import jax.numpy as jnp
from jax import lax
from jax.experimental import pallas as pl
from jax.experimental.pallas import tpu as pltpu

D_MODEL = 1024
BATCH = 8
SEQ = 2048
DEPTH = 1
DEC_BATCH = 128
DEC_SEQ = 1
PAST_LEN = 16384
HG_HEADS = 8
HG_FDIM = 64
HG_IDIM = 64
HG_W = HG_HEADS * HG_FDIM
RET_HEADS = 4
RET_DK = 128
RET_DV = 128
RET_W = RET_HEADS * RET_DK
D_FF = 2816
CONV_W = 3
PLE_DIM = 256
CHUNK = 64
ROPE_BASE = 10000.0
EPS = 1e-6

C_HQ, C_HF, C_HI, C_HG = 0, 512, 1024, 1536
C_RQ, C_RK, C_RV, C_RG = 2048, 2560, 3072, 3584
C_GA, C_GB, C_END = 4096, 5120, 6144

N_LEVELS = 6
FAST_CHUNK = 128
FAST_DECAY_LIMIT = 80.0
HG_GROUP = 4
RET_GROUP = 2
MXU_W = 256
SUBLANES = 8
LANES = 128

MIX_TILE = 256
MIX_SUB = 256
FFN_TILE = 512
FFN_SUB = 256
CARRY_ROWS = 8
V7X_VMEM_LIMIT = 56 * 1024 * 1024

F32 = jnp.float32
BF16 = jnp.bfloat16


def _level_tables():
    c = CHUNK
    t = np.arange(c)[:, None]
    r = np.arange(c)[None, :]
    blocks = [r <= t, r > t]
    for l in range(N_LEVELS):
        same = (t >> l) == (r >> l)
        upper = ((t >> l) & 1) == 1
        blocks.append(np.where(upper, same & (r <= t), same & (r > t)))
    sel = np.concatenate(blocks, axis=0).astype(np.float32)
    sel2 = np.concatenate([sel, sel], axis=1)
    s = r
    level = np.full((c, c), N_LEVELS + 1, np.int32)
    level[np.broadcast_to(s == t, (c, c))] = N_LEVELS
    x = t ^ s
    for l in range(N_LEVELS):
        hit = (s < t) & ((x >> l) == 1)
        level[hit] = l
    return sel2, level


def _block_mask(rows, cols, rblk, cblk):
    r = np.arange(rows)[:, None] // rblk
    c = np.arange(cols)[None, :] // cblk
    return (r == c).astype(np.float32)


def _rope_tables(pos0, length):
    d = RET_DK
    inv_freq = ROPE_BASE ** (-np.arange(0, d, 2, dtype=np.float64) / d)
    pos = pos0 + np.arange(length, dtype=np.float64)
    ang = pos[:, None] * inv_freq[None, :]
    cos, sin = np.cos(ang), np.sin(ang)
    cos_full = np.tile(np.concatenate([cos, cos], axis=-1), (1, RET_HEADS))
    sin_signed = np.tile(np.concatenate([-sin, sin], axis=-1), (1, RET_HEADS))
    return jnp.asarray(cos_full, F32), jnp.asarray(sin_signed, F32)


def _log_gamma():
    return np.log1p(-np.exp2(-5.0 - np.arange(RET_HEADS, dtype=np.float32))).astype(np.float32)


def _retention_tables():
    c = MIX_SUB
    log_gamma = _log_gamma()
    t = np.arange(c, dtype=np.float32)
    delta = t[:, None] - t[None, :]
    dmask = np.where(delta >= 0, np.exp(np.maximum(delta, 0.0) * log_gamma[:, None, None]), 0.0)
    q_decay = np.exp((t + 1.0)[None, :] * log_gamma[:, None])
    k_decay = np.exp((c - 1.0 - t)[None, :] * log_gamma[:, None])
    qdec = np.repeat(q_decay.T, RET_DK, axis=1)
    kdec = np.repeat(k_decay.T, RET_DK, axis=1)
    chunk_decay = np.exp(np.float32(c) * log_gamma)
    cd = np.repeat(chunk_decay, RET_DK).reshape(RET_HEADS // RET_GROUP, MXU_W, 1)
    cd = np.broadcast_to(cd, (RET_HEADS // RET_GROUP, MXU_W, MXU_W))
    return tuple(jnp.asarray(a, F32) for a in (dmask, qdec, kdec, cd))


def _rms(x, g):
    ms = jnp.mean(x * x, axis=-1, keepdims=True)
    return x * lax.rsqrt(ms + EPS) * g


def _silu(x):
    return x * jax.nn.sigmoid(x)


def _dot(a, b):
    return jnp.dot(a, b, preferred_element_type=F32)


def _dot_nt(a, b):
    return lax.dot_general(a, b, (((1,), (1,)), ((), ())), preferred_element_type=F32)


def _dot_tn(a, b):
    return lax.dot_general(a, b, (((0,), (0,)), ((), ())), preferred_element_type=F32)


def _rotary(x, cos_full, sin_signed):
    swapped = jnp.concatenate(
        [pltpu.roll(x[:, RET_DK * h:RET_DK * (h + 1)], RET_DK // 2, 1) for h in range(RET_HEADS)], axis=1)
    return x * cos_full + swapped * sin_signed


def _hgrn_inputs(hq, hf, lb):
    q = _silu(hq)
    f = lb + (1.0 - lb) * jax.nn.sigmoid(hf)
    return q, f, 1.0 - f


def _head_mean_square(o, head_w):
    parts = []
    for g in range(o.shape[1] // LANES):
        sq = o[:, LANES * g:LANES * (g + 1)]
        sq = sq * sq
        if head_w == LANES:
            ms = jnp.broadcast_to(jnp.sum(sq, axis=-1, keepdims=True), sq.shape)
        else:
            low = lax.broadcasted_iota(jnp.int32, sq.shape, 1) < head_w
            lo = jnp.sum(jnp.where(low, sq, 0.0), axis=-1, keepdims=True)
            hi = jnp.sum(jnp.where(low, 0.0, sq), axis=-1, keepdims=True)
            ms = jnp.where(low, lo, hi)
        parts.append(ms * (1.0 / head_w))
    return jnp.concatenate(parts, axis=1)


def _merge_branches(x, o_hg, o_ret, u_hg, u_rg, u_gab, g_hg, g_ret, w_br_hg, w_br_ret, w_out):
    a_hg = o_hg * lax.rsqrt(_head_mean_square(o_hg, HG_IDIM) + EPS) * g_hg * _silu(u_hg)
    y_hg = _dot(a_hg.astype(BF16), w_br_hg)
    a_ret = o_ret * lax.rsqrt(_head_mean_square(o_ret, RET_DV) + EPS) * g_ret * _silu(u_rg)
    y_ret = _dot(a_ret.astype(BF16), w_br_ret)
    mixed = jax.nn.sigmoid(u_gab[:, :D_MODEL]) * y_hg + jax.nn.sigmoid(u_gab[:, D_MODEL:]) * y_ret
    return x + _dot(mixed.astype(BF16), w_out)


def _ffn_tail(h, ac, gate_half, p, w_down, g_ple, w_ple_gate, w_ple, g_final):
    h = h + _dot((jax.nn.gelu(ac) * gate_half).astype(BF16), w_down)
    gate = jax.nn.sigmoid(_dot(_rms(h, g_ple).astype(BF16), w_ple_gate))
    h = h + gate * _dot(p.astype(BF16), w_ple)
    return _rms(h, g_final)


def _mix_prompt_kernel(x_ref, cos_ref, sin_ref, w_in_ref, g_mix_ref, lb_ref, sel_ref, tri_ref, hmask_ref, causal_ref, lvl_ref,
                       bd64_ref, bd64f_ref, bd128f_ref, dmask_ref, qdec_ref, kdec_ref, cd_ref,
                       g_hg_ref, g_ret_ref, w_br_hg_ref, w_br_ret_ref, w_out_ref,
                       h_ref, st_hg_out_ref, st_ret_out_ref,
                       xn_s, q_s, kk_s, lfh_s, lfl_s, v_s, vbd_s, qh_s, qi_s, kbd_s, ks_s, alast_s, o_hg_s, o_ret_s, st_hg, st_ret):
    t_idx = pl.program_id(1)
    n_hg_groups = HG_HEADS // HG_GROUP
    n_ret_groups = RET_HEADS // RET_GROUP

    @pl.when(t_idx == 0)
    def _():
        st_hg[...] = jnp.zeros_like(st_hg)
        st_ret[...] = jnp.zeros_like(st_ret)

    subs = [slice(i * MIX_SUB, (i + 1) * MIX_SUB) for i in range(MIX_TILE // MIX_SUB)]
    chunk_sums = []
    for r in subs:
        xn = _rms(x_ref[r, :], g_mix_ref[...]).astype(BF16)
        xn_s[r, :] = xn

        u = _dot(xn, w_in_ref[:, C_HQ:C_HG])
        q, f, kk = _hgrn_inputs(u[:, 0:HG_W], u[:, HG_W:2 * HG_W], lb_ref[...])
        q_s[r, :] = q
        kk_s[r, :] = kk
        lf = jnp.log(f)
        lf_hi = lf.astype(BF16)
        lfh_s[r, :] = lf_hi
        lfl_s[r, :] = (lf - lf_hi.astype(F32)).astype(BF16)
        v = u[:, 2 * HG_W:3 * HG_W]
        v_s[r, :] = v.astype(BF16)
        for h in range(HG_GROUP):
            vbd_s[h, r, :] = (v * hmask_ref[h:h + 1, :]).astype(BF16)
        chunk_sums += [jnp.sum(lf[c * CHUNK:(c + 1) * CHUNK, :], axis=0, keepdims=True)
                       for c in range(MIX_SUB // CHUNK)]

        u = _dot(xn, w_in_ref[:, C_RQ:C_RG])
        cos_full = cos_ref[r, :]
        sin_signed = sin_ref[r, :]
        qr = _rotary(u[:, 0:RET_W], cos_full, sin_signed)
        kr = _rotary(u[:, RET_W:2 * RET_W], cos_full, sin_signed) * (RET_DK ** -0.5)
        vr = u[:, 2 * RET_W:3 * RET_W].astype(BF16)
        qrb = qr.astype(BF16)
        krb = kr.astype(BF16)
        o_heads = []
        for h in range(RET_HEADS):
            hs = slice(RET_DK * h, RET_DK * (h + 1))
            scores = _dot_nt(qrb[:, hs], krb[:, hs]) * dmask_ref[h]
            o_heads.append(_dot(scores.astype(BF16), vr[:, hs]))
        qd = (qr * qdec_ref[...]).astype(BF16)
        kd = (kr * kdec_ref[...]).astype(BF16)
        bd128f = bd128f_ref[...]
        o_inter = []
        for g in range(n_ret_groups):
            cs = slice(MXU_W * g, MXU_W * (g + 1))
            st = st_ret[g]
            o_inter.append(_dot(qd[:, cs], st.astype(BF16)))
            st_ret[g] = cd_ref[g] * st + _dot_tn(kd[:, cs], vr[:, cs]) * bd128f
        o_ret_s[r, :] = jnp.concatenate(o_heads, axis=1) + jnp.concatenate(o_inter, axis=1)

    mild = jnp.min(functools.reduce(jnp.minimum, chunk_sums)) >= -FAST_DECAY_LIMIT

    groups = [slice(MXU_W * g, MXU_W * (g + 1)) for g in range(n_hg_groups)]
    bd_tile = lambda ref, rows, cs: jnp.concatenate([ref[h, rows, cs] for h in range(HG_GROUP)], axis=0)

    def prepare_fast():
        per_sub = MIX_SUB // FAST_CHUNK
        rep = lambda rows: jnp.concatenate([jnp.broadcast_to(e, (FAST_CHUNK, HG_W)) for e in rows], axis=0)
        for i, r in enumerate(subs):
            lf2 = jnp.concatenate([lfh_s[r, :], lfl_s[r, :]], axis=0)
            cum = _dot(tri_ref[...], lf2)
            mids = [cum[c * FAST_CHUNK + FAST_CHUNK // 2 - 1:c * FAST_CHUNK + FAST_CHUNK // 2, :] for c in range(per_sub)]
            ends = [cum[(c + 1) * FAST_CHUNK - 1:(c + 1) * FAST_CHUNK, :] for c in range(per_sub)]
            mid = rep(mids)
            kk = kk_s[r, :]
            q = q_s[r, :]
            qh_s[r, :] = (q * jnp.exp(cum - mid)).astype(BF16)
            qi_s[r, :] = (q * jnp.exp(cum)).astype(BF16)
            k_mid = kk * jnp.exp(mid - cum)
            for h in range(HG_GROUP):
                kbd_s[h, r, :] = (k_mid * hmask_ref[h:h + 1, :]).astype(BF16)
            ks_s[r, :] = (kk * jnp.exp(rep(ends) - cum)).astype(BF16)
            for c, e in enumerate(ends):
                ci = i * per_sub + c
                alast_s[ci:ci + 1, :] = jnp.exp(e)

    def chunk_fast(c):
        rows = slice(c * FAST_CHUNK, (c + 1) * FAST_CHUNK)
        causal = causal_ref[...] != 0
        bd64f = bd64f_ref[...]
        o_parts = []
        for g, cs in enumerate(groups):
            scores = jnp.where(causal, _dot_nt(qh_s[rows, cs], bd_tile(kbd_s, rows, cs)), 0.0)
            st = st_hg[g]
            o_parts.append(_dot(scores.astype(BF16), bd_tile(vbd_s, rows, cs)) + _dot_nt(qi_s[rows, cs], st.astype(BF16)))
            st_hg[g] = st * alast_s[c:c + 1, cs] + _dot_tn(v_s[rows, cs], ks_s[rows, cs]) * bd64f
        o_hg_s[rows, :] = jnp.concatenate(o_parts, axis=1)

    def chunk_safe(c):
        rows = slice(c * CHUNK, (c + 1) * CHUNK)
        q = q_s[rows, :]
        kk = kk_s[rows, :]
        lf2 = jnp.concatenate([lfh_s[rows, :], lfl_s[rows, :]], axis=0)
        ex = jnp.exp(_dot(sel_ref[...], lf2))
        lvl = lvl_ref[...]
        bd64 = bd64_ref[...]
        bd64f = bd64f_ref[...]
        o_parts = []
        for g, cs in enumerate(groups):
            scores = jnp.zeros((CHUNK, MXU_W), F32)
            for l in range(N_LEVELS + 1):
                if l < N_LEVELS:
                    e = ex[CHUNK * (l + 2):CHUNK * (l + 3), cs]
                    ql = (q[:, cs] * e).astype(BF16)
                    kl = (kk[:, cs] * e).astype(BF16)
                else:
                    ql = q[:, cs].astype(BF16)
                    kl = kk[:, cs].astype(BF16)
                rhs = jnp.concatenate([kl] * HG_GROUP, axis=0) * bd64
                scores = jnp.where(lvl == l, _dot_nt(ql, rhs), scores)
            e_cum = ex[0:CHUNK, cs]
            e_suf = ex[CHUNK:2 * CHUNK, cs]
            st = st_hg[g]
            o_inter = _dot_nt((q[:, cs] * e_cum).astype(BF16), st.astype(BF16))
            o_parts.append(_dot(scores.astype(BF16), bd_tile(vbd_s, rows, cs)) + o_inter)
            upd = _dot_tn(v_s[rows, cs], (kk[:, cs] * e_suf).astype(BF16))
            st_hg[g] = st * e_cum[CHUNK - 1:CHUNK, :] + upd * bd64f
        o_hg_s[rows, :] = jnp.concatenate(o_parts, axis=1)

    def recurrences(fast):
        if fast:
            prepare_fast()
            for c in range(MIX_TILE // FAST_CHUNK):
                chunk_fast(c)
        else:
            for c in range(MIX_TILE // CHUNK):
                chunk_safe(c)

    pl.when(mild)(functools.partial(recurrences, True))
    pl.when(jnp.logical_not(mild))(functools.partial(recurrences, False))

    for r in subs:
        xn = xn_s[r, :]
        h_ref[r, :] = _merge_branches(
            x_ref[r, :], o_hg_s[r, :], o_ret_s[r, :],
            _dot(xn, w_in_ref[:, C_HG:C_RQ]), _dot(xn, w_in_ref[:, C_RG:C_GA]), _dot(xn, w_in_ref[:, C_GA:C_END]),
            g_hg_ref[...], g_ret_ref[...],
            w_br_hg_ref[...], w_br_ret_ref[...], w_out_ref[...])

    @pl.when(t_idx == pl.num_programs(1) - 1)
    def _():
        for g in range(n_hg_groups):
            st_t = st_hg[g].T
            for h in range(HG_GROUP):
                blk = slice(HG_FDIM * h, HG_FDIM * (h + 1))
                st_hg_out_ref[HG_GROUP * g + h] = st_t[blk, blk]
        for g in range(n_ret_groups):
            for h in range(RET_GROUP):
                blk = slice(RET_DK * h, RET_DK * (h + 1))
                st_ret_out_ref[RET_GROUP * g + h] = st_ret[g, blk, blk]


def _const_spec(shape):
    zeros = (0,) * len(shape)
    return pl.BlockSpec(shape, lambda *_: zeros, pipeline_mode=pl.Buffered(1))


def _mix_prompt(x, cos_full, sin_signed, w_in, g_mix, lb, consts, g_hg, g_ret,
                w_br_hg, w_br_ret, w_out):
    b, l, d = x.shape
    nt = l // MIX_TILE
    tile = lambda w: pl.BlockSpec((None, MIX_TILE, w), lambda i, j: (i, j, 0))
    table = lambda w: pl.BlockSpec((MIX_TILE, w), lambda i, j: (j, 0))
    n_hg_groups = HG_HEADS // HG_GROUP
    n_ret_groups = RET_HEADS // RET_GROUP
    const_args = (w_in, g_mix, lb) + tuple(consts) + (g_hg, g_ret, w_br_hg, w_br_ret, w_out)
    state_spec = lambda h, dk, dv: pl.BlockSpec((None, None, h, dk, dv), lambda i, j: (0, i, 0, 0, 0))
    return pl.pallas_call(
        _mix_prompt_kernel,
        grid=(b, nt),
        in_specs=[tile(d), table(RET_W), table(RET_W)] + [_const_spec(a.shape) for a in const_args],
        out_specs=[tile(d), state_spec(HG_HEADS, HG_FDIM, HG_IDIM), state_spec(RET_HEADS, RET_DK, RET_DV)],
        out_shape=[jax.ShapeDtypeStruct((b, l, d), F32),
                   jax.ShapeDtypeStruct((1, b, HG_HEADS, HG_FDIM, HG_IDIM), F32),
                   jax.ShapeDtypeStruct((1, b, RET_HEADS, RET_DK, RET_DV), F32)],
        scratch_shapes=[
            pltpu.VMEM((MIX_TILE, d), BF16),
            pltpu.VMEM((MIX_TILE, HG_W), F32),
            pltpu.VMEM((MIX_TILE, HG_W), F32),
            pltpu.VMEM((MIX_TILE, HG_W), BF16),
            pltpu.VMEM((MIX_TILE, HG_W), BF16),
            pltpu.VMEM((MIX_TILE, HG_W), BF16),
            pltpu.VMEM((HG_GROUP, MIX_TILE, HG_W), BF16),
            pltpu.VMEM((MIX_TILE, HG_W), BF16),
            pltpu.VMEM((MIX_TILE, HG_W), BF16),
            pltpu.VMEM((HG_GROUP, MIX_TILE, HG_W), BF16),
            pltpu.VMEM((MIX_TILE, HG_W), BF16),
            pltpu.VMEM((max(MIX_TILE // CHUNK, SUBLANES), HG_W), F32),
            pltpu.VMEM((MIX_TILE, HG_W), F32),
            pltpu.VMEM((MIX_TILE, RET_W), F32),
            pltpu.VMEM((n_hg_groups, MXU_W, MXU_W), F32),
            pltpu.VMEM((n_ret_groups, MXU_W, MXU_W), F32),
        ],
        compiler_params=pltpu.CompilerParams(
            dimension_semantics=("arbitrary", "arbitrary"), vmem_limit_bytes=V7X_VMEM_LIMIT),
        name="mix_prompt",
    )(x, cos_full, sin_signed, *const_args)


def _ffn_prompt_kernel(h_ref, p_ref, g_ffn_ref, w_up_ref, conv_w_ref, conv_b_ref, w_down_ref,
                       g_ple_ref, w_ple_gate_ref, w_ple_ref, g_final_ref,
                       y_ref, conv_out_ref, a_ext, gate_s):
    t_idx = pl.program_id(1)
    tl = FFN_TILE

    @pl.when(t_idx == 0)
    def _():
        a_ext[0:CARRY_ROWS, :] = jnp.zeros((CARRY_ROWS, D_FF), F32)

    @pl.when(t_idx > 0)
    def _():
        a_ext[0:CARRY_ROWS, :] = a_ext[tl:tl + CARRY_ROWS, :]

    subs = [slice(i * FFN_SUB, (i + 1) * FFN_SUB) for i in range(tl // FFN_SUB)]
    for r in subs:
        xn = _rms(h_ref[r, :], g_ffn_ref[...]).astype(BF16)
        a_ext[CARRY_ROWS + r.start:CARRY_ROWS + r.stop, :] = _dot(xn, w_up_ref[:, 0:D_FF])
        gate_s[r, :] = _dot(xn, w_up_ref[:, D_FF:2 * D_FF])
    cw = conv_w_ref[...]
    for r in subs:
        ac = conv_b_ref[...]
        for j in range(CONV_W):
            off = CARRY_ROWS - (CONV_W - 1) + j
            ac = ac + a_ext[off + r.start:off + r.stop, :] * cw[j:j + 1, :]
        y_ref[r, :] = _ffn_tail(h_ref[r, :], ac, gate_s[r, :], p_ref[r, :], w_down_ref[...], g_ple_ref[...],
                                w_ple_gate_ref[...], w_ple_ref[...], g_final_ref[...])

    @pl.when(t_idx == pl.num_programs(1) - 1)
    def _():
        conv_out_ref[...] = a_ext[CARRY_ROWS + tl - (CONV_W - 1):CARRY_ROWS + tl, :]


def _ffn_prompt(h, p, g_ffn, w_up, conv_w, conv_b, w_down, g_ple, w_ple_gate, w_ple, g_final):
    b, l, d = h.shape
    nt = l // FFN_TILE
    tile = lambda w: pl.BlockSpec((None, FFN_TILE, w), lambda i, j: (i, j, 0))
    const_args = (g_ffn, w_up, conv_w, conv_b, w_down, g_ple, w_ple_gate, w_ple, g_final)
    return pl.pallas_call(
        _ffn_prompt_kernel,
        grid=(b, nt),
        in_specs=[tile(d), tile(PLE_DIM)] + [_const_spec(a.shape) for a in const_args],
        out_specs=[tile(d), pl.BlockSpec((None, CONV_W - 1, D_FF), lambda i, j: (i, 0, 0))],
        out_shape=[jax.ShapeDtypeStruct((b, l, d), F32),
                   jax.ShapeDtypeStruct((b, CONV_W - 1, D_FF), F32)],
        scratch_shapes=[pltpu.VMEM((FFN_TILE + CARRY_ROWS, D_FF), F32), pltpu.VMEM((FFN_TILE, D_FF), F32)],
        compiler_params=pltpu.CompilerParams(
            dimension_semantics=("arbitrary", "arbitrary"), vmem_limit_bytes=V7X_VMEM_LIMIT),
        name="ffn_prompt",
    )(h, p, *const_args)


def _proj_sample_kernel(x_ref, cos_ref, sin_ref, w_in_ref, g_mix_ref, lb_ref,
                        f_ref, kk_ref, q_ref, v_ref, kr_ref, qr_ref, vr_ref, gates_ref):
    xn = _rms(x_ref[...], g_mix_ref[...]).astype(BF16)
    u = _dot(xn, w_in_ref[:, C_HQ:C_HG])
    q, f, kk = _hgrn_inputs(u[:, 0:HG_W], u[:, HG_W:2 * HG_W], lb_ref[...])
    f_ref[...] = f.T
    kk_ref[...] = kk.T
    q_ref[...] = q.T
    v_ref[...] = u[:, 2 * HG_W:3 * HG_W].T
    u = _dot(xn, w_in_ref[:, C_RQ:C_RG])
    cos_full = cos_ref[...]
    sin_signed = sin_ref[...]
    qr_ref[...] = _rotary(u[:, 0:RET_W], cos_full, sin_signed)
    kr_ref[...] = _rotary(u[:, RET_W:2 * RET_W], cos_full, sin_signed) * (RET_DK ** -0.5)
    vr_ref[...] = u[:, 2 * RET_W:3 * RET_W]
    gates_ref[:, 0:HG_W] = _dot(xn, w_in_ref[:, C_HG:C_RQ])
    gates_ref[:, HG_W:HG_W + RET_W] = _dot(xn, w_in_ref[:, C_RG:C_GA])
    gates_ref[:, HG_W + RET_W:] = _dot(xn, w_in_ref[:, C_GA:C_END])


def _proj_sample(x, cos_full, sin_signed, w_in, g_mix, lb):
    n = x.shape[0]
    col = jax.ShapeDtypeStruct((HG_W, n), F32)
    row = jax.ShapeDtypeStruct((n, RET_W), F32)
    return pl.pallas_call(
        _proj_sample_kernel,
        out_shape=[col] * 4 + [row] * 3 + [jax.ShapeDtypeStruct((n, HG_W + RET_W + 2 * D_MODEL), F32)],
        compiler_params=pltpu.CompilerParams(vmem_limit_bytes=V7X_VMEM_LIMIT),
        name="proj_sample",
    )(x, cos_full, sin_signed, w_in, g_mix, lb)


def _state_tokens_minor_kernel(s_ref, a_t_ref, k_t_ref, q_t_ref, v_t_ref, s_out_ref, o_t_ref):
    dk, dv, _ = s_ref.shape
    row0 = pl.program_id(0) * dk
    v_blk = v_t_ref[...]

    def body(k, acc):
        s_new = a_t_ref[pl.ds(row0 + k, 1), :] * s_ref[k] + k_t_ref[pl.ds(row0 + k, 1), :] * v_blk
        s_out_ref[k] = s_new
        return acc + q_t_ref[pl.ds(row0 + k, 1), :] * s_new

    o_t_ref[...] = lax.fori_loop(0, dk, body, jnp.zeros(v_blk.shape, F32), unroll=SUBLANES)


def _state_update_tokens_minor(state_t, layer, a_t, k_t, q_t, v_t, *, name):
    _, heads, dk, dv, n = state_t.shape
    full = lambda a: pl.BlockSpec(a.shape, lambda h: (0, 0))
    per_head = pl.BlockSpec((dv, n), lambda h: (h, 0))
    slab = lambda lead: pl.BlockSpec((None, None, dk, dv, n), lambda h: (lead, h, 0, 0, 0))
    return pl.pallas_call(
        _state_tokens_minor_kernel,
        grid=(heads,),
        in_specs=[slab(layer), full(a_t), full(k_t), full(q_t), per_head],
        out_specs=[slab(0), per_head],
        out_shape=[jax.ShapeDtypeStruct((1,) + state_t.shape[1:], F32), jax.ShapeDtypeStruct(v_t.shape, F32)],
        compiler_params=pltpu.CompilerParams(
            dimension_semantics=("arbitrary",), vmem_limit_bytes=V7X_VMEM_LIMIT),
        name=name,
    )(state_t, a_t, k_t, q_t, v_t)


def _state_kernel(s_ref, a_ref, k_ref, q_ref, v_ref, s_out_ref, o_ref, *, dk, dv, kb):
    j = pl.program_id(0)
    n = s_ref.shape[0]

    @pl.when(j == 0)
    def _():
        o_ref[...] = jnp.zeros_like(o_ref)

    group0 = j * (kb // SUBLANES)
    head_row = pl.multiple_of((j * kb // dk) * dv, dv)
    v_blk = v_ref[pl.ds(head_row, dv), :]
    acc = jnp.zeros(v_blk.shape, F32)
    for g in range(kb // SUBLANES):
        rows = slice(SUBLANES * g, SUBLANES * (g + 1))
        s_t = s_ref[:, rows, :].reshape(n * SUBLANES, dv).T
        s_new = a_ref[pl.ds(group0 + g, 1), :] * s_t + k_ref[pl.ds(group0 + g, 1), :] * v_blk
        s_out_ref[:, rows, :] = s_new.T.reshape(n, SUBLANES, dv)
        acc = acc + q_ref[pl.ds(group0 + g, 1), :] * s_new
    o_ref[pl.ds(head_row, dv), :] += acc


def _state_update(state, layer, a_il, k_il, q_il, v_rep, *, kb, name):
    _, n, heads, dk, dv = state.shape
    per_head = dk // kb
    full = lambda a: pl.BlockSpec(a.shape, lambda j: (0, 0))
    slab = lambda lead: pl.BlockSpec((None, n, None, kb, dv), lambda j: (lead, 0, j // per_head, j % per_head, 0))
    return pl.pallas_call(
        functools.partial(_state_kernel, dk=dk, dv=dv, kb=kb),
        grid=(heads * per_head,),
        in_specs=[slab(layer), full(a_il), full(k_il), full(q_il), full(v_rep)],
        out_specs=[slab(0), full(v_rep)],
        out_shape=[jax.ShapeDtypeStruct((1,) + state.shape[1:], F32), jax.ShapeDtypeStruct(v_rep.shape, F32)],
        compiler_params=pltpu.CompilerParams(
            dimension_semantics=("arbitrary",), vmem_limit_bytes=V7X_VMEM_LIMIT),
        name=name,
    )(state, a_il, k_il, q_il, v_rep)


def _lane_group_sum(o_ref, sel):
    o = o_ref[...]
    hi = o.astype(BF16)
    lo = (o - hi.astype(F32)).astype(BF16)
    return _dot_nt(sel, hi) + _dot_nt(sel, lo)


def _tail_sample_kernel(x_ref, o_hg_t_ref, o_ret_ref, sel_ref, gates_ref, conv0_ref, p_ref,
                        g_hg_ref, g_ret_ref, w_br_hg_ref, w_br_ret_ref, w_out_ref,
                        g_ffn_ref, w_up_ref, conv_w_ref, conv_b_ref, w_down_ref,
                        g_ple_ref, w_ple_gate_ref, w_ple_ref, g_final_ref,
                        y_ref, conv_out_ref):
    h = _merge_branches(
        x_ref[...], o_hg_t_ref[...].T, _lane_group_sum(o_ret_ref, sel_ref[...]),
        gates_ref[:, 0:HG_W], gates_ref[:, HG_W:HG_W + RET_W], gates_ref[:, HG_W + RET_W:],
        g_hg_ref[...], g_ret_ref[...],
        w_br_hg_ref[...], w_br_ret_ref[...], w_out_ref[...])
    xn = _rms(h, g_ffn_ref[...]).astype(BF16)
    a = _dot(xn, w_up_ref[:, 0:D_FF])
    gate_half = _dot(xn, w_up_ref[:, D_FF:2 * D_FF])
    cw = conv_w_ref[...]
    prev = conv0_ref[:, 1, :]
    ac = conv_b_ref[...] + conv0_ref[:, 0, :] * cw[0:1, :] + prev * cw[1:2, :] + a * cw[2:3, :]
    conv_out_ref[0, :, 0, :] = prev
    conv_out_ref[0, :, 1, :] = a
    y_ref[...] = _ffn_tail(h, ac, gate_half, p_ref[...], w_down_ref[...], g_ple_ref[...],
                           w_ple_gate_ref[...], w_ple_ref[...], g_final_ref[...])


def _tail_sample(x, o_hg_t, o_ret, sel, gates, conv0, p, *weights):
    n = x.shape[0]
    return pl.pallas_call(
        _tail_sample_kernel,
        out_shape=[jax.ShapeDtypeStruct((n, D_MODEL), F32),
                   jax.ShapeDtypeStruct((1, n, CONV_W - 1, D_FF), F32)],
        compiler_params=pltpu.CompilerParams(vmem_limit_bytes=V7X_VMEM_LIMIT),
        name="tail_sample",
    )(x, o_hg_t, o_ret, sel, gates, conv0, p, *weights)


def kernel(x_prompt, x_sample, state_hgrn, state_ret, state_conv, p_prompt, p_sample, lb_logits, w_in, g_mix, g_hg_out, g_ret_out, w_br_hg, w_br_ret, w_out, g_ffn, w_up, conv_w, conv_b, w_down, g_ple, w_ple, w_ple_gate, g_final):
    assert DEPTH == 1 and x_prompt.shape == (BATCH, SEQ, D_MODEL) and x_sample.shape == (DEC_BATCH, DEC_SEQ, D_MODEL)
    i = 0
    row = lambda a: a.reshape(1, -1).astype(F32)
    lb = jnp.cumsum(jax.nn.softmax(lb_logits.astype(F32), axis=0), axis=0)[i].reshape(1, HG_W)

    sel2, level = _level_tables()
    tri = np.tril(_block_mask(MIX_SUB, MIX_SUB, FAST_CHUNK, FAST_CHUNK))
    causal = np.tile(np.tril(np.ones((FAST_CHUNK, FAST_CHUNK), np.int32)), (1, HG_GROUP))
    hmask = (np.arange(HG_W)[None, :] // HG_FDIM) % HG_GROUP == np.arange(HG_GROUP)[:, None]
    consts = (
        jnp.asarray(sel2, BF16),
        jnp.asarray(np.concatenate([tri, tri], axis=1), BF16),
        jnp.asarray(hmask, F32),
        jnp.asarray(causal),
        jnp.asarray(np.tile(level, (1, HG_GROUP))),
        jnp.asarray(_block_mask(MXU_W, MXU_W, CHUNK, HG_FDIM), BF16),
        jnp.asarray(_block_mask(MXU_W, MXU_W, HG_IDIM, HG_FDIM), F32),
        jnp.asarray(_block_mask(MXU_W, MXU_W, RET_DK, RET_DV), F32),
    ) + _retention_tables()

    w_in_b = w_in[i].astype(BF16)
    mix_w = (row(g_hg_out[i]), row(g_ret_out[i]),
             w_br_hg[i].astype(BF16), w_br_ret[i].astype(BF16), w_out[i].astype(BF16))
    ffn_w = (row(g_ffn[i]), w_up[i].astype(BF16), conv_w[i].astype(F32), row(conv_b[i]), w_down[i].astype(BF16),
             row(g_ple[i]), w_ple_gate[i].astype(BF16), w_ple[i].astype(BF16), row(g_final))

    cos_p, sin_p = _rope_tables(0.0, SEQ)
    h_p, hg_state_p, ret_state_p = _mix_prompt(x_prompt, cos_p, sin_p, w_in_b, row(g_mix[i]), lb, consts, *mix_w)
    y_prompt, conv_p = _ffn_prompt(h_p, p_prompt[i], *ffn_w)

    n = DEC_BATCH
    xs = x_sample.reshape(n, D_MODEL)
    cos_s, sin_s = _rope_tables(float(PAST_LEN), DEC_SEQ)
    f_t, kk_t, q_t, v_t, kr, qr, vr, gates = _proj_sample(xs, cos_s, sin_s, w_in_b, row(g_mix[i]), lb)
    hg_new_t, o_hg_t = _state_update_tokens_minor(
        jnp.transpose(state_hgrn, (0, 2, 3, 4, 1)), i, f_t, kk_t, q_t, v_t, name="state_hgrn")
    hg_new = jnp.transpose(hg_new_t, (0, 4, 1, 2, 3))
    interleave = lambda a: a.reshape(n, -1, SUBLANES).transpose(1, 0, 2).reshape(-1, n * SUBLANES)
    repeat = lambda a: jnp.repeat(a.T, SUBLANES, axis=1)
    gamma = jnp.asarray(np.broadcast_to(np.repeat(np.exp(_log_gamma()), RET_DK // SUBLANES)[:, None],
                                        (RET_W // SUBLANES, n * SUBLANES)))
    ret_new, o_ret = _state_update(
        state_ret, i, gamma, interleave(kr), interleave(qr), repeat(vr), kb=RET_DK // 2, name="state_ret")
    sel = jnp.asarray(_block_mask(n, n * SUBLANES, 1, SUBLANES), BF16)
    y_sample, conv_s = _tail_sample(
        xs, o_hg_t, o_ret, sel, gates, state_conv[i], p_sample[i].reshape(n, PLE_DIM),
        *mix_w, *ffn_w)

    return (y_prompt, y_sample.reshape(n, DEC_SEQ, D_MODEL),
            hg_state_p, ret_state_p, conv_p[None],
            hg_new, ret_new,
            conv_s)
```

```python
import functools

import numpy as np
import jax
import jax.numpy as jnp
from jax import lax
from jax.experimental import pallas as pl
from jax.experimental.pallas import tpu as pltpu

D_MODEL = 1024
BATCH = 8
SEQ = 2048
DEPTH = 1
DEC_BATCH = 128
DEC_SEQ = 1
PAST_LEN = 16384
HG_HEADS = 8
HG_FDIM = 64
HG_IDIM = 64
HG_W = HG_HEADS * HG_FDIM
RET_HEADS = 4
RET_DK = 128
RET_DV = 128
RET_W = RET_HEADS * RET_DK
D_FF = 2816
CONV_W = 3
PLE_DIM = 256
CHUNK = 64
ROPE_BASE = 10000.0
EPS = 1e-6

C_HQ, C_HF, C_HI, C_HG = 0, 512, 1024, 1536
C_RQ, C_RK, C_RV, C_RG = 2048, 2560, 3072, 3584
C_GA, C_GB, C_END = 4096, 5120, 6144

N_LEVELS = 6
FAST_CHUNK = 128
FAST_DECAY_LIMIT = 80.0
HG_GROUP = 4
RET_GROUP = 2
MXU_W = 256
SUBLANES = 8
LANES = 128

MIX_TILE = 256
MIX_SUB = 256
FFN_TILE = 512
FFN_SUB = 256
CARRY_ROWS = 8
V7X_VMEM_LIMIT = 56 * 1024 * 1024

F32 = jnp.float32
BF16 = jnp.bfloat16


def _level_tables():
    c = CHUNK
    t = np.arange(c)[:, None]
    r = np.arange(c)[None, :]
    blocks = [r <= t, r > t]
    for l in range(N_LEVELS):
        same = (t >> l) == (r >> l)
        upper = ((t >> l) & 1) == 1
        blocks.append(np.where(upper, same & (r <= t), same & (r > t)))
    sel = np.concatenate(blocks, axis=0).astype(np.float32)
    sel2 = np.concatenate([sel, sel], axis=1)
    s = r
    level = np.full((c, c), N_LEVELS + 1, np.int32)
    level[np.broadcast_to(s == t, (c, c))] = N_LEVELS
    x = t ^ s
    for l in range(N_LEVELS):
        hit = (s < t) & ((x >> l) == 1)
        level[hit] = l
    return sel2, level


def _block_mask(rows, cols, rblk, cblk):
    r = np.arange(rows)[:, None] // rblk
    c = np.arange(cols)[None, :] // cblk
    return (r == c).astype(np.float32)


def _rope_tables(pos0, length):
    d = RET_DK
    inv_freq = ROPE_BASE ** (-np.arange(0, d, 2, dtype=np.float64) / d)
    pos = pos0 + np.arange(length, dtype=np.float64)
    ang = pos[:, None] * inv_freq[None, :]
    cos, sin = np.cos(ang), np.sin(ang)
    cos_full = np.tile(np.concatenate([cos, cos], axis=-1), (1, RET_HEADS))
    sin_signed = np.tile(np.concatenate([-sin, sin], axis=-1), (1, RET_HEADS))
    return jnp.asarray(cos_full, F32), jnp.asarray(sin_signed, F32)


def _log_gamma():
    return np.log1p(-np.exp2(-5.0 - np.arange(RET_HEADS, dtype=np.float32))).astype(np.float32)


def _retention_tables():
    c = MIX_SUB
    log_gamma = _log_gamma()
    t = np.arange(c, dtype=np.float32)
    delta = t[:, None] - t[None, :]
    dmask = np.where(delta >= 0, np.exp(np.maximum(delta, 0.0) * log_gamma[:, None, None]), 0.0)
    q_decay = np.exp((t + 1.0)[None, :] * log_gamma[:, None])
    k_decay = np.exp((c - 1.0 - t)[None, :] * log_gamma[:, None])
    qdec = np.repeat(q_decay.T, RET_DK, axis=1)
    kdec = np.repeat(k_decay.T, RET_DK, axis=1)
    chunk_decay = np.exp(np.float32(c) * log_gamma)
    cd = np.repeat(chunk_decay, RET_DK).reshape(RET_HEADS // RET_GROUP, MXU_W, 1)
    cd = np.broadcast_to(cd, (RET_HEADS // RET_GROUP, MXU_W, MXU_W))
    return tuple(jnp.asarray(a, F32) for a in (dmask, qdec, kdec, cd))


def _rms(x, g):
    ms = jnp.mean(x * x, axis=-1, keepdims=True)
    return x * lax.rsqrt(ms + EPS) * g


def _silu(x):
    return x * jax.nn.sigmoid(x)


def _dot(a, b):
    return jnp.dot(a, b, preferred_element_type=F32)


def _dot_nt(a, b):
    return lax.dot_general(a, b, (((1,), (1,)), ((), ())), preferred_element_type=F32)


def _dot_tn(a, b):
    return lax.dot_general(a, b, (((0,), (0,)), ((), ())), preferred_element_type=F32)


def _rotary(x, cos_full, sin_signed):
    swapped = jnp.concatenate(
        [pltpu.roll(x[:, RET_DK * h:RET_DK * (h + 1)], RET_DK // 2, 1) for h in range(RET_HEADS)], axis=1)
    return x * cos_full + swapped * sin_signed


def _hgrn_forget(hf, lb):
    f = lb + (1.0 - lb) * jax.nn.sigmoid(hf)
    return f, 1.0 - f


def _head_mean_square(o, head_w):
    parts = []
    for g in range(o.shape[1] // LANES):
        sq = o[:, LANES * g:LANES * (g + 1)]
        sq = sq * sq
        if head_w == LANES:
            ms = jnp.broadcast_to(jnp.sum(sq, axis=-1, keepdims=True), sq.shape)
        else:
            low = lax.broadcasted_iota(jnp.int32, sq.shape, 1) < head_w
            lo = jnp.sum(jnp.where(low, sq, 0.0), axis=-1, keepdims=True)
            hi = jnp.sum(jnp.where(low, 0.0, sq), axis=-1, keepdims=True)
            ms = jnp.where(low, lo, hi)
        parts.append(ms * (1.0 / head_w))
    return jnp.concatenate(parts, axis=1)


def _merge_branches(x, o_hg, o_ret, u_hg, u_rg, u_gab, g_hg, g_ret, w_br_hg, w_br_ret, w_out):
    a_hg = o_hg * lax.rsqrt(_head_mean_square(o_hg, HG_IDIM) + EPS) * g_hg * _silu(u_hg)
    y_hg = _dot(a_hg.astype(BF16), w_br_hg)
    a_ret = o_ret * lax.rsqrt(_head_mean_square(o_ret, RET_DV) + EPS) * g_ret * _silu(u_rg)
    y_ret = _dot(a_ret.astype(BF16), w_br_ret)
    mixed = jax.nn.sigmoid(u_gab[:, :D_MODEL]) * y_hg + jax.nn.sigmoid(u_gab[:, D_MODEL:]) * y_ret
    return x + _dot(mixed.astype(BF16), w_out)


def _ffn_tail(h, mid, p, w_down, g_ple, w_ple_gate, w_ple, g_final):
    h = h + _dot(mid, w_down)
    gate = jax.nn.sigmoid(_dot(_rms(h, g_ple).astype(BF16), w_ple_gate))
    h = h + gate * _dot(p.astype(BF16), w_ple)
    return _rms(h, g_final)


def _mix_prompt_kernel(x_ref, cos_ref, sin_ref, w_in_ref, g_mix_ref, lb_ref, sel_ref, tri_ref, hmask_ref, causal_ref, lvl_ref,
                       bd64_ref, bd64f_ref, bd128f_ref, dmask_ref, qdec_ref, kdec_ref, cd_ref,
                       g_hg_ref, g_ret_ref, w_br_hg_ref, w_br_ret_ref, w_out_ref,
                       h_ref, st_hg_out_ref, st_ret_out_ref,
                       xn_s, q_s, kk_s, lfh_s, lfl_s, v_s, vbd_s, qh_s, qi_s, kbd_s, ks_s, alast_s, o_hg_s, o_ret_s, st_hg, st_ret):
    t_idx = pl.program_id(1)
    n_hg_groups = HG_HEADS // HG_GROUP
    n_ret_groups = RET_HEADS // RET_GROUP

    @pl.when(t_idx == 0)
    def _():
        st_hg[...] = jnp.zeros_like(st_hg)
        st_ret[...] = jnp.zeros_like(st_ret)

    subs = [slice(i * MIX_SUB, (i + 1) * MIX_SUB) for i in range(MIX_TILE // MIX_SUB)]
    chunk_sums = []
    for r in subs:
        xn = _rms(x_ref[r, :], g_mix_ref[...]).astype(BF16)
        xn_s[r, :] = xn

        u = _dot(xn, w_in_ref[:, C_HQ:C_HG])
        f, kk = _hgrn_forget(u[:, HG_W:2 * HG_W], lb_ref[...])
        q_s[r, :] = _silu(u[:, 0:HG_W])
        kk_s[r, :] = kk
        lf = jnp.log(f)
        lf_hi = lf.astype(BF16)
        lfh_s[r, :] = lf_hi
        lfl_s[r, :] = (lf - lf_hi.astype(F32)).astype(BF16)
        v = u[:, 2 * HG_W:3 * HG_W]
        v_s[r, :] = v.astype(BF16)
        for h in range(HG_GROUP):
            vbd_s[h, r, :] = (v * hmask_ref[h:h + 1, :]).astype(BF16)
        chunk_sums += [jnp.sum(lf[c * CHUNK:(c + 1) * CHUNK, :], axis=0, keepdims=True)
                       for c in range(MIX_SUB // CHUNK)]

    mild = jnp.min(functools.reduce(jnp.minimum, chunk_sums)) >= -FAST_DECAY_LIMIT

    groups = [slice(MXU_W * g, MXU_W * (g + 1)) for g in range(n_hg_groups)]
    bd_tile = lambda ref, rows, cs: jnp.concatenate([ref[h, rows, cs] for h in range(HG_GROUP)], axis=0)

    def prepare_fast():
        per_sub = MIX_SUB // FAST_CHUNK
        rep = lambda rows: jnp.concatenate([jnp.broadcast_to(e, (FAST_CHUNK, HG_W)) for e in rows], axis=0)
        for i, r in enumerate(subs):
            lf2 = jnp.concatenate([lfh_s[r, :], lfl_s[r, :]], axis=0)
            cum = _dot(tri_ref[...], lf2)
            mids = [cum[c * FAST_CHUNK + FAST_CHUNK // 2 - 1:c * FAST_CHUNK + FAST_CHUNK // 2, :] for c in range(per_sub)]
            ends = [cum[(c + 1) * FAST_CHUNK - 1:(c + 1) * FAST_CHUNK, :] for c in range(per_sub)]
            mid = rep(mids)
            kk = kk_s[r, :]
            q = q_s[r, :]
            qh_s[r, :] = (q * jnp.exp(cum - mid)).astype(BF16)
            qi_s[r, :] = (q * jnp.exp(cum)).astype(BF16)
            k_mid = kk * jnp.exp(mid - cum)
            for h in range(HG_GROUP):
                kbd_s[h, r, :] = (k_mid * hmask_ref[h:h + 1, :]).astype(BF16)
            ks_s[r, :] = (kk * jnp.exp(rep(ends) - cum)).astype(BF16)
            for c, e in enumerate(ends):
                ci = i * per_sub + c
                alast_s[ci:ci + 1, :] = jnp.exp(e)

    def chunk_fast_products(c):
        rows = slice(c * FAST_CHUNK, (c + 1) * FAST_CHUNK)
        causal = causal_ref[...] != 0
        bd64f = bd64f_ref[...]
        scores = [jnp.where(causal, _dot_nt(qh_s[rows, cs], bd_tile(kbd_s, rows, cs)), 0.0).astype(BF16)
                  for cs in groups]
        upds = [_dot_tn(v_s[rows, cs], ks_s[rows, cs]) * bd64f for cs in groups]
        return scores, upds

    def chunk_fast(c, scores, upds):
        rows = slice(c * FAST_CHUNK, (c + 1) * FAST_CHUNK)
        o_parts = []
        for g, cs in enumerate(groups):
            st = st_hg[g]
            o_parts.append(_dot(scores[g], bd_tile(vbd_s, rows, cs)) + _dot_nt(qi_s[rows, cs], st.astype(BF16)))
            st_hg[g] = st * alast_s[c:c + 1, cs] + upds[g]
        o_hg_s[rows, :] = jnp.concatenate(o_parts, axis=1)

    def chunk_safe(c):
        rows = slice(c * CHUNK, (c + 1) * CHUNK)
        q = q_s[rows, :]
        kk = kk_s[rows, :]
        lf2 = jnp.concatenate([lfh_s[rows, :], lfl_s[rows, :]], axis=0)
        ex = jnp.exp(_dot(sel_ref[...], lf2))
        lvl = lvl_ref[...]
        bd64 = bd64_ref[...]
        bd64f = bd64f_ref[...]
        o_parts = []
        for g, cs in enumerate(groups):
            scores = jnp.zeros((CHUNK, MXU_W), F32)
            for l in range(N_LEVELS + 1):
                if l < N_LEVELS:
                    e = ex[CHUNK * (l + 2):CHUNK * (l + 3), cs]
                    ql = (q[:, cs] * e).astype(BF16)
                    kl = (kk[:, cs] * e).astype(BF16)
                else:
                    ql = q[:, cs].astype(BF16)
                    kl = kk[:, cs].astype(BF16)
                rhs = jnp.concatenate([kl] * HG_GROUP, axis=0) * bd64
                scores = jnp.where(lvl == l, _dot_nt(ql, rhs), scores)
            e_cum = ex[0:CHUNK, cs]
            e_suf = ex[CHUNK:2 * CHUNK, cs]
            st = st_hg[g]
            o_inter = _dot_nt((q[:, cs] * e_cum).astype(BF16), st.astype(BF16))
            o_parts.append(_dot(scores.astype(BF16), bd_tile(vbd_s, rows, cs)) + o_inter)
            upd = _dot_tn(v_s[rows, cs], (kk[:, cs] * e_suf).astype(BF16))
            st_hg[g] = st * e_cum[CHUNK - 1:CHUNK, :] + upd * bd64f
        o_hg_s[rows, :] = jnp.concatenate(o_parts, axis=1)

    def retention(r):
        xn = xn_s[r, :]
        u = _dot(xn, w_in_ref[:, C_RQ:C_RG])
        cos_full = cos_ref[r, :]
        sin_signed = sin_ref[r, :]
        qr = _rotary(u[:, 0:RET_W], cos_full, sin_signed)
        kr = _rotary(u[:, RET_W:2 * RET_W], cos_full, sin_signed) * (RET_DK ** -0.5)
        vr = u[:, 2 * RET_W:3 * RET_W].astype(BF16)
        qrb = qr.astype(BF16)
        krb = kr.astype(BF16)
        o_heads = []
        for h in range(RET_HEADS):
            hs = slice(RET_DK * h, RET_DK * (h + 1))
            scores = _dot_nt(qrb[:, hs], krb[:, hs]) * dmask_ref[h]
            o_heads.append(_dot(scores.astype(BF16), vr[:, hs]))
        qd = (qr * qdec_ref[...]).astype(BF16)
        kd = (kr * kdec_ref[...]).astype(BF16)
        bd128f = bd128f_ref[...]
        o_inter = []
        for g in range(n_ret_groups):
            cs = slice(MXU_W * g, MXU_W * (g + 1))
            st = st_ret[g]
            o_inter.append(_dot(qd[:, cs], st.astype(BF16)))
            st_ret[g] = cd_ref[g] * st + _dot_tn(kd[:, cs], vr[:, cs]) * bd128f
        o_ret_s[r, :] = jnp.concatenate(o_heads, axis=1) + jnp.concatenate(o_inter, axis=1)

    def recurrences(fast):
        if fast:
            prepare_fast()
        for r in subs:
            retention(r)
        if fast:
            products = [chunk_fast_products(c) for c in range(MIX_TILE // FAST_CHUNK)]
            for c, (scores, upds) in enumerate(products):
                chunk_fast(c, scores, upds)
        else:
            for c in range(MIX_TILE // CHUNK):
                chunk_safe(c)

    pl.when(mild)(functools.partial(recurrences, True))
    pl.when(jnp.logical_not(mild))(functools.partial(recurrences, False))

    for r in subs:
        xn = xn_s[r, :]
        h_ref[r, :] = _merge_branches(
            x_ref[r, :], o_hg_s[r, :], o_ret_s[r, :],
            _dot(xn, w_in_ref[:, C_HG:C_RQ]), _dot(xn, w_in_ref[:, C_RG:C_GA]), _dot(xn, w_in_ref[:, C_GA:C_END]),
            g_hg_ref[...], g_ret_ref[...],
            w_br_hg_ref[...], w_br_ret_ref[...], w_out_ref[...])

    @pl.when(t_idx == pl.num_programs(1) - 1)
    def _():
        for g in range(n_hg_groups):
            st_t = st_hg[g].T
            for h in range(HG_GROUP):
                blk = slice(HG_FDIM * h, HG_FDIM * (h + 1))
                st_hg_out_ref[HG_GROUP * g + h] = st_t[blk, blk]
        for g in range(n_ret_groups):
            for h in range(RET_GROUP):
                blk = slice(RET_DK * h, RET_DK * (h + 1))
                st_ret_out_ref[RET_GROUP * g + h] = st_ret[g, blk, blk]


def _const_spec(shape):
    zeros = (0,) * len(shape)
    return pl.BlockSpec(shape, lambda *_: zeros, pipeline_mode=pl.Buffered(1))


def _mix_prompt(x, cos_full, sin_signed, w_in, g_mix, lb, consts, g_hg, g_ret,
                w_br_hg, w_br_ret, w_out):
    b, l, d = x.shape
    nt = l // MIX_TILE
    tile = lambda w: pl.BlockSpec((None, MIX_TILE, w), lambda i, j: (i, j, 0))
    table = lambda w: pl.BlockSpec((MIX_TILE, w), lambda i, j: (j, 0))
    n_hg_groups = HG_HEADS // HG_GROUP
    n_ret_groups = RET_HEADS // RET_GROUP
    const_args = (w_in, g_mix, lb) + tuple(consts) + (g_hg, g_ret, w_br_hg, w_br_ret, w_out)
    state_spec = lambda h, dk, dv: pl.BlockSpec((None, None, h, dk, dv), lambda i, j: (0, i, 0, 0, 0))
    return pl.pallas_call(
        _mix_prompt_kernel,
        grid=(b, nt),
        in_specs=[tile(d), table(RET_W), table(RET_W)] + [_const_spec(a.shape) for a in const_args],
        out_specs=[tile(d), state_spec(HG_HEADS, HG_FDIM, HG_IDIM), state_spec(RET_HEADS, RET_DK, RET_DV)],
        out_shape=[jax.ShapeDtypeStruct((b, l, d), F32),
                   jax.ShapeDtypeStruct((1, b, HG_HEADS, HG_FDIM, HG_IDIM), F32),
                   jax.ShapeDtypeStruct((1, b, RET_HEADS, RET_DK, RET_DV), F32)],
        scratch_shapes=[
            pltpu.VMEM((MIX_TILE, d), BF16),
            pltpu.VMEM((MIX_TILE, HG_W), F32),
            pltpu.VMEM((MIX_TILE, HG_W), F32),
            pltpu.VMEM((MIX_TILE, HG_W), BF16),
            pltpu.VMEM((MIX_TILE, HG_W), BF16),
            pltpu.VMEM((MIX_TILE, HG_W), BF16),
            pltpu.VMEM((HG_GROUP, MIX_TILE, HG_W), BF16),
            pltpu.VMEM((MIX_TILE, HG_W), BF16),
            pltpu.VMEM((MIX_TILE, HG_W), BF16),
            pltpu.VMEM((HG_GROUP, MIX_TILE, HG_W), BF16),
            pltpu.VMEM((MIX_TILE, HG_W), BF16),
            pltpu.VMEM((max(MIX_TILE // CHUNK, SUBLANES), HG_W), F32),
            pltpu.VMEM((MIX_TILE, HG_W), F32),
            pltpu.VMEM((MIX_TILE, RET_W), F32),
            pltpu.VMEM((n_hg_groups, MXU_W, MXU_W), F32),
            pltpu.VMEM((n_ret_groups, MXU_W, MXU_W), F32),
        ],
        compiler_params=pltpu.CompilerParams(
            dimension_semantics=("arbitrary", "arbitrary"), vmem_limit_bytes=V7X_VMEM_LIMIT),
        name="mix_prompt",
    )(x, cos_full, sin_signed, *const_args)


def _ffn_prompt_kernel(h_ref, p_ref, g_ffn_ref, w_up_ref, conv_w_ref, conv_b_ref, w_down_ref,
                       g_ple_ref, w_ple_gate_ref, w_ple_ref, g_final_ref,
                       y_ref, conv_out_ref, a_ext):
    t_idx = pl.program_id(1)
    tl = FFN_TILE

    @pl.when(t_idx == 0)
    def _():
        a_ext[0:CARRY_ROWS, :] = jnp.zeros((CARRY_ROWS, D_FF), F32)

    @pl.when(t_idx > 0)
    def _():
        a_ext[0:CARRY_ROWS, :] = a_ext[tl:tl + CARRY_ROWS, :]

    subs = [slice(i * FFN_SUB, (i + 1) * FFN_SUB) for i in range(tl // FFN_SUB)]
    xns = []
    for r in subs:
        xns.append(_rms(h_ref[r, :], g_ffn_ref[...]).astype(BF16))
        a_ext[CARRY_ROWS + r.start:CARRY_ROWS + r.stop, :] = _dot(xns[-1], w_up_ref[:, 0:D_FF])
    cw = conv_w_ref[...]
    mids = []
    for r, xn in zip(subs, xns):
        ac = conv_b_ref[...]
        for j in range(CONV_W):
            off = CARRY_ROWS - (CONV_W - 1) + j
            ac = ac + a_ext[off + r.start:off + r.stop, :] * cw[j:j + 1, :]
        mids.append((jax.nn.gelu(ac) * _dot(xn, w_up_ref[:, D_FF:2 * D_FF])).astype(BF16))
    for r, mid in zip(subs, mids):
        y_ref[r, :] = _ffn_tail(h_ref[r, :], mid, p_ref[r, :], w_down_ref[...], g_ple_ref[...],
                                w_ple_gate_ref[...], w_ple_ref[...], g_final_ref[...])

    @pl.when(t_idx == pl.num_programs(1) - 1)
    def _():
        conv_out_ref[...] = a_ext[CARRY_ROWS + tl - (CONV_W - 1):CARRY_ROWS + tl, :]


def _ffn_prompt(h, p, g_ffn, w_up, conv_w, conv_b, w_down, g_ple, w_ple_gate, w_ple, g_final):
    b, l, d = h.shape
    nt = l // FFN_TILE
    tile = lambda w: pl.BlockSpec((None, FFN_TILE, w), lambda i, j: (i, j, 0))
    const_args = (g_ffn, w_up, conv_w, conv_b, w_down, g_ple, w_ple_gate, w_ple, g_final)
    return pl.pallas_call(
        _ffn_prompt_kernel,
        grid=(b, nt),
        in_specs=[tile(d), tile(PLE_DIM)] + [_const_spec(a.shape) for a in const_args],
        out_specs=[tile(d), pl.BlockSpec((None, CONV_W - 1, D_FF), lambda i, j: (i, 0, 0))],
        out_shape=[jax.ShapeDtypeStruct((b, l, d), F32),
                   jax.ShapeDtypeStruct((b, CONV_W - 1, D_FF), F32)],
        scratch_shapes=[pltpu.VMEM((FFN_TILE + CARRY_ROWS, D_FF), F32)],
        compiler_params=pltpu.CompilerParams(
            dimension_semantics=("arbitrary", "arbitrary"), vmem_limit_bytes=V7X_VMEM_LIMIT),
        name="ffn_prompt",
    )(h, p, *const_args)


def _proj_sample_kernel(x_ref, cos_ref, sin_ref, w_in_ref, g_mix_ref, lb_ref,
                        f_ref, kk_ref, q_ref, v_ref, kr_ref, qr_ref, vr_ref, gates_ref):
    xn = _rms(x_ref[...], g_mix_ref[...]).astype(BF16)
    u = _dot(xn, w_in_ref[:, C_HQ:C_HG])
    f, kk = _hgrn_forget(u[:, HG_W:2 * HG_W], lb_ref[...])
    f_ref[...] = f.T
    kk_ref[...] = kk.T
    q_ref[...] = _silu(u[:, 0:HG_W]).T
    v_ref[...] = u[:, 2 * HG_W:3 * HG_W].T
    u = _dot(xn, w_in_ref[:, C_RQ:C_RG])
    cos_full = cos_ref[...]
    sin_signed = sin_ref[...]
    qr_ref[...] = _rotary(u[:, 0:RET_W], cos_full, sin_signed)
    kr_ref[...] = _rotary(u[:, RET_W:2 * RET_W], cos_full, sin_signed) * (RET_DK ** -0.5)
    vr_ref[...] = u[:, 2 * RET_W:3 * RET_W]
    gates_ref[:, 0:HG_W] = _dot(xn, w_in_ref[:, C_HG:C_RQ])
    gates_ref[:, HG_W:HG_W + RET_W] = _dot(xn, w_in_ref[:, C_RG:C_GA])
    gates_ref[:, HG_W + RET_W:] = _dot(xn, w_in_ref[:, C_GA:C_END])


def _proj_sample(x, cos_full, sin_signed, w_in, g_mix, lb):
    n = x.shape[0]
    col = jax.ShapeDtypeStruct((HG_W, n), F32)
    row = jax.ShapeDtypeStruct((n, RET_W), F32)
    return pl.pallas_call(
        _proj_sample_kernel,
        out_shape=[col] * 4 + [row] * 3 + [jax.ShapeDtypeStruct((n, HG_W + RET_W + 2 * D_MODEL), F32)],
        compiler_params=pltpu.CompilerParams(vmem_limit_bytes=V7X_VMEM_LIMIT),
        name="proj_sample",
    )(x, cos_full, sin_signed, w_in, g_mix, lb)


def _state_tokens_minor_kernel(s_ref, a_t_ref, k_t_ref, q_t_ref, v_t_ref, s_out_ref, o_t_ref):
    dk, dv, _ = s_ref.shape
    row0 = pl.program_id(0) * dk
    v_blk = v_t_ref[...]

    def body(k, acc):
        s_new = a_t_ref[pl.ds(row0 + k, 1), :] * s_ref[k] + k_t_ref[pl.ds(row0 + k, 1), :] * v_blk
        s_out_ref[k] = s_new
        return acc + q_t_ref[pl.ds(row0 + k, 1), :] * s_new

    o_t_ref[...] = lax.fori_loop(0, dk, body, jnp.zeros(v_blk.shape, F32), unroll=SUBLANES)


def _state_update_tokens_minor(state_t, layer, a_t, k_t, q_t, v_t, *, name):
    _, heads, dk, dv, n = state_t.shape
    full = lambda a: pl.BlockSpec(a.shape, lambda h: (0, 0))
    per_head = pl.BlockSpec((dv, n), lambda h: (h, 0))
    slab = lambda lead: pl.BlockSpec((None, None, dk, dv, n), lambda h: (lead, h, 0, 0, 0))
    return pl.pallas_call(
        _state_tokens_minor_kernel,
        grid=(heads,),
        in_specs=[slab(layer), full(a_t), full(k_t), full(q_t), per_head],
        out_specs=[slab(0), per_head],
        out_shape=[jax.ShapeDtypeStruct((1,) + state_t.shape[1:], F32), jax.ShapeDtypeStruct(v_t.shape, F32)],
        compiler_params=pltpu.CompilerParams(
            dimension_semantics=("arbitrary",), vmem_limit_bytes=V7X_VMEM_LIMIT),
        name=name,
    )(state_t, a_t, k_t, q_t, v_t)


def _state_kernel(s_ref, a_ref, k_ref, q_ref, v_ref, s_out_ref, o_ref, *, dk, dv, kb):
    j = pl.program_id(0)
    n = s_ref.shape[0]

    @pl.when(j == 0)
    def _():
        o_ref[...] = jnp.zeros_like(o_ref)

    group0 = j * (kb // SUBLANES)
    head_row = pl.multiple_of((j * kb // dk) * dv, dv)
    v_blk = v_ref[pl.ds(head_row, dv), :]
    acc = jnp.zeros(v_blk.shape, F32)
    for g in range(kb // SUBLANES):
        rows = slice(SUBLANES * g, SUBLANES * (g + 1))
        s_t = s_ref[:, rows, :].reshape(n * SUBLANES, dv).T
        s_new = a_ref[pl.ds(group0 + g, 1), :] * s_t + k_ref[pl.ds(group0 + g, 1), :] * v_blk
        s_out_ref[:, rows, :] = s_new.T.reshape(n, SUBLANES, dv)
        acc = acc + q_ref[pl.ds(group0 + g, 1), :] * s_new
    o_ref[pl.ds(head_row, dv), :] += acc


def _state_update(state, layer, a_il, k_il, q_il, v_rep, *, kb, name):
    _, n, heads, dk, dv = state.shape
    per_head = dk // kb
    full = lambda a: pl.BlockSpec(a.shape, lambda j: (0, 0))
    slab = lambda lead: pl.BlockSpec((None, n, None, kb, dv), lambda j: (lead, 0, j // per_head, j % per_head, 0))
    return pl.pallas_call(
        functools.partial(_state_kernel, dk=dk, dv=dv, kb=kb),
        grid=(heads * per_head,),
        in_specs=[slab(layer), full(a_il), full(k_il), full(q_il), full(v_rep)],
        out_specs=[slab(0), full(v_rep)],
        out_shape=[jax.ShapeDtypeStruct((1,) + state.shape[1:], F32), jax.ShapeDtypeStruct(v_rep.shape, F32)],
        compiler_params=pltpu.CompilerParams(
            dimension_semantics=("arbitrary",), vmem_limit_bytes=V7X_VMEM_LIMIT),
        name=name,
    )(state, a_il, k_il, q_il, v_rep)


def _lane_group_sum(o_ref, sel):
    o = o_ref[...]
    hi = o.astype(BF16)
    lo = (o - hi.astype(F32)).astype(BF16)
    return _dot_nt(sel, hi) + _dot_nt(sel, lo)


def _tail_sample_kernel(x_ref, o_hg_t_ref, o_ret_ref, sel_ref, gates_ref, conv0_ref, p_ref,
                        g_hg_ref, g_ret_ref, w_br_hg_ref, w_br_ret_ref, w_out_ref,
                        g_ffn_ref, w_up_ref, conv_w_ref, conv_b_ref, w_down_ref,
                        g_ple_ref, w_ple_gate_ref, w_ple_ref, g_final_ref,
                        y_ref, conv_out_ref):
    h = _merge_branches(
        x_ref[...], o_hg_t_ref[...].T, _lane_group_sum(o_ret_ref, sel_ref[...]),
        gates_ref[:, 0:HG_W], gates_ref[:, HG_W:HG_W + RET_W], gates_ref[:, HG_W + RET_W:],
        g_hg_ref[...], g_ret_ref[...],
        w_br_hg_ref[...], w_br_ret_ref[...], w_out_ref[...])
    xn = _rms(h, g_ffn_ref[...]).astype(BF16)
    a = _dot(xn, w_up_ref[:, 0:D_FF])
    gate_half = _dot(xn, w_up_ref[:, D_FF:2 * D_FF])
    cw = conv_w_ref[...]
    prev = conv0_ref[:, 1, :]
    ac = conv_b_ref[...] + conv0_ref[:, 0, :] * cw[0:1, :] + prev * cw[1:2, :] + a * cw[2:3, :]
    conv_out_ref[0, :, 0, :] = prev
    conv_out_ref[0, :, 1, :] = a
    y_ref[...] = _ffn_tail(h, (jax.nn.gelu(ac) * gate_half).astype(BF16), p_ref[...], w_down_ref[...], g_ple_ref[...],
                           w_ple_gate_ref[...], w_ple_ref[...], g_final_ref[...])


def _tail_sample(x, o_hg_t, o_ret, sel, gates, conv0, p, *weights):
    n = x.shape[0]
    return pl.pallas_call(
        _tail_sample_kernel,
        out_shape=[jax.ShapeDtypeStruct((n, D_MODEL), F32),
                   jax.ShapeDtypeStruct((1, n, CONV_W - 1, D_FF), F32)],
        compiler_params=pltpu.CompilerParams(vmem_limit_bytes=V7X_VMEM_LIMIT),
        name="tail_sample",
    )(x, o_hg_t, o_ret, sel, gates, conv0, p, *weights)


def kernel(x_prompt, x_sample, state_hgrn, state_ret, state_conv, p_prompt, p_sample, lb_logits, w_in, g_mix, g_hg_out, g_ret_out, w_br_hg, w_br_ret, w_out, g_ffn, w_up, conv_w, conv_b, w_down, g_ple, w_ple, w_ple_gate, g_final):
    assert DEPTH == 1 and x_prompt.shape == (BATCH, SEQ, D_MODEL) and x_sample.shape == (DEC_BATCH, DEC_SEQ, D_MODEL)
    i = 0
    row = lambda a: a.reshape(1, -1).astype(F32)
    lb = jnp.cumsum(jax.nn.softmax(lb_logits.astype(F32), axis=0), axis=0)[i].reshape(1, HG_W)

    sel2, level = _level_tables()
    tri = np.tril(_block_mask(MIX_SUB, MIX_SUB, FAST_CHUNK, FAST_CHUNK))
    causal = np.tile(np.tril(np.ones((FAST_CHUNK, FAST_CHUNK), np.int32)), (1, HG_GROUP))
    hmask = (np.arange(HG_W)[None, :] // HG_FDIM) % HG_GROUP == np.arange(HG_GROUP)[:, None]
    consts = (
        jnp.asarray(sel2, BF16),
        jnp.asarray(np.concatenate([tri, tri], axis=1), BF16),
        jnp.asarray(hmask, F32),
        jnp.asarray(causal),
        jnp.asarray(np.tile(level, (1, HG_GROUP))),
        jnp.asarray(_block_mask(MXU_W, MXU_W, CHUNK, HG_FDIM), BF16),
        jnp.asarray(_block_mask(MXU_W, MXU_W, HG_IDIM, HG_FDIM), F32),
        jnp.asarray(_block_mask(MXU_W, MXU_W, RET_DK, RET_DV), F32),
    ) + _retention_tables()

    w_in_b = w_in[i].astype(BF16)
    mix_w = (row(g_hg_out[i]), row(g_ret_out[i]),
             w_br_hg[i].astype(BF16), w_br_ret[i].astype(BF16), w_out[i].astype(BF16))
    ffn_w = (row(g_ffn[i]), w_up[i].astype(BF16), conv_w[i].astype(F32), row(conv_b[i]), w_down[i].astype(BF16),
             row(g_ple[i]), w_ple_gate[i].astype(BF16), w_ple[i].astype(BF16), row(g_final))

    cos_p, sin_p = _rope_tables(0.0, SEQ)
    h_p, hg_state_p, ret_state_p = _mix_prompt(x_prompt, cos_p, sin_p, w_in_b, row(g_mix[i]), lb, consts, *mix_w)
    y_prompt, conv_p = _ffn_prompt(h_p, p_prompt[i], *ffn_w)

    n = DEC_BATCH
    xs = x_sample.reshape(n, D_MODEL)
    cos_s, sin_s = _rope_tables(float(PAST_LEN), DEC_SEQ)
    f_t, kk_t, q_t, v_t, kr, qr, vr, gates = _proj_sample(xs, cos_s, sin_s, w_in_b, row(g_mix[i]), lb)
    hg_new_t, o_hg_t = _state_update_tokens_minor(
        jnp.transpose(state_hgrn, (0, 2, 3, 4, 1)), i, f_t, kk_t, q_t, v_t, name="state_hgrn")
    hg_new = jnp.transpose(hg_new_t, (0, 4, 1, 2, 3))
    interleave = lambda a: a.reshape(n, -1, SUBLANES).transpose(1, 0, 2).reshape(-1, n * SUBLANES)
    repeat = lambda a: jnp.repeat(a.T, SUBLANES, axis=1)
    gamma = jnp.asarray(np.broadcast_to(np.repeat(np.exp(_log_gamma()), RET_DK // SUBLANES)[:, None],
                                        (RET_W // SUBLANES, n * SUBLANES)))
    ret_new, o_ret = _state_update(
        state_ret, i, gamma, interleave(kr), interleave(qr), repeat(vr), kb=RET_DK // 2, name="state_ret")
    sel = jnp.asarray(_block_mask(n, n * SUBLANES, 1, SUBLANES), BF16)
    y_sample, conv_s = _tail_sample(
        xs, o_hg_t, o_ret, sel, gates, state_conv[i], p_sample[i].reshape(n, PLE_DIM),
        *mix_w, *ffn_w)

    return (y_prompt, y_sample.reshape(n, DEC_SEQ, D_MODEL),
            hg_state_p, ret_state_p, conv_p[None],
            hg_new, ret_new,
            conv_s)
```

```python
import functools

import numpy as np
import jax
import jax.numpy as jnp
from jax import lax
from jax.experimental import pallas as pl
from jax.experimental.pallas import tpu as pltpu

D_MODEL = 1024
BATCH = 8
SEQ = 2048
DEPTH = 1
DEC_BATCH = 128
DEC_SEQ = 1
PAST_LEN = 16384
HG_HEADS = 8
HG_FDIM = 64
HG_IDIM = 64
HG_W = HG_HEADS * HG_FDIM
RET_HEADS = 4
RET_DK = 128
RET_DV = 128
RET_W = RET_HEADS * RET_DK
D_FF = 2816
CONV_W = 3
PLE_DIM = 256
CHUNK = 64
ROPE_BASE = 10000.0
EPS = 1e-6

C_HQ, C_HF, C_HI, C_HG = 0, 512, 1024, 1536
C_RQ, C_RK, C_RV, C_RG = 2048, 2560, 3072, 3584
C_GA, C_GB, C_END = 4096, 5120, 6144

N_LEVELS = 6
FAST_CHUNK = 128
FAST_DECAY_LIMIT = 80.0
HG_GROUP = 4
RET_GROUP = 2
MXU_W = 256
SUBLANES = 8
LANES = 128

MIX_TILE = 512
MIX_SUB = 256
FFN_TILE = 512
FFN_SUB = 512
CARRY_ROWS = 8
V7X_VMEM_LIMIT = 56 * 1024 * 1024

F32 = jnp.float32
BF16 = jnp.bfloat16


def _level_tables():
    c = CHUNK
    t = np.arange(c)[:, None]
    r = np.arange(c)[None, :]
    blocks = [r <= t, r > t]
    for l in range(N_LEVELS):
        same = (t >> l) == (r >> l)
        upper = ((t >> l) & 1) == 1
        blocks.append(np.where(upper, same & (r <= t), same & (r > t)))
    sel = np.concatenate(blocks, axis=0).astype(np.float32)
    sel2 = np.concatenate([sel, sel], axis=1)
    s = r
    level = np.full((c, c), N_LEVELS + 1, np.int32)
    level[np.broadcast_to(s == t, (c, c))] = N_LEVELS
    x = t ^ s
    for l in range(N_LEVELS):
        hit = (s < t) & ((x >> l) == 1)
        level[hit] = l
    return sel2, level


def _block_mask(rows, cols, rblk, cblk):
    r = np.arange(rows)[:, None] // rblk
    c = np.arange(cols)[None, :] // cblk
    return (r == c).astype(np.float32)


def _rope_tables(pos0, length):
    d = RET_DK
    inv_freq = ROPE_BASE ** (-np.arange(0, d, 2, dtype=np.float64) / d)
    pos = pos0 + np.arange(length, dtype=np.float64)
    ang = pos[:, None] * inv_freq[None, :]
    cos, sin = np.cos(ang), np.sin(ang)
    cos_full = np.tile(np.concatenate([cos, cos], axis=-1), (1, RET_HEADS))
    sin_signed = np.tile(np.concatenate([-sin, sin], axis=-1), (1, RET_HEADS))
    return jnp.asarray(cos_full, F32), jnp.asarray(sin_signed, F32)


def _log_gamma():
    return np.log1p(-np.exp2(-5.0 - np.arange(RET_HEADS, dtype=np.float32))).astype(np.float32)


def _retention_tables():
    c = MIX_SUB
    log_gamma = _log_gamma()
    t = np.arange(c, dtype=np.float32)
    delta = t[:, None] - t[None, :]
    dmask = np.where(delta >= 0, np.exp(np.maximum(delta, 0.0) * log_gamma[:, None, None]), 0.0)
    q_decay = np.exp((t + 1.0)[None, :] * log_gamma[:, None])
    k_decay = np.exp((c - 1.0 - t)[None, :] * log_gamma[:, None])
    qdec = np.repeat(q_decay.T, RET_DK, axis=1)
    kdec = np.repeat(k_decay.T, RET_DK, axis=1)
    chunk_decay = np.exp(np.float32(c) * log_gamma)
    cd = np.repeat(chunk_decay, RET_DK).reshape(RET_HEADS // RET_GROUP, MXU_W, 1)
    cd = np.broadcast_to(cd, (RET_HEADS // RET_GROUP, MXU_W, MXU_W))
    return tuple(jnp.asarray(a, F32) for a in (dmask, qdec, kdec, cd))


def _rms(x, g):
    ms = jnp.mean(x * x, axis=-1, keepdims=True)
    return x * lax.rsqrt(ms + EPS) * g


def _silu(x):
    return x * jax.nn.sigmoid(x)


def _dot(a, b):
    return jnp.dot(a, b, preferred_element_type=F32)


def _dot_nt(a, b):
    return lax.dot_general(a, b, (((1,), (1,)), ((), ())), preferred_element_type=F32)


def _dot_tn(a, b):
    return lax.dot_general(a, b, (((0,), (0,)), ((), ())), preferred_element_type=F32)


def _rotary(x, cos_full, sin_signed):
    swapped = jnp.concatenate(
        [pltpu.roll(x[:, RET_DK * h:RET_DK * (h + 1)], RET_DK // 2, 1) for h in range(RET_HEADS)], axis=1)
    return x * cos_full + swapped * sin_signed


def _hgrn_forget(hf, lb):
    f = lb + (1.0 - lb) * jax.nn.sigmoid(hf)
    return f, 1.0 - f


def _head_mean_square(o, head_w):
    parts = []
    for g in range(o.shape[1] // LANES):
        sq = o[:, LANES * g:LANES * (g + 1)]
        sq = sq * sq
        if head_w == LANES:
            ms = jnp.broadcast_to(jnp.sum(sq, axis=-1, keepdims=True), sq.shape)
        else:
            low = lax.broadcasted_iota(jnp.int32, sq.shape, 1) < head_w
            lo = jnp.sum(jnp.where(low, sq, 0.0), axis=-1, keepdims=True)
            hi = jnp.sum(jnp.where(low, 0.0, sq), axis=-1, keepdims=True)
            ms = jnp.where(low, lo, hi)
        parts.append(ms * (1.0 / head_w))
    return jnp.concatenate(parts, axis=1)


def _merge_branches(x, o_hg, o_ret, u_hg, u_rg, u_gab, g_hg, g_ret, w_br_hg, w_br_ret, w_out):
    a_hg = o_hg * lax.rsqrt(_head_mean_square(o_hg, HG_IDIM) + EPS) * g_hg * _silu(u_hg)
    y_hg = _dot(a_hg.astype(BF16), w_br_hg)
    a_ret = o_ret * lax.rsqrt(_head_mean_square(o_ret, RET_DV) + EPS) * g_ret * _silu(u_rg)
    y_ret = _dot(a_ret.astype(BF16), w_br_ret)
    mixed = jax.nn.sigmoid(u_gab[:, :D_MODEL]) * y_hg + jax.nn.sigmoid(u_gab[:, D_MODEL:]) * y_ret
    return x + _dot(mixed.astype(BF16), w_out)


def _ffn_tail(h, mid, p, w_down, g_ple, w_ple_gate, w_ple, g_final):
    h = h + _dot(mid, w_down)
    gate = jax.nn.sigmoid(_dot(_rms(h, g_ple).astype(BF16), w_ple_gate))
    h = h + gate * _dot(p.astype(BF16), w_ple)
    return _rms(h, g_final)


def _mix_prompt_kernel(x_ref, cos_ref, sin_ref, w_in_ref, g_mix_ref, lb_ref, sel_ref, tri_ref, hmask_ref, causal_ref, lvl_ref,
                       bd64_ref, bd64f_ref, bd128f_ref, dmask_ref, qdec_ref, kdec_ref, cd_ref,
                       g_hg_ref, g_ret_ref, w_br_hg_ref, w_br_ret_ref, w_out_ref,
                       h_ref, st_hg_out_ref, st_ret_out_ref,
                       xn_s, q_s, kk_s, lfh_s, lfl_s, v_s, vbd_s, qh_s, qi_s, kbd_s, ks_s, alast_s, o_hg_s, o_ret_s, st_hg, st_ret):
    t_idx = pl.program_id(1)
    n_hg_groups = HG_HEADS // HG_GROUP
    n_ret_groups = RET_HEADS // RET_GROUP

    @pl.when(t_idx == 0)
    def _():
        st_hg[...] = jnp.zeros_like(st_hg)
        st_ret[...] = jnp.zeros_like(st_ret)

    subs = [slice(i * MIX_SUB, (i + 1) * MIX_SUB) for i in range(MIX_TILE // MIX_SUB)]
    whole = [slice(0, MIX_TILE)]
    chunk_sums = []
    for r in whole:
        xn = _rms(x_ref[r, :], g_mix_ref[...]).astype(BF16)
        xn_s[r, :] = xn

        u = _dot(xn, w_in_ref[:, C_HQ:C_HG])
        f, kk = _hgrn_forget(u[:, HG_W:2 * HG_W], lb_ref[...])
        q_s[r, :] = _silu(u[:, 0:HG_W])
        kk_s[r, :] = kk
        lf = jnp.log(f)
        lf_hi = lf.astype(BF16)
        lfh_s[r, :] = lf_hi
        lfl_s[r, :] = (lf - lf_hi.astype(F32)).astype(BF16)
        v = u[:, 2 * HG_W:3 * HG_W]
        v_s[r, :] = v.astype(BF16)
        for h in range(HG_GROUP):
            vbd_s[h, r, :] = (v * hmask_ref[h:h + 1, :]).astype(BF16)
        chunk_sums += [jnp.sum(lf[c * CHUNK:(c + 1) * CHUNK, :], axis=0, keepdims=True)
                       for c in range(MIX_TILE // CHUNK)]

    mild = jnp.min(functools.reduce(jnp.minimum, chunk_sums)) >= -FAST_DECAY_LIMIT

    groups = [slice(MXU_W * g, MXU_W * (g + 1)) for g in range(n_hg_groups)]
    bd_tile = lambda ref, rows, cs: jnp.concatenate([ref[h, rows, cs] for h in range(HG_GROUP)], axis=0)

    def prepare_fast():
        per_sub = MIX_SUB // FAST_CHUNK
        rep = lambda rows: jnp.concatenate([jnp.broadcast_to(e, (FAST_CHUNK, HG_W)) for e in rows], axis=0)
        for i, r in enumerate(subs):
            lf2 = jnp.concatenate([lfh_s[r, :], lfl_s[r, :]], axis=0)
            cum = _dot(tri_ref[...], lf2)
            mids = [cum[c * FAST_CHUNK + FAST_CHUNK // 2 - 1:c * FAST_CHUNK + FAST_CHUNK // 2, :] for c in range(per_sub)]
            ends = [cum[(c + 1) * FAST_CHUNK - 1:(c + 1) * FAST_CHUNK, :] for c in range(per_sub)]
            mid = rep(mids)
            kk = kk_s[r, :]
            q = q_s[r, :]
            qh_s[r, :] = (q * jnp.exp(cum - mid)).astype(BF16)
            qi_s[r, :] = (q * jnp.exp(cum)).astype(BF16)
            k_mid = kk * jnp.exp(mid - cum)
            for h in range(HG_GROUP):
                kbd_s[h, r, :] = (k_mid * hmask_ref[h:h + 1, :]).astype(BF16)
            ks_s[r, :] = (kk * jnp.exp(rep(ends) - cum)).astype(BF16)
            for c, e in enumerate(ends):
                ci = i * per_sub + c
                alast_s[ci:ci + 1, :] = jnp.exp(e)

    def chunk_fast_products(c):
        rows = slice(c * FAST_CHUNK, (c + 1) * FAST_CHUNK)
        causal = causal_ref[...] != 0
        bd64f = bd64f_ref[...]
        scores = [jnp.where(causal, _dot_nt(qh_s[rows, cs], bd_tile(kbd_s, rows, cs)), 0.0).astype(BF16)
                  for cs in groups]
        upds = [_dot_tn(v_s[rows, cs], ks_s[rows, cs]) * bd64f for cs in groups]
        return scores, upds

    def chunk_fast(c, scores, upds):
        rows = slice(c * FAST_CHUNK, (c + 1) * FAST_CHUNK)
        o_parts = []
        for g, cs in enumerate(groups):
            st = st_hg[g]
            o_parts.append(_dot(scores[g], bd_tile(vbd_s, rows, cs)) + _dot_nt(qi_s[rows, cs], st.astype(BF16)))
            st_hg[g] = st * alast_s[c:c + 1, cs] + upds[g]
        o_hg_s[rows, :] = jnp.concatenate(o_parts, axis=1)

    def chunk_safe(c):
        rows = slice(c * CHUNK, (c + 1) * CHUNK)
        q = q_s[rows, :]
        kk = kk_s[rows, :]
        lf2 = jnp.concatenate([lfh_s[rows, :], lfl_s[rows, :]], axis=0)
        ex = jnp.exp(_dot(sel_ref[...], lf2))
        lvl = lvl_ref[...]
        bd64 = bd64_ref[...]
        bd64f = bd64f_ref[...]
        o_parts = []
        for g, cs in enumerate(groups):
            scores = jnp.zeros((CHUNK, MXU_W), F32)
            for l in range(N_LEVELS + 1):
                if l < N_LEVELS:
                    e = ex[CHUNK * (l + 2):CHUNK * (l + 3), cs]
                    ql = (q[:, cs] * e).astype(BF16)
                    kl = (kk[:, cs] * e).astype(BF16)
                else:
                    ql = q[:, cs].astype(BF16)
                    kl = kk[:, cs].astype(BF16)
                rhs = jnp.concatenate([kl] * HG_GROUP, axis=0) * bd64
                scores = jnp.where(lvl == l, _dot_nt(ql, rhs), scores)
            e_cum = ex[0:CHUNK, cs]
            e_suf = ex[CHUNK:2 * CHUNK, cs]
            st = st_hg[g]
            o_inter = _dot_nt((q[:, cs] * e_cum).astype(BF16), st.astype(BF16))
            o_parts.append(_dot(scores.astype(BF16), bd_tile(vbd_s, rows, cs)) + o_inter)
            upd = _dot_tn(v_s[rows, cs], (kk[:, cs] * e_suf).astype(BF16))
            st_hg[g] = st * e_cum[CHUNK - 1:CHUNK, :] + upd * bd64f
        o_hg_s[rows, :] = jnp.concatenate(o_parts, axis=1)

    def retention(r):
        xn = xn_s[r, :]
        u = _dot(xn, w_in_ref[:, C_RQ:C_RG])
        cos_full = cos_ref[r, :]
        sin_signed = sin_ref[r, :]
        qr = _rotary(u[:, 0:RET_W], cos_full, sin_signed)
        kr = _rotary(u[:, RET_W:2 * RET_W], cos_full, sin_signed) * (RET_DK ** -0.5)
        vr = u[:, 2 * RET_W:3 * RET_W].astype(BF16)
        qrb = qr.astype(BF16)
        krb = kr.astype(BF16)
        o_heads = []
        for h in range(RET_HEADS):
            hs = slice(RET_DK * h, RET_DK * (h + 1))
            scores = _dot_nt(qrb[:, hs], krb[:, hs]) * dmask_ref[h]
            o_heads.append(_dot(scores.astype(BF16), vr[:, hs]))
        qd = (qr * qdec_ref[...]).astype(BF16)
        kd = (kr * kdec_ref[...]).astype(BF16)
        bd128f = bd128f_ref[...]
        o_inter = []
        for g in range(n_ret_groups):
            cs = slice(MXU_W * g, MXU_W * (g + 1))
            st = st_ret[g]
            o_inter.append(_dot(qd[:, cs], st.astype(BF16)))
            st_ret[g] = cd_ref[g] * st + _dot_tn(kd[:, cs], vr[:, cs]) * bd128f
        o_ret_s[r, :] = jnp.concatenate(o_heads, axis=1) + jnp.concatenate(o_inter, axis=1)

    def recurrences(fast):
        if fast:
            prepare_fast()
        for r in subs:
            retention(r)
        if fast:
            products = [chunk_fast_products(c) for c in range(MIX_TILE // FAST_CHUNK)]
            for c, (scores, upds) in enumerate(products):
                chunk_fast(c, scores, upds)
        else:
            for c in range(MIX_TILE // CHUNK):
                chunk_safe(c)

    pl.when(mild)(functools.partial(recurrences, True))
    pl.when(jnp.logical_not(mild))(functools.partial(recurrences, False))

    for r in whole:
        xn = xn_s[r, :]
        h_ref[r, :] = _merge_branches(
            x_ref[r, :], o_hg_s[r, :], o_ret_s[r, :],
            _dot(xn, w_in_ref[:, C_HG:C_RQ]), _dot(xn, w_in_ref[:, C_RG:C_GA]), _dot(xn, w_in_ref[:, C_GA:C_END]),
            g_hg_ref[...], g_ret_ref[...],
            w_br_hg_ref[...], w_br_ret_ref[...], w_out_ref[...])

    @pl.when(t_idx == pl.num_programs(1) - 1)
    def _():
        for g in range(n_hg_groups):
            st_t = st_hg[g].T
            for h in range(HG_GROUP):
                blk = slice(HG_FDIM * h, HG_FDIM * (h + 1))
                st_hg_out_ref[HG_GROUP * g + h] = st_t[blk, blk]
        for g in range(n_ret_groups):
            for h in range(RET_GROUP):
                blk = slice(RET_DK * h, RET_DK * (h + 1))
                st_ret_out_ref[RET_GROUP * g + h] = st_ret[g, blk, blk]


def _const_spec(shape):
    zeros = (0,) * len(shape)
    return pl.BlockSpec(shape, lambda *_: zeros, pipeline_mode=pl.Buffered(1))


def _mix_prompt(x, cos_full, sin_signed, w_in, g_mix, lb, consts, g_hg, g_ret,
                w_br_hg, w_br_ret, w_out):
    b, l, d = x.shape
    nt = l // MIX_TILE
    tile = lambda w: pl.BlockSpec((None, MIX_TILE, w), lambda i, j: (i, j, 0))
    table = lambda w: pl.BlockSpec((MIX_TILE, w), lambda i, j: (j, 0))
    n_hg_groups = HG_HEADS // HG_GROUP
    n_ret_groups = RET_HEADS // RET_GROUP
    const_args = (w_in, g_mix, lb) + tuple(consts) + (g_hg, g_ret, w_br_hg, w_br_ret, w_out)
    state_spec = lambda h, dk, dv: pl.BlockSpec((None, None, h, dk, dv), lambda i, j: (0, i, 0, 0, 0))
    return pl.pallas_call(
        _mix_prompt_kernel,
        grid=(b, nt),
        in_specs=[tile(d), table(RET_W), table(RET_W)] + [_const_spec(a.shape) for a in const_args],
        out_specs=[tile(d), state_spec(HG_HEADS, HG_FDIM, HG_IDIM), state_spec(RET_HEADS, RET_DK, RET_DV)],
        out_shape=[jax.ShapeDtypeStruct((b, l, d), F32),
                   jax.ShapeDtypeStruct((1, b, HG_HEADS, HG_FDIM, HG_IDIM), F32),
                   jax.ShapeDtypeStruct((1, b, RET_HEADS, RET_DK, RET_DV), F32)],
        scratch_shapes=[
            pltpu.VMEM((MIX_TILE, d), BF16),
            pltpu.VMEM((MIX_TILE, HG_W), F32),
            pltpu.VMEM((MIX_TILE, HG_W), F32),
            pltpu.VMEM((MIX_TILE, HG_W), BF16),
            pltpu.VMEM((MIX_TILE, HG_W), BF16),
            pltpu.VMEM((MIX_TILE, HG_W), BF16),
            pltpu.VMEM((HG_GROUP, MIX_TILE, HG_W), BF16),
            pltpu.VMEM((MIX_TILE, HG_W), BF16),
            pltpu.VMEM((MIX_TILE, HG_W), BF16),
            pltpu.VMEM((HG_GROUP, MIX_TILE, HG_W), BF16),
            pltpu.VMEM((MIX_TILE, HG_W), BF16),
            pltpu.VMEM((max(MIX_TILE // CHUNK, SUBLANES), HG_W), F32),
            pltpu.VMEM((MIX_TILE, HG_W), F32),
            pltpu.VMEM((MIX_TILE, RET_W), F32),
            pltpu.VMEM((n_hg_groups, MXU_W, MXU_W), F32),
            pltpu.VMEM((n_ret_groups, MXU_W, MXU_W), F32),
        ],
        compiler_params=pltpu.CompilerParams(
            dimension_semantics=("arbitrary", "arbitrary"), vmem_limit_bytes=V7X_VMEM_LIMIT),
        name="mix_prompt",
    )(x, cos_full, sin_signed, *const_args)


def _ffn_prompt_kernel(h_ref, p_ref, g_ffn_ref, w_up_ref, conv_w_ref, conv_b_ref, w_down_ref,
                       g_ple_ref, w_ple_gate_ref, w_ple_ref, g_final_ref,
                       y_ref, conv_out_ref, a_ext):
    t_idx = pl.program_id(1)
    tl = FFN_TILE

    @pl.when(t_idx == 0)
    def _():
        a_ext[0:CARRY_ROWS, :] = jnp.zeros((CARRY_ROWS, D_FF), F32)

    @pl.when(t_idx > 0)
    def _():
        a_ext[0:CARRY_ROWS, :] = a_ext[tl:tl + CARRY_ROWS, :]

    subs = [slice(i * FFN_SUB, (i + 1) * FFN_SUB) for i in range(tl // FFN_SUB)]
    xns = []
    for r in subs:
        xns.append(_rms(h_ref[r, :], g_ffn_ref[...]).astype(BF16))
        a_ext[CARRY_ROWS + r.start:CARRY_ROWS + r.stop, :] = _dot(xns[-1], w_up_ref[:, 0:D_FF])
    cw = conv_w_ref[...]
    mids = []
    for r, xn in zip(subs, xns):
        ac = conv_b_ref[...]
        for j in range(CONV_W):
            off = CARRY_ROWS - (CONV_W - 1) + j
            ac = ac + a_ext[off + r.start:off + r.stop, :] * cw[j:j + 1, :]
        mids.append((jax.nn.gelu(ac) * _dot(xn, w_up_ref[:, D_FF:2 * D_FF])).astype(BF16))
    for r, mid in zip(subs, mids):
        y_ref[r, :] = _ffn_tail(h_ref[r, :], mid, p_ref[r, :], w_down_ref[...], g_ple_ref[...],
                                w_ple_gate_ref[...], w_ple_ref[...], g_final_ref[...])

    @pl.when(t_idx == pl.num_programs(1) - 1)
    def _():
        conv_out_ref[...] = a_ext[CARRY_ROWS + tl - (CONV_W - 1):CARRY_ROWS + tl, :]


def _ffn_prompt(h, p, g_ffn, w_up, conv_w, conv_b, w_down, g_ple, w_ple_gate, w_ple, g_final):
    b, l, d = h.shape
    nt = l // FFN_TILE
    tile = lambda w: pl.BlockSpec((None, FFN_TILE, w), lambda i, j: (i, j, 0))
    const_args = (g_ffn, w_up, conv_w, conv_b, w_down, g_ple, w_ple_gate, w_ple, g_final)
    return pl.pallas_call(
        _ffn_prompt_kernel,
        grid=(b, nt),
        in_specs=[tile(d), tile(PLE_DIM)] + [_const_spec(a.shape) for a in const_args],
        out_specs=[tile(d), pl.BlockSpec((None, CONV_W - 1, D_FF), lambda i, j: (i, 0, 0))],
        out_shape=[jax.ShapeDtypeStruct((b, l, d), F32),
                   jax.ShapeDtypeStruct((b, CONV_W - 1, D_FF), F32)],
        scratch_shapes=[pltpu.VMEM((FFN_TILE + CARRY_ROWS, D_FF), F32)],
        compiler_params=pltpu.CompilerParams(
            dimension_semantics=("arbitrary", "arbitrary"), vmem_limit_bytes=V7X_VMEM_LIMIT),
        name="ffn_prompt",
    )(h, p, *const_args)


def _proj_sample_kernel(x_ref, cos_ref, sin_ref, w_in_ref, g_mix_ref, lb_ref,
                        f_ref, kk_ref, q_ref, v_ref, kr_ref, qr_ref, vr_ref, gates_ref):
    xn = _rms(x_ref[...], g_mix_ref[...]).astype(BF16)
    u = _dot(xn, w_in_ref[:, C_HQ:C_HG])
    f, kk = _hgrn_forget(u[:, HG_W:2 * HG_W], lb_ref[...])
    f_ref[...] = f.T
    kk_ref[...] = kk.T
    q_ref[...] = _silu(u[:, 0:HG_W]).T
    v_ref[...] = u[:, 2 * HG_W:3 * HG_W].T
    u = _dot(xn, w_in_ref[:, C_RQ:C_RG])
    cos_full = cos_ref[...]
    sin_signed = sin_ref[...]
    qr_ref[...] = _rotary(u[:, 0:RET_W], cos_full, sin_signed)
    kr_ref[...] = _rotary(u[:, RET_W:2 * RET_W], cos_full, sin_signed) * (RET_DK ** -0.5)
    vr_ref[...] = u[:, 2 * RET_W:3 * RET_W]
    gates_ref[:, 0:HG_W] = _dot(xn, w_in_ref[:, C_HG:C_RQ])
    gates_ref[:, HG_W:HG_W + RET_W] = _dot(xn, w_in_ref[:, C_RG:C_GA])
    gates_ref[:, HG_W + RET_W:] = _dot(xn, w_in_ref[:, C_GA:C_END])


def _proj_sample(x, cos_full, sin_signed, w_in, g_mix, lb):
    n = x.shape[0]
    col = jax.ShapeDtypeStruct((HG_W, n), F32)
    row = jax.ShapeDtypeStruct((n, RET_W), F32)
    return pl.pallas_call(
        _proj_sample_kernel,
        out_shape=[col] * 4 + [row] * 3 + [jax.ShapeDtypeStruct((n, HG_W + RET_W + 2 * D_MODEL), F32)],
        compiler_params=pltpu.CompilerParams(vmem_limit_bytes=V7X_VMEM_LIMIT),
        name="proj_sample",
    )(x, cos_full, sin_signed, w_in, g_mix, lb)


def _state_tokens_minor_kernel(s_ref, a_t_ref, k_t_ref, q_t_ref, v_t_ref, s_out_ref, o_t_ref):
    dk, dv, _ = s_ref.shape
    row0 = pl.program_id(0) * dk
    v_blk = v_t_ref[...]

    def body(k, acc):
        s_new = a_t_ref[pl.ds(row0 + k, 1), :] * s_ref[k] + k_t_ref[pl.ds(row0 + k, 1), :] * v_blk
        s_out_ref[k] = s_new
        return acc + q_t_ref[pl.ds(row0 + k, 1), :] * s_new

    o_t_ref[...] = lax.fori_loop(0, dk, body, jnp.zeros(v_blk.shape, F32), unroll=SUBLANES)


def _state_update_tokens_minor(state_t, layer, a_t, k_t, q_t, v_t, *, name):
    _, heads, dk, dv, n = state_t.shape
    full = lambda a: pl.BlockSpec(a.shape, lambda h: (0, 0))
    per_head = pl.BlockSpec((dv, n), lambda h: (h, 0))
    slab = lambda lead: pl.BlockSpec((None, None, dk, dv, n), lambda h: (lead, h, 0, 0, 0))
    return pl.pallas_call(
        _state_tokens_minor_kernel,
        grid=(heads,),
        in_specs=[slab(layer), full(a_t), full(k_t), full(q_t), per_head],
        out_specs=[slab(0), per_head],
        out_shape=[jax.ShapeDtypeStruct((1,) + state_t.shape[1:], F32), jax.ShapeDtypeStruct(v_t.shape, F32)],
        compiler_params=pltpu.CompilerParams(
            dimension_semantics=("arbitrary",), vmem_limit_bytes=V7X_VMEM_LIMIT),
        name=name,
    )(state_t, a_t, k_t, q_t, v_t)


def _state_kernel(s_ref, a_ref, k_ref, q_ref, v_ref, s_out_ref, o_ref, *, dk, dv, kb):
    j = pl.program_id(0)
    n = s_ref.shape[0]

    @pl.when(j == 0)
    def _():
        o_ref[...] = jnp.zeros_like(o_ref)

    group0 = j * (kb // SUBLANES)
    head_row = pl.multiple_of((j * kb // dk) * dv, dv)
    v_blk = v_ref[pl.ds(head_row, dv), :]
    acc = jnp.zeros(v_blk.shape, F32)
    for g in range(kb // SUBLANES):
        rows = slice(SUBLANES * g, SUBLANES * (g + 1))
        s_t = s_ref[:, rows, :].reshape(n * SUBLANES, dv).T
        s_new = a_ref[pl.ds(group0 + g, 1), :] * s_t + k_ref[pl.ds(group0 + g, 1), :] * v_blk
        s_out_ref[:, rows, :] = s_new.T.reshape(n, SUBLANES, dv)
        acc = acc + q_ref[pl.ds(group0 + g, 1), :] * s_new
    o_ref[pl.ds(head_row, dv), :] += acc


def _state_update(state, layer, a_il, k_il, q_il, v_rep, *, kb, name):
    _, n, heads, dk, dv = state.shape
    per_head = dk // kb
    full = lambda a: pl.BlockSpec(a.shape, lambda j: (0, 0))
    slab = lambda lead: pl.BlockSpec((None, n, None, kb, dv), lambda j: (lead, 0, j // per_head, j % per_head, 0))
    return pl.pallas_call(
        functools.partial(_state_kernel, dk=dk, dv=dv, kb=kb),
        grid=(heads * per_head,),
        in_specs=[slab(layer), full(a_il), full(k_il), full(q_il), full(v_rep)],
        out_specs=[slab(0), full(v_rep)],
        out_shape=[jax.ShapeDtypeStruct((1,) + state.shape[1:], F32), jax.ShapeDtypeStruct(v_rep.shape, F32)],
        compiler_params=pltpu.CompilerParams(
            dimension_semantics=("arbitrary",), vmem_limit_bytes=V7X_VMEM_LIMIT),
        name=name,
    )(state, a_il, k_il, q_il, v_rep)


def _lane_group_sum(o_ref, sel):
    o = o_ref[...]
    hi = o.astype(BF16)
    lo = (o - hi.astype(F32)).astype(BF16)
    return _dot_nt(sel, hi) + _dot_nt(sel, lo)


def _tail_sample_kernel(x_ref, o_hg_t_ref, o_ret_ref, sel_ref, gates_ref, conv0_ref, p_ref,
                        g_hg_ref, g_ret_ref, w_br_hg_ref, w_br_ret_ref, w_out_ref,
                        g_ffn_ref, w_up_ref, conv_w_ref, conv_b_ref, w_down_ref,
                        g_ple_ref, w_ple_gate_ref, w_ple_ref, g_final_ref,
                        y_ref, conv_out_ref):
    h = _merge_branches(
        x_ref[...], o_hg_t_ref[...].T, _lane_group_sum(o_ret_ref, sel_ref[...]),
        gates_ref[:, 0:HG_W], gates_ref[:, HG_W:HG_W + RET_W], gates_ref[:, HG_W + RET_W:],
        g_hg_ref[...], g_ret_ref[...],
        w_br_hg_ref[...], w_br_ret_ref[...], w_out_ref[...])
    xn = _rms(h, g_ffn_ref[...]).astype(BF16)
    a = _dot(xn, w_up_ref[:, 0:D_FF])
    gate_half = _dot(xn, w_up_ref[:, D_FF:2 * D_FF])
    cw = conv_w_ref[...]
    prev = conv0_ref[:, 1, :]
    ac = conv_b_ref[...] + conv0_ref[:, 0, :] * cw[0:1, :] + prev * cw[1:2, :] + a * cw[2:3, :]
    conv_out_ref[0, :, 0, :] = prev
    conv_out_ref[0, :, 1, :] = a
    y_ref[...] = _ffn_tail(h, (jax.nn.gelu(ac) * gate_half).astype(BF16), p_ref[...], w_down_ref[...], g_ple_ref[...],
                           w_ple_gate_ref[...], w_ple_ref[...], g_final_ref[...])


def _tail_sample(x, o_hg_t, o_ret, sel, gates, conv0, p, *weights):
    n = x.shape[0]
    return pl.pallas_call(
        _tail_sample_kernel,
        out_shape=[jax.ShapeDtypeStruct((n, D_MODEL), F32),
                   jax.ShapeDtypeStruct((1, n, CONV_W - 1, D_FF), F32)],
        compiler_params=pltpu.CompilerParams(vmem_limit_bytes=V7X_VMEM_LIMIT),
        name="tail_sample",
    )(x, o_hg_t, o_ret, sel, gates, conv0, p, *weights)


def kernel(x_prompt, x_sample, state_hgrn, state_ret, state_conv, p_prompt, p_sample, lb_logits, w_in, g_mix, g_hg_out, g_ret_out, w_br_hg, w_br_ret, w_out, g_ffn, w_up, conv_w, conv_b, w_down, g_ple, w_ple, w_ple_gate, g_final):
    assert DEPTH == 1 and x_prompt.shape == (BATCH, SEQ, D_MODEL) and x_sample.shape == (DEC_BATCH, DEC_SEQ, D_MODEL)
    i = 0
    row = lambda a: a.reshape(1, -1).astype(F32)
    lb = jnp.cumsum(jax.nn.softmax(lb_logits.astype(F32), axis=0), axis=0)[i].reshape(1, HG_W)

    sel2, level = _level_tables()
    tri = np.tril(_block_mask(MIX_SUB, MIX_SUB, FAST_CHUNK, FAST_CHUNK))
    causal = np.tile(np.tril(np.ones((FAST_CHUNK, FAST_CHUNK), np.int32)), (1, HG_GROUP))
    hmask = (np.arange(HG_W)[None, :] // HG_FDIM) % HG_GROUP == np.arange(HG_GROUP)[:, None]
    consts = (
        jnp.asarray(sel2, BF16),
        jnp.asarray(np.concatenate([tri, tri], axis=1), BF16),
        jnp.asarray(hmask, F32),
        jnp.asarray(causal),
        jnp.asarray(np.tile(level, (1, HG_GROUP))),
        jnp.asarray(_block_mask(MXU_W, MXU_W, CHUNK, HG_FDIM), BF16),
        jnp.asarray(_block_mask(MXU_W, MXU_W, HG_IDIM, HG_FDIM), F32),
        jnp.asarray(_block_mask(MXU_W, MXU_W, RET_DK, RET_DV), F32),
    ) + _retention_tables()

    w_in_b = w_in[i].astype(BF16)
    mix_w = (row(g_hg_out[i]), row(g_ret_out[i]),
             w_br_hg[i].astype(BF16), w_br_ret[i].astype(BF16), w_out[i].astype(BF16))
    ffn_w = (row(g_ffn[i]), w_up[i].astype(BF16), conv_w[i].astype(F32), row(conv_b[i]), w_down[i].astype(BF16),
             row(g_ple[i]), w_ple_gate[i].astype(BF16), w_ple[i].astype(BF16), row(g_final))

    cos_p, sin_p = _rope_tables(0.0, SEQ)
    h_p, hg_state_p, ret_state_p = _mix_prompt(x_prompt, cos_p, sin_p, w_in_b, row(g_mix[i]), lb, consts, *mix_w)
    y_prompt, conv_p = _ffn_prompt(h_p, p_prompt[i], *ffn_w)

    n = DEC_BATCH
    xs = x_sample.reshape(n, D_MODEL)
    cos_s, sin_s = _rope_tables(float(PAST_LEN), DEC_SEQ)
    f_t, kk_t, q_t, v_t, kr, qr, vr, gates = _proj_sample(xs, cos_s, sin_s, w_in_b, row(g_mix[i]), lb)
    hg_new_t, o_hg_t = _state_update_tokens_minor(
        jnp.transpose(state_hgrn, (0, 2, 3, 4, 1)), i, f_t, kk_t, q_t, v_t, name="state_hgrn")
    hg_new = jnp.transpose(hg_new_t, (0, 4, 1, 2, 3))
    interleave = lambda a: a.reshape(n, -1, SUBLANES).transpose(1, 0, 2).reshape(-1, n * SUBLANES)
    repeat = lambda a: jnp.repeat(a.T, SUBLANES, axis=1)
    gamma = jnp.asarray(np.broadcast_to(np.repeat(np.exp(_log_gamma()), RET_DK // SUBLANES)[:, None],
                                        (RET_W // SUBLANES, n * SUBLANES)))
    ret_new, o_ret = _state_update(
        state_ret, i, gamma, interleave(kr), interleave(qr), repeat(vr), kb=RET_DK // 2, name="state_ret")
    sel = jnp.asarray(_block_mask(n, n * SUBLANES, 1, SUBLANES), BF16)
    y_sample, conv_s = _tail_sample(
        xs, o_hg_t, o_ret, sel, gates, state_conv[i], p_sample[i].reshape(n, PLE_DIM),
        *mix_w, *ffn_w)

    return (y_prompt, y_sample.reshape(n, DEC_SEQ, D_MODEL),
            hg_state_p, ret_state_p, conv_p[None],
            hg_new, ret_new,
            conv_s)
```

```python
import functools

import numpy as np
import jax
import jax.numpy as jnp
from jax import lax
from jax.experimental import pallas as pl
from jax.experimental.pallas import tpu as pltpu

D_MODEL = 1024
BATCH = 8
SEQ = 2048
DEPTH = 1
DEC_BATCH = 128
DEC_SEQ = 1
PAST_LEN = 16384
HG_HEADS = 8
HG_FDIM = 64
HG_IDIM = 64
HG_W = HG_HEADS * HG_FDIM
RET_HEADS = 4
RET_DK = 128
RET_DV = 128
RET_W = RET_HEADS * RET_DK
D_FF = 2816
CONV_W = 3
PLE_DIM = 256
CHUNK = 64
ROPE_BASE = 10000.0
EPS = 1e-6

C_HQ, C_HF, C_HI, C_HG = 0, 512, 1024, 1536
C_RQ, C_RK, C_RV, C_RG = 2048, 2560, 3072, 3584
C_GA, C_GB, C_END = 4096, 5120, 6144

N_LEVELS = 6
FAST_CHUNK = 128
FAST_DECAY_LIMIT = 80.0
HG_GROUP = 4
RET_GROUP = 2
MXU_W = 256
SUBLANES = 8
LANES = 128

MIX_TILE = 512
MIX_SUB = 256
FFN_TILE = 512
FFN_SUB = 512
CARRY_ROWS = 8
MIB = 1024 * 1024
MIX_VMEM_LIMIT = 48 * MIB
FFN_VMEM_LIMIT = 48 * MIB
SAMPLE_VMEM_LIMIT = 32 * MIB

F32 = jnp.float32
BF16 = jnp.bfloat16


def _level_tables():
    c = CHUNK
    t = np.arange(c)[:, None]
    r = np.arange(c)[None, :]
    blocks = [r <= t, r > t]
    for l in range(N_LEVELS):
        same = (t >> l) == (r >> l)
        upper = ((t >> l) & 1) == 1
        blocks.append(np.where(upper, same & (r <= t), same & (r > t)))
    sel = np.concatenate(blocks, axis=0).astype(np.float32)
    sel2 = np.concatenate([sel, sel], axis=1)
    s = r
    level = np.full((c, c), N_LEVELS + 1, np.int32)
    level[np.broadcast_to(s == t, (c, c))] = N_LEVELS
    x = t ^ s
    for l in range(N_LEVELS):
        hit = (s < t) & ((x >> l) == 1)
        level[hit] = l
    return sel2, level


def _block_mask(rows, cols, rblk, cblk):
    r = np.arange(rows)[:, None] // rblk
    c = np.arange(cols)[None, :] // cblk
    return (r == c).astype(np.float32)


def _rope_tables(pos0, length):
    d = RET_DK
    inv_freq = ROPE_BASE ** (-np.arange(0, d, 2, dtype=np.float64) / d)
    pos = pos0 + np.arange(length, dtype=np.float64)
    ang = pos[:, None] * inv_freq[None, :]
    cos, sin = np.cos(ang), np.sin(ang)
    cos_full = np.tile(np.concatenate([cos, cos], axis=-1), (1, RET_HEADS))
    sin_signed = np.tile(np.concatenate([-sin, sin], axis=-1), (1, RET_HEADS))
    return jnp.asarray(cos_full, F32), jnp.asarray(sin_signed, F32)


def _log_gamma():
    return np.log1p(-np.exp2(-5.0 - np.arange(RET_HEADS, dtype=np.float32))).astype(np.float32)


def _retention_tables():
    c = MIX_SUB
    log_gamma = _log_gamma()
    t = np.arange(c, dtype=np.float32)
    delta = t[:, None] - t[None, :]
    dmask = np.where(delta >= 0, np.exp(np.maximum(delta, 0.0) * log_gamma[:, None, None]), 0.0)
    q_decay = np.exp((t + 1.0)[None, :] * log_gamma[:, None])
    k_decay = np.exp((c - 1.0 - t)[None, :] * log_gamma[:, None])
    qdec = np.repeat(q_decay.T, RET_DK, axis=1)
    kdec = np.repeat(k_decay.T, RET_DK, axis=1)
    chunk_decay = np.exp(np.float32(c) * log_gamma)
    cd = np.repeat(chunk_decay, RET_DK).reshape(RET_HEADS // RET_GROUP, MXU_W, 1)
    cd = np.broadcast_to(cd, (RET_HEADS // RET_GROUP, MXU_W, MXU_W))
    return tuple(jnp.asarray(a, F32) for a in (dmask, qdec, kdec, cd))


def _rms(x, g):
    ms = jnp.mean(x * x, axis=-1, keepdims=True)
    return x * lax.rsqrt(ms + EPS) * g


def _silu(x):
    return x * jax.nn.sigmoid(x)


def _dot(a, b):
    return jnp.dot(a, b, preferred_element_type=F32)


def _dot_nt(a, b):
    return lax.dot_general(a, b, (((1,), (1,)), ((), ())), preferred_element_type=F32)


def _dot_tn(a, b):
    return lax.dot_general(a, b, (((0,), (0,)), ((), ())), preferred_element_type=F32)


def _rotary(x, cos_full, sin_signed):
    swapped = jnp.concatenate(
        [pltpu.roll(x[:, RET_DK * h:RET_DK * (h + 1)], RET_DK // 2, 1) for h in range(RET_HEADS)], axis=1)
    return x * cos_full + swapped * sin_signed


def _hgrn_forget(hf, lb):
    f = lb + (1.0 - lb) * jax.nn.sigmoid(hf)
    return f, 1.0 - f


def _head_mean_square(o, head_w):
    parts = []
    for g in range(o.shape[1] // LANES):
        sq = o[:, LANES * g:LANES * (g + 1)]
        sq = sq * sq
        if head_w == LANES:
            ms = jnp.broadcast_to(jnp.sum(sq, axis=-1, keepdims=True), sq.shape)
        else:
            low = lax.broadcasted_iota(jnp.int32, sq.shape, 1) < head_w
            lo = jnp.sum(jnp.where(low, sq, 0.0), axis=-1, keepdims=True)
            hi = jnp.sum(jnp.where(low, 0.0, sq), axis=-1, keepdims=True)
            ms = jnp.where(low, lo, hi)
        parts.append(ms * (1.0 / head_w))
    return jnp.concatenate(parts, axis=1)


def _merge_branches(x, o_hg, o_ret, u_hg, u_rg, u_gab, g_hg, g_ret, w_br_hg, w_br_ret, w_out):
    a_hg = o_hg * lax.rsqrt(_head_mean_square(o_hg, HG_IDIM) + EPS) * g_hg * _silu(u_hg)
    y_hg = _dot(a_hg.astype(BF16), w_br_hg)
    a_ret = o_ret * lax.rsqrt(_head_mean_square(o_ret, RET_DV) + EPS) * g_ret * _silu(u_rg)
    y_ret = _dot(a_ret.astype(BF16), w_br_ret)
    mixed = jax.nn.sigmoid(u_gab[:, :D_MODEL]) * y_hg + jax.nn.sigmoid(u_gab[:, D_MODEL:]) * y_ret
    return x + _dot(mixed.astype(BF16), w_out)


def _ffn_tail(h, mid, p, w_down, g_ple, w_ple_gate, w_ple, g_final):
    h = h + _dot(mid, w_down)
    gate = jax.nn.sigmoid(_dot(_rms(h, g_ple).astype(BF16), w_ple_gate))
    h = h + gate * _dot(p.astype(BF16), w_ple)
    return _rms(h, g_final)


def _mix_prompt_kernel(x_ref, cos_ref, sin_ref, w_in_ref, g_mix_ref, lb_ref, sel_ref, tri_ref, hmask_ref, causal_ref, lvl_ref,
                       bd64_ref, bd64f_ref, bd128f_ref, dmask_ref, qdec_ref, kdec_ref, cd_ref,
                       g_hg_ref, g_ret_ref, w_br_hg_ref, w_br_ret_ref, w_out_ref,
                       h_ref, st_hg_out_ref, st_ret_out_ref,
                       xn_s, q_s, kk_s, lfh_s, lfl_s, v_s, vbd_s, qh_s, qi_s, kbd_s, ks_s, alast_s, o_hg_s, o_ret_s, st_hg, st_ret):
    t_idx = pl.program_id(1)
    n_hg_groups = HG_HEADS // HG_GROUP
    n_ret_groups = RET_HEADS // RET_GROUP

    @pl.when(t_idx == 0)
    def _():
        st_hg[...] = jnp.zeros_like(st_hg)
        st_ret[...] = jnp.zeros_like(st_ret)

    subs = [slice(i * MIX_SUB, (i + 1) * MIX_SUB) for i in range(MIX_TILE // MIX_SUB)]
    whole = [slice(0, MIX_TILE)]
    chunk_sums = []
    for r in whole:
        xn = _rms(x_ref[r, :], g_mix_ref[...]).astype(BF16)
        xn_s[r, :] = xn

        u = _dot(xn, w_in_ref[:, C_HQ:C_HG])
        f, kk = _hgrn_forget(u[:, HG_W:2 * HG_W], lb_ref[...])
        q_s[r, :] = _silu(u[:, 0:HG_W])
        kk_s[r, :] = kk
        lf = jnp.log(f)
        lf_hi = lf.astype(BF16)
        lfh_s[r, :] = lf_hi
        lfl_s[r, :] = (lf - lf_hi.astype(F32)).astype(BF16)
        v = u[:, 2 * HG_W:3 * HG_W]
        v_s[r, :] = v.astype(BF16)
        for h in range(HG_GROUP):
            vbd_s[h, r, :] = (v * hmask_ref[h:h + 1, :]).astype(BF16)
        chunk_sums += [jnp.sum(lf[c * CHUNK:(c + 1) * CHUNK, :], axis=0, keepdims=True)
                       for c in range(MIX_TILE // CHUNK)]

    mild = jnp.min(functools.reduce(jnp.minimum, chunk_sums)) >= -FAST_DECAY_LIMIT

    groups = [slice(MXU_W * g, MXU_W * (g + 1)) for g in range(n_hg_groups)]
    bd_tile = lambda ref, rows, cs: jnp.concatenate([ref[h, rows, cs] for h in range(HG_GROUP)], axis=0)

    def prepare_fast():
        per_sub = MIX_SUB // FAST_CHUNK
        rep = lambda rows: jnp.concatenate([jnp.broadcast_to(e, (FAST_CHUNK, HG_W)) for e in rows], axis=0)
        for i, r in enumerate(subs):
            lf2 = jnp.concatenate([lfh_s[r, :], lfl_s[r, :]], axis=0)
            cum = _dot(tri_ref[...], lf2)
            mids = [cum[c * FAST_CHUNK + FAST_CHUNK // 2 - 1:c * FAST_CHUNK + FAST_CHUNK // 2, :] for c in range(per_sub)]
            ends = [cum[(c + 1) * FAST_CHUNK - 1:(c + 1) * FAST_CHUNK, :] for c in range(per_sub)]
            mid = rep(mids)
            kk = kk_s[r, :]
            q = q_s[r, :]
            qh_s[r, :] = (q * jnp.exp(cum - mid)).astype(BF16)
            qi_s[r, :] = (q * jnp.exp(cum)).astype(BF16)
            k_mid = kk * jnp.exp(mid - cum)
            for h in range(HG_GROUP):
                kbd_s[h, r, :] = (k_mid * hmask_ref[h:h + 1, :]).astype(BF16)
            ks_s[r, :] = (kk * jnp.exp(rep(ends) - cum)).astype(BF16)
            for c, e in enumerate(ends):
                ci = i * per_sub + c
                alast_s[ci:ci + 1, :] = jnp.exp(e)

    def chunk_fast_products(c):
        rows = slice(c * FAST_CHUNK, (c + 1) * FAST_CHUNK)
        causal = causal_ref[...] != 0
        bd64f = bd64f_ref[...]
        scores = [jnp.where(causal, _dot_nt(qh_s[rows, cs], bd_tile(kbd_s, rows, cs)), 0.0).astype(BF16)
                  for cs in groups]
        upds = [_dot_tn(v_s[rows, cs], ks_s[rows, cs]) * bd64f for cs in groups]
        return scores, upds

    def chunk_fast(c, scores, upds):
        rows = slice(c * FAST_CHUNK, (c + 1) * FAST_CHUNK)
        o_parts = []
        for g, cs in enumerate(groups):
            st = st_hg[g]
            o_parts.append(_dot(scores[g], bd_tile(vbd_s, rows, cs)) + _dot_nt(qi_s[rows, cs], st.astype(BF16)))
            st_hg[g] = st * alast_s[c:c + 1, cs] + upds[g]
        o_hg_s[rows, :] = jnp.concatenate(o_parts, axis=1)

    def chunk_safe(c):
        rows = slice(c * CHUNK, (c + 1) * CHUNK)
        q = q_s[rows, :]
        kk = kk_s[rows, :]
        lf2 = jnp.concatenate([lfh_s[rows, :], lfl_s[rows, :]], axis=0)
        ex = jnp.exp(_dot(sel_ref[...], lf2))
        lvl = lvl_ref[...]
        bd64 = bd64_ref[...]
        bd64f = bd64f_ref[...]
        o_parts = []
        for g, cs in enumerate(groups):
            scores = jnp.zeros((CHUNK, MXU_W), F32)
            for l in range(N_LEVELS + 1):
                if l < N_LEVELS:
                    e = ex[CHUNK * (l + 2):CHUNK * (l + 3), cs]
                    ql = (q[:, cs] * e).astype(BF16)
                    kl = (kk[:, cs] * e).astype(BF16)
                else:
                    ql = q[:, cs].astype(BF16)
                    kl = kk[:, cs].astype(BF16)
                rhs = jnp.concatenate([kl] * HG_GROUP, axis=0) * bd64
                scores = jnp.where(lvl == l, _dot_nt(ql, rhs), scores)
            e_cum = ex[0:CHUNK, cs]
            e_suf = ex[CHUNK:2 * CHUNK, cs]
            st = st_hg[g]
            o_inter = _dot_nt((q[:, cs] * e_cum).astype(BF16), st.astype(BF16))
            o_parts.append(_dot(scores.astype(BF16), bd_tile(vbd_s, rows, cs)) + o_inter)
            upd = _dot_tn(v_s[rows, cs], (kk[:, cs] * e_suf).astype(BF16))
            st_hg[g] = st * e_cum[CHUNK - 1:CHUNK, :] + upd * bd64f
        o_hg_s[rows, :] = jnp.concatenate(o_parts, axis=1)

    def retention(r):
        xn = xn_s[r, :]
        u = _dot(xn, w_in_ref[:, C_RQ:C_RG])
        cos_full = cos_ref[r, :]
        sin_signed = sin_ref[r, :]
        qr = _rotary(u[:, 0:RET_W], cos_full, sin_signed)
        kr = _rotary(u[:, RET_W:2 * RET_W], cos_full, sin_signed) * (RET_DK ** -0.5)
        vr = u[:, 2 * RET_W:3 * RET_W].astype(BF16)
        qrb = qr.astype(BF16)
        krb = kr.astype(BF16)
        o_heads = []
        for h in range(RET_HEADS):
            hs = slice(RET_DK * h, RET_DK * (h + 1))
            scores = _dot_nt(qrb[:, hs], krb[:, hs]) * dmask_ref[h]
            o_heads.append(_dot(scores.astype(BF16), vr[:, hs]))
        qd = (qr * qdec_ref[...]).astype(BF16)
        kd = (kr * kdec_ref[...]).astype(BF16)
        bd128f = bd128f_ref[...]
        o_inter = []
        for g in range(n_ret_groups):
            cs = slice(MXU_W * g, MXU_W * (g + 1))
            st = st_ret[g]
            o_inter.append(_dot(qd[:, cs], st.astype(BF16)))
            st_ret[g] = cd_ref[g] * st + _dot_tn(kd[:, cs], vr[:, cs]) * bd128f
        o_ret_s[r, :] = jnp.concatenate(o_heads, axis=1) + jnp.concatenate(o_inter, axis=1)

    def recurrences(fast):
        if fast:
            prepare_fast()
        for r in subs:
            retention(r)
        if fast:
            products = [chunk_fast_products(c) for c in range(MIX_TILE // FAST_CHUNK)]
            for c, (scores, upds) in enumerate(products):
                chunk_fast(c, scores, upds)
        else:
            for c in range(MIX_TILE // CHUNK):
                chunk_safe(c)

    pl.when(mild)(functools.partial(recurrences, True))
    pl.when(jnp.logical_not(mild))(functools.partial(recurrences, False))

    for r in whole:
        xn = xn_s[r, :]
        h_ref[r, :] = _merge_branches(
            x_ref[r, :], o_hg_s[r, :], o_ret_s[r, :],
            _dot(xn, w_in_ref[:, C_HG:C_RQ]), _dot(xn, w_in_ref[:, C_RG:C_GA]), _dot(xn, w_in_ref[:, C_GA:C_END]),
            g_hg_ref[...], g_ret_ref[...],
            w_br_hg_ref[...], w_br_ret_ref[...], w_out_ref[...])

    @pl.when(t_idx == pl.num_programs(1) - 1)
    def _():
        for g in range(n_hg_groups):
            st_t = st_hg[g].T
            for h in range(HG_GROUP):
                blk = slice(HG_FDIM * h, HG_FDIM * (h + 1))
                st_hg_out_ref[HG_GROUP * g + h] = st_t[blk, blk]
        for g in range(n_ret_groups):
            for h in range(RET_GROUP):
                blk = slice(RET_DK * h, RET_DK * (h + 1))
                st_ret_out_ref[RET_GROUP * g + h] = st_ret[g, blk, blk]


def _const_spec(shape):
    zeros = (0,) * len(shape)
    return pl.BlockSpec(shape, lambda *_: zeros, pipeline_mode=pl.Buffered(1))


def _mix_prompt(x, cos_full, sin_signed, w_in, g_mix, lb, consts, g_hg, g_ret,
                w_br_hg, w_br_ret, w_out):
    b, l, d = x.shape
    nt = l // MIX_TILE
    tile = lambda w: pl.BlockSpec((None, MIX_TILE, w), lambda i, j: (i, j, 0))
    table = lambda w: pl.BlockSpec((MIX_TILE, w), lambda i, j: (j, 0))
    n_hg_groups = HG_HEADS // HG_GROUP
    n_ret_groups = RET_HEADS // RET_GROUP
    const_args = (w_in, g_mix, lb) + tuple(consts) + (g_hg, g_ret, w_br_hg, w_br_ret, w_out)
    state_spec = lambda h, dk, dv: pl.BlockSpec((None, None, h, dk, dv), lambda i, j: (0, i, 0, 0, 0))
    return pl.pallas_call(
        _mix_prompt_kernel,
        grid=(b, nt),
        in_specs=[tile(d), table(RET_W), table(RET_W)] + [_const_spec(a.shape) for a in const_args],
        out_specs=[tile(d), state_spec(HG_HEADS, HG_FDIM, HG_IDIM), state_spec(RET_HEADS, RET_DK, RET_DV)],
        out_shape=[jax.ShapeDtypeStruct((b, l, d), F32),
                   jax.ShapeDtypeStruct((1, b, HG_HEADS, HG_FDIM, HG_IDIM), F32),
                   jax.ShapeDtypeStruct((1, b, RET_HEADS, RET_DK, RET_DV), F32)],
        scratch_shapes=[
            pltpu.VMEM((MIX_TILE, d), BF16),
            pltpu.VMEM((MIX_TILE, HG_W), F32),
            pltpu.VMEM((MIX_TILE, HG_W), F32),
            pltpu.VMEM((MIX_TILE, HG_W), BF16),
            pltpu.VMEM((MIX_TILE, HG_W), BF16),
            pltpu.VMEM((MIX_TILE, HG_W), BF16),
            pltpu.VMEM((HG_GROUP, MIX_TILE, HG_W), BF16),
            pltpu.VMEM((MIX_TILE, HG_W), BF16),
            pltpu.VMEM((MIX_TILE, HG_W), BF16),
            pltpu.VMEM((HG_GROUP, MIX_TILE, HG_W), BF16),
            pltpu.VMEM((MIX_TILE, HG_W), BF16),
            pltpu.VMEM((max(MIX_TILE // CHUNK, SUBLANES), HG_W), F32),
            pltpu.VMEM((MIX_TILE, HG_W), F32),
            pltpu.VMEM((MIX_TILE, RET_W), F32),
            pltpu.VMEM((n_hg_groups, MXU_W, MXU_W), F32),
            pltpu.VMEM((n_ret_groups, MXU_W, MXU_W), F32),
        ],
        compiler_params=pltpu.CompilerParams(
            dimension_semantics=("arbitrary", "arbitrary"), vmem_limit_bytes=MIX_VMEM_LIMIT),
        name="mix_prompt",
    )(x, cos_full, sin_signed, *const_args)


def _ffn_prompt_kernel(h_ref, p_ref, g_ffn_ref, w_up_ref, conv_w_ref, conv_b_ref, w_down_ref,
                       g_ple_ref, w_ple_gate_ref, w_ple_ref, g_final_ref,
                       y_ref, conv_out_ref, a_ext):
    t_idx = pl.program_id(1)
    tl = FFN_TILE

    @pl.when(t_idx == 0)
    def _():
        a_ext[0:CARRY_ROWS, :] = jnp.zeros((CARRY_ROWS, D_FF), F32)

    @pl.when(t_idx > 0)
    def _():
        a_ext[0:CARRY_ROWS, :] = a_ext[tl:tl + CARRY_ROWS, :]

    subs = [slice(i * FFN_SUB, (i + 1) * FFN_SUB) for i in range(tl // FFN_SUB)]
    xns = []
    for r in subs:
        xns.append(_rms(h_ref[r, :], g_ffn_ref[...]).astype(BF16))
        a_ext[CARRY_ROWS + r.start:CARRY_ROWS + r.stop, :] = _dot(xns[-1], w_up_ref[:, 0:D_FF])
    cw = conv_w_ref[...]
    mids = []
    for r, xn in zip(subs, xns):
        ac = conv_b_ref[...]
        for j in range(CONV_W):
            off = CARRY_ROWS - (CONV_W - 1) + j
            ac = ac + a_ext[off + r.start:off + r.stop, :] * cw[j:j + 1, :]
        mids.append((jax.nn.gelu(ac) * _dot(xn, w_up_ref[:, D_FF:2 * D_FF])).astype(BF16))
    for r, mid in zip(subs, mids):
        y_ref[r, :] = _ffn_tail(h_ref[r, :], mid, p_ref[r, :], w_down_ref[...], g_ple_ref[...],
                                w_ple_gate_ref[...], w_ple_ref[...], g_final_ref[...])

    @pl.when(t_idx == pl.num_programs(1) - 1)
    def _():
        conv_out_ref[...] = a_ext[CARRY_ROWS + tl - (CONV_W - 1):CARRY_ROWS + tl, :]


def _ffn_prompt(h, p, g_ffn, w_up, conv_w, conv_b, w_down, g_ple, w_ple_gate, w_ple, g_final):
    b, l, d = h.shape
    nt = l // FFN_TILE
    tile = lambda w: pl.BlockSpec((None, FFN_TILE, w), lambda i, j: (i, j, 0))
    const_args = (g_ffn, w_up, conv_w, conv_b, w_down, g_ple, w_ple_gate, w_ple, g_final)
    return pl.pallas_call(
        _ffn_prompt_kernel,
        grid=(b, nt),
        in_specs=[tile(d), tile(PLE_DIM)] + [_const_spec(a.shape) for a in const_args],
        out_specs=[tile(d), pl.BlockSpec((None, CONV_W - 1, D_FF), lambda i, j: (i, 0, 0))],
        out_shape=[jax.ShapeDtypeStruct((b, l, d), F32),
                   jax.ShapeDtypeStruct((b, CONV_W - 1, D_FF), F32)],
        scratch_shapes=[pltpu.VMEM((FFN_TILE + CARRY_ROWS, D_FF), F32)],
        compiler_params=pltpu.CompilerParams(
            dimension_semantics=("arbitrary", "arbitrary"), vmem_limit_bytes=FFN_VMEM_LIMIT),
        name="ffn_prompt",
    )(h, p, *const_args)


def _proj_sample_kernel(x_ref, cos_ref, sin_ref, w_in_ref, g_mix_ref, lb_ref,
                        f_ref, kk_ref, q_ref, v_ref, kr_ref, qr_ref, vr_ref, gates_ref):
    xn = _rms(x_ref[...], g_mix_ref[...]).astype(BF16)
    u = _dot(xn, w_in_ref[:, C_HQ:C_HG])
    f, kk = _hgrn_forget(u[:, HG_W:2 * HG_W], lb_ref[...])
    f_ref[...] = f.T
    kk_ref[...] = kk.T
    q_ref[...] = _silu(u[:, 0:HG_W]).T
    v_ref[...] = u[:, 2 * HG_W:3 * HG_W].T
    u = _dot(xn, w_in_ref[:, C_RQ:C_RG])
    cos_full = cos_ref[...]
    sin_signed = sin_ref[...]
    qr_ref[...] = _rotary(u[:, 0:RET_W], cos_full, sin_signed)
    kr_ref[...] = _rotary(u[:, RET_W:2 * RET_W], cos_full, sin_signed) * (RET_DK ** -0.5)
    vr_ref[...] = u[:, 2 * RET_W:3 * RET_W]
    gates_ref[:, 0:HG_W] = _dot(xn, w_in_ref[:, C_HG:C_RQ])
    gates_ref[:, HG_W:HG_W + RET_W] = _dot(xn, w_in_ref[:, C_RG:C_GA])
    gates_ref[:, HG_W + RET_W:] = _dot(xn, w_in_ref[:, C_GA:C_END])


def _proj_sample(x, cos_full, sin_signed, w_in, g_mix, lb):
    n = x.shape[0]
    col = jax.ShapeDtypeStruct((HG_W, n), F32)
    row = jax.ShapeDtypeStruct((n, RET_W), F32)
    return pl.pallas_call(
        _proj_sample_kernel,
        out_shape=[col] * 4 + [row] * 3 + [jax.ShapeDtypeStruct((n, HG_W + RET_W + 2 * D_MODEL), F32)],
        compiler_params=pltpu.CompilerParams(vmem_limit_bytes=SAMPLE_VMEM_LIMIT),
        name="proj_sample",
    )(x, cos_full, sin_signed, w_in, g_mix, lb)


def _state_tokens_minor_kernel(s_ref, a_t_ref, k_t_ref, q_t_ref, v_t_ref, s_out_ref, o_t_ref):
    dk, dv, _ = s_ref.shape
    row0 = pl.program_id(0) * dk
    v_blk = v_t_ref[...]

    def body(k, acc):
        s_new = a_t_ref[pl.ds(row0 + k, 1), :] * s_ref[k] + k_t_ref[pl.ds(row0 + k, 1), :] * v_blk
        s_out_ref[k] = s_new
        return acc + q_t_ref[pl.ds(row0 + k, 1), :] * s_new

    o_t_ref[...] = lax.fori_loop(0, dk, body, jnp.zeros(v_blk.shape, F32), unroll=SUBLANES)


def _state_update_tokens_minor(state_t, layer, a_t, k_t, q_t, v_t, *, name):
    _, heads, dk, dv, n = state_t.shape
    full = lambda a: pl.BlockSpec(a.shape, lambda h: (0, 0))
    per_head = pl.BlockSpec((dv, n), lambda h: (h, 0))
    slab = lambda lead: pl.BlockSpec((None, None, dk, dv, n), lambda h: (lead, h, 0, 0, 0))
    return pl.pallas_call(
        _state_tokens_minor_kernel,
        grid=(heads,),
        in_specs=[slab(layer), full(a_t), full(k_t), full(q_t), per_head],
        out_specs=[slab(0), per_head],
        out_shape=[jax.ShapeDtypeStruct((1,) + state_t.shape[1:], F32), jax.ShapeDtypeStruct(v_t.shape, F32)],
        compiler_params=pltpu.CompilerParams(
            dimension_semantics=("arbitrary",), vmem_limit_bytes=SAMPLE_VMEM_LIMIT),
        name=name,
    )(state_t, a_t, k_t, q_t, v_t)


def _state_kernel(s_ref, a_ref, k_ref, q_ref, v_ref, s_out_ref, o_ref, *, dk, dv, kb):
    j = pl.program_id(0)
    n = s_ref.shape[0]

    @pl.when(j == 0)
    def _():
        o_ref[...] = jnp.zeros_like(o_ref)

    group0 = j * (kb // SUBLANES)
    head_row = pl.multiple_of((j * kb // dk) * dv, dv)
    v_blk = v_ref[pl.ds(head_row, dv), :]
    acc = jnp.zeros(v_blk.shape, F32)
    for g in range(kb // SUBLANES):
        rows = slice(SUBLANES * g, SUBLANES * (g + 1))
        s_t = s_ref[:, rows, :].reshape(n * SUBLANES, dv).T
        s_new = a_ref[pl.ds(group0 + g, 1), :] * s_t + k_ref[pl.ds(group0 + g, 1), :] * v_blk
        s_out_ref[:, rows, :] = s_new.T.reshape(n, SUBLANES, dv)
        acc = acc + q_ref[pl.ds(group0 + g, 1), :] * s_new
    o_ref[pl.ds(head_row, dv), :] += acc


def _state_update(state, layer, a_il, k_il, q_il, v_rep, *, kb, name):
    _, n, heads, dk, dv = state.shape
    per_head = dk // kb
    full = lambda a: pl.BlockSpec(a.shape, lambda j: (0, 0))
    slab = lambda lead: pl.BlockSpec((None, n, None, kb, dv), lambda j: (lead, 0, j // per_head, j % per_head, 0))
    return pl.pallas_call(
        functools.partial(_state_kernel, dk=dk, dv=dv, kb=kb),
        grid=(heads * per_head,),
        in_specs=[slab(layer), full(a_il), full(k_il), full(q_il), full(v_rep)],
        out_specs=[slab(0), full(v_rep)],
        out_shape=[jax.ShapeDtypeStruct((1,) + state.shape[1:], F32), jax.ShapeDtypeStruct(v_rep.shape, F32)],
        compiler_params=pltpu.CompilerParams(
            dimension_semantics=("arbitrary",), vmem_limit_bytes=SAMPLE_VMEM_LIMIT),
        name=name,
    )(state, a_il, k_il, q_il, v_rep)


def _lane_group_sum(o_ref, sel):
    o = o_ref[...]
    hi = o.astype(BF16)
    lo = (o - hi.astype(F32)).astype(BF16)
    return _dot_nt(sel, hi) + _dot_nt(sel, lo)


def _tail_sample_kernel(x_ref, o_hg_t_ref, o_ret_ref, sel_ref, gates_ref, conv0_ref, p_ref,
                        g_hg_ref, g_ret_ref, w_br_hg_ref, w_br_ret_ref, w_out_ref,
                        g_ffn_ref, w_up_ref, conv_w_ref, conv_b_ref, w_down_ref,
                        g_ple_ref, w_ple_gate_ref, w_ple_ref, g_final_ref,
                        y_ref, conv_out_ref):
    h = _merge_branches(
        x_ref[...], o_hg_t_ref[...].T, _lane_group_sum(o_ret_ref, sel_ref[...]),
        gates_ref[:, 0:HG_W], gates_ref[:, HG_W:HG_W + RET_W], gates_ref[:, HG_W + RET_W:],
        g_hg_ref[...], g_ret_ref[...],
        w_br_hg_ref[...], w_br_ret_ref[...], w_out_ref[...])
    xn = _rms(h, g_ffn_ref[...]).astype(BF16)
    a = _dot(xn, w_up_ref[:, 0:D_FF])
    gate_half = _dot(xn, w_up_ref[:, D_FF:2 * D_FF])
    cw = conv_w_ref[...]
    prev = conv0_ref[:, 1, :]
    ac = conv_b_ref[...] + conv0_ref[:, 0, :] * cw[0:1, :] + prev * cw[1:2, :] + a * cw[2:3, :]
    conv_out_ref[0, :, 0, :] = prev
    conv_out_ref[0, :, 1, :] = a
    y_ref[...] = _ffn_tail(h, (jax.nn.gelu(ac) * gate_half).astype(BF16), p_ref[...], w_down_ref[...], g_ple_ref[...],
                           w_ple_gate_ref[...], w_ple_ref[...], g_final_ref[...])


def _tail_sample(x, o_hg_t, o_ret, sel, gates, conv0, p, *weights):
    n = x.shape[0]
    return pl.pallas_call(
        _tail_sample_kernel,
        out_shape=[jax.ShapeDtypeStruct((n, D_MODEL), F32),
                   jax.ShapeDtypeStruct((1, n, CONV_W - 1, D_FF), F32)],
        compiler_params=pltpu.CompilerParams(vmem_limit_bytes=SAMPLE_VMEM_LIMIT),
        name="tail_sample",
    )(x, o_hg_t, o_ret, sel, gates, conv0, p, *weights)


def kernel(x_prompt, x_sample, state_hgrn, state_ret, state_conv, p_prompt, p_sample, lb_logits, w_in, g_mix, g_hg_out, g_ret_out, w_br_hg, w_br_ret, w_out, g_ffn, w_up, conv_w, conv_b, w_down, g_ple, w_ple, w_ple_gate, g_final):
    assert DEPTH == 1 and x_prompt.shape == (BATCH, SEQ, D_MODEL) and x_sample.shape == (DEC_BATCH, DEC_SEQ, D_MODEL)
    i = 0
    row = lambda a: a.reshape(1, -1).astype(F32)
    lb = jnp.cumsum(jax.nn.softmax(lb_logits.astype(F32), axis=0), axis=0)[i].reshape(1, HG_W)

    sel2, level = _level_tables()
    tri = np.tril(_block_mask(MIX_SUB, MIX_SUB, FAST_CHUNK, FAST_CHUNK))
    causal = np.tile(np.tril(np.ones((FAST_CHUNK, FAST_CHUNK), np.int32)), (1, HG_GROUP))
    hmask = (np.arange(HG_W)[None, :] // HG_FDIM) % HG_GROUP == np.arange(HG_GROUP)[:, None]
    consts = (
        jnp.asarray(sel2, BF16),
        jnp.asarray(np.concatenate([tri, tri], axis=1), BF16),
        jnp.asarray(hmask, F32),
        jnp.asarray(causal),
        jnp.asarray(np.tile(level, (1, HG_GROUP))),
        jnp.asarray(_block_mask(MXU_W, MXU_W, CHUNK, HG_FDIM), BF16),
        jnp.asarray(_block_mask(MXU_W, MXU_W, HG_IDIM, HG_FDIM), F32),
        jnp.asarray(_block_mask(MXU_W, MXU_W, RET_DK, RET_DV), F32),
    ) + _retention_tables()

    w_in_b = w_in[i].astype(BF16)
    mix_w = (row(g_hg_out[i]), row(g_ret_out[i]),
             w_br_hg[i].astype(BF16), w_br_ret[i].astype(BF16), w_out[i].astype(BF16))
    ffn_w = (row(g_ffn[i]), w_up[i].astype(BF16), conv_w[i].astype(F32), row(conv_b[i]), w_down[i].astype(BF16),
             row(g_ple[i]), w_ple_gate[i].astype(BF16), w_ple[i].astype(BF16), row(g_final))

    cos_p, sin_p = _rope_tables(0.0, SEQ)
    h_p, hg_state_p, ret_state_p = _mix_prompt(x_prompt, cos_p, sin_p, w_in_b, row(g_mix[i]), lb, consts, *mix_w)
    y_prompt, conv_p = _ffn_prompt(h_p, p_prompt[i], *ffn_w)

    n = DEC_BATCH
    xs = x_sample.reshape(n, D_MODEL)
    cos_s, sin_s = _rope_tables(float(PAST_LEN), DEC_SEQ)
    f_t, kk_t, q_t, v_t, kr, qr, vr, gates = _proj_sample(xs, cos_s, sin_s, w_in_b, row(g_mix[i]), lb)
    hg_new_t, o_hg_t = _state_update_tokens_minor(
        jnp.transpose(state_hgrn, (0, 2, 3, 4, 1)), i, f_t, kk_t, q_t, v_t, name="state_hgrn")
    hg_new = jnp.transpose(hg_new_t, (0, 4, 1, 2, 3))
    interleave = lambda a: a.reshape(n, -1, SUBLANES).transpose(1, 0, 2).reshape(-1, n * SUBLANES)
    repeat = lambda a: jnp.repeat(a.T, SUBLANES, axis=1)
    gamma = jnp.asarray(np.broadcast_to(np.repeat(np.exp(_log_gamma()), RET_DK // SUBLANES)[:, None],
                                        (RET_W // SUBLANES, n * SUBLANES)))
    ret_new, o_ret = _state_update(
        state_ret, i, gamma, interleave(kr), interleave(qr), repeat(vr), kb=RET_DK // 2, name="state_ret")
    sel = jnp.asarray(_block_mask(n, n * SUBLANES, 1, SUBLANES), BF16)
    y_sample, conv_s = _tail_sample(
        xs, o_hg_t, o_ret, sel, gates, state_conv[i], p_sample[i].reshape(n, PLE_DIM),
        *mix_w, *ffn_w)

    return (y_prompt, y_sample.reshape(n, DEC_SEQ, D_MODEL),
            hg_state_p, ret_state_p, conv_p[None],
            hg_new, ret_new,
            conv_s)
```

```python
import functools

import numpy as np
import jax
import jax.numpy as jnp
from jax import lax
from jax.experimental import pallas as pl
from jax.experimental.pallas import tpu as pltpu

D_MODEL = 1024
BATCH = 8
SEQ = 2048
DEPTH = 1
DEC_BATCH = 128
DEC_SEQ = 1
PAST_LEN = 16384
HG_HEADS = 8
HG_FDIM = 64
HG_IDIM = 64
HG_W = HG_HEADS * HG_FDIM
RET_HEADS = 4
RET_DK = 128
RET_DV = 128
RET_W = RET_HEADS * RET_DK
D_FF = 2816
CONV_W = 3
PLE_DIM = 256
CHUNK = 64
ROPE_BASE = 10000.0
EPS = 1e-6

C_HQ, C_HF, C_HI, C_HG = 0, 512, 1024, 1536
C_RQ, C_RK, C_RV, C_RG = 2048, 2560, 3072, 3584
C_GA, C_GB, C_END = 4096, 5120, 6144

N_LEVELS = 6
FAST_CHUNK = 128
FAST_DECAY_LIMIT = 80.0
HG_GROUP = 4
RET_GROUP = 2
MXU_W = 256
SUBLANES = 8
LANES = 128

MIX_TILE = 512
MIX_SUB = 256
FFN_TILE = 512
FFN_SUB = 512
CARRY_ROWS = 8
MIB = 1024 * 1024
MIX_VMEM_LIMIT = 48 * MIB
FFN_VMEM_LIMIT = 48 * MIB
SAMPLE_VMEM_LIMIT = 32 * MIB

F32 = jnp.float32
BF16 = jnp.bfloat16


def _level_tables():
    c = CHUNK
    t = np.arange(c)[:, None]
    r = np.arange(c)[None, :]
    blocks = [r <= t, r > t]
    for l in range(N_LEVELS):
        same = (t >> l) == (r >> l)
        upper = ((t >> l) & 1) == 1
        blocks.append(np.where(upper, same & (r <= t), same & (r > t)))
    sel = np.concatenate(blocks, axis=0).astype(np.float32)
    sel2 = np.concatenate([sel, sel], axis=1)
    s = r
    level = np.full((c, c), N_LEVELS + 1, np.int32)
    level[np.broadcast_to(s == t, (c, c))] = N_LEVELS
    x = t ^ s
    for l in range(N_LEVELS):
        hit = (s < t) & ((x >> l) == 1)
        level[hit] = l
    return sel2, level


def _block_mask(rows, cols, rblk, cblk):
    r = np.arange(rows)[:, None] // rblk
    c = np.arange(cols)[None, :] // cblk
    return (r == c).astype(np.float32)


def _rope_tables(pos0, length):
    d = RET_DK
    inv_freq = ROPE_BASE ** (-np.arange(0, d, 2, dtype=np.float64) / d)
    pos = pos0 + np.arange(length, dtype=np.float64)
    ang = pos[:, None] * inv_freq[None, :]
    cos, sin = np.cos(ang), np.sin(ang)
    cos_full = np.tile(np.concatenate([cos, cos], axis=-1), (1, RET_HEADS))
    sin_signed = np.tile(np.concatenate([-sin, sin], axis=-1), (1, RET_HEADS))
    return jnp.asarray(cos_full, F32), jnp.asarray(sin_signed, F32)


def _log_gamma():
    return np.log1p(-np.exp2(-5.0 - np.arange(RET_HEADS, dtype=np.float32))).astype(np.float32)


def _retention_tables():
    c = MIX_SUB
    log_gamma = _log_gamma()
    t = np.arange(c, dtype=np.float32)
    delta = t[:, None] - t[None, :]
    dmask = np.where(delta >= 0, np.exp(np.maximum(delta, 0.0) * log_gamma[:, None, None]), 0.0)
    q_decay = np.exp((t + 1.0)[None, :] * log_gamma[:, None])
    k_decay = np.exp((c - 1.0 - t)[None, :] * log_gamma[:, None])
    qdec = np.repeat(q_decay.T, RET_DK, axis=1)
    kdec = np.repeat(k_decay.T, RET_DK, axis=1)
    chunk_decay = np.exp(np.float32(c) * log_gamma)
    cd = np.repeat(chunk_decay, RET_DK).reshape(RET_HEADS // RET_GROUP, MXU_W, 1)
    cd = np.broadcast_to(cd, (RET_HEADS // RET_GROUP, MXU_W, MXU_W))
    return tuple(jnp.asarray(a, F32) for a in (dmask, qdec, kdec, cd))


def _rms(x, g):
    ms = jnp.mean(x * x, axis=-1, keepdims=True)
    return x * lax.rsqrt(ms + EPS) * g


def _silu(x):
    return x * jax.nn.sigmoid(x)


def _dot(a, b):
    return jnp.dot(a, b, preferred_element_type=F32)


def _dot_nt(a, b):
    return lax.dot_general(a, b, (((1,), (1,)), ((), ())), preferred_element_type=F32)


def _dot_tn(a, b):
    return lax.dot_general(a, b, (((0,), (0,)), ((), ())), preferred_element_type=F32)


def _rotary(x, cos_full, sin_signed):
    swapped = jnp.concatenate(
        [pltpu.roll(x[:, RET_DK * h:RET_DK * (h + 1)], RET_DK // 2, 1) for h in range(RET_HEADS)], axis=1)
    return x * cos_full + swapped * sin_signed


def _hgrn_forget(hf, lb):
    f = lb + (1.0 - lb) * jax.nn.sigmoid(hf)
    return f, 1.0 - f


def _head_mean_square(o, head_w):
    parts = []
    for g in range(o.shape[1] // LANES):
        sq = o[:, LANES * g:LANES * (g + 1)]
        sq = sq * sq
        if head_w == LANES:
            ms = jnp.broadcast_to(jnp.sum(sq, axis=-1, keepdims=True), sq.shape)
        else:
            low = lax.broadcasted_iota(jnp.int32, sq.shape, 1) < head_w
            lo = jnp.sum(jnp.where(low, sq, 0.0), axis=-1, keepdims=True)
            hi = jnp.sum(jnp.where(low, 0.0, sq), axis=-1, keepdims=True)
            ms = jnp.where(low, lo, hi)
        parts.append(ms * (1.0 / head_w))
    return jnp.concatenate(parts, axis=1)


def _merge_branches(x, o_hg, o_ret, u_hg, u_rg, u_gab, g_hg, g_ret, w_br_hg, w_br_ret, w_out):
    a_hg = o_hg * lax.rsqrt(_head_mean_square(o_hg, HG_IDIM) + EPS) * g_hg * _silu(u_hg)
    y_hg = _dot(a_hg.astype(BF16), w_br_hg)
    a_ret = o_ret * lax.rsqrt(_head_mean_square(o_ret, RET_DV) + EPS) * g_ret * _silu(u_rg)
    y_ret = _dot(a_ret.astype(BF16), w_br_ret)
    mixed = jax.nn.sigmoid(u_gab[:, :D_MODEL]) * y_hg + jax.nn.sigmoid(u_gab[:, D_MODEL:]) * y_ret
    return x + _dot(mixed.astype(BF16), w_out)


def _ffn_tail(h, mid, p, w_down, g_ple, w_ple_gate, w_ple, g_final):
    h = h + _dot(mid, w_down)
    embed = _dot(p.astype(BF16), w_ple)
    gate = jax.nn.sigmoid(_dot(_rms(h, g_ple).astype(BF16), w_ple_gate))
    h = h + gate * embed
    return _rms(h, g_final)


def _mix_prompt_kernel(x_ref, cos_ref, sin_ref, w_in_ref, g_mix_ref, lb_ref, sel_ref, tri_ref, hmask_ref, causal_ref, lvl_ref,
                       bd64_ref, bd64f_ref, bd128f_ref, dmask_ref, qdec_ref, kdec_ref, cd_ref,
                       g_hg_ref, g_ret_ref, w_br_hg_ref, w_br_ret_ref, w_out_ref,
                       h_ref, st_hg_out_ref, st_ret_out_ref,
                       xn_s, q_s, kk_s, lfh_s, lfl_s, v_s, vbd_s, qh_s, qi_s, kbd_s, ks_s, alast_s, o_hg_s, o_ret_s, st_hg, st_ret):
    t_idx = pl.program_id(1)
    n_hg_groups = HG_HEADS // HG_GROUP
    n_ret_groups = RET_HEADS // RET_GROUP

    @pl.when(t_idx == 0)
    def _():
        st_hg[...] = jnp.zeros_like(st_hg)
        st_ret[...] = jnp.zeros_like(st_ret)

    subs = [slice(i * MIX_SUB, (i + 1) * MIX_SUB) for i in range(MIX_TILE // MIX_SUB)]
    whole = [slice(0, MIX_TILE)]
    chunk_sums = []
    for r in whole:
        xn = _rms(x_ref[r, :], g_mix_ref[...]).astype(BF16)
        xn_s[r, :] = xn

        u = _dot(xn, w_in_ref[:, C_HQ:C_HG])
        f, kk = _hgrn_forget(u[:, HG_W:2 * HG_W], lb_ref[...])
        q_s[r, :] = _silu(u[:, 0:HG_W])
        kk_s[r, :] = kk
        lf = jnp.log(f)
        lf_hi = lf.astype(BF16)
        lfh_s[r, :] = lf_hi
        lfl_s[r, :] = (lf - lf_hi.astype(F32)).astype(BF16)
        v = u[:, 2 * HG_W:3 * HG_W]
        v_s[r, :] = v.astype(BF16)
        for h in range(HG_GROUP):
            vbd_s[h, r, :] = (v * hmask_ref[h:h + 1, :]).astype(BF16)
        chunk_sums += [jnp.sum(lf[c * CHUNK:(c + 1) * CHUNK, :], axis=0, keepdims=True)
                       for c in range(MIX_TILE // CHUNK)]

    mild = jnp.min(functools.reduce(jnp.minimum, chunk_sums)) >= -FAST_DECAY_LIMIT

    groups = [slice(MXU_W * g, MXU_W * (g + 1)) for g in range(n_hg_groups)]
    bd_tile = lambda ref, rows, cs: jnp.concatenate([ref[h, rows, cs] for h in range(HG_GROUP)], axis=0)

    def prepare_fast():
        per_sub = MIX_SUB // FAST_CHUNK
        rep = lambda rows: jnp.concatenate([jnp.broadcast_to(e, (FAST_CHUNK, HG_W)) for e in rows], axis=0)
        for i, r in enumerate(subs):
            lf2 = jnp.concatenate([lfh_s[r, :], lfl_s[r, :]], axis=0)
            cum = _dot(tri_ref[...], lf2)
            mids = [cum[c * FAST_CHUNK + FAST_CHUNK // 2 - 1:c * FAST_CHUNK + FAST_CHUNK // 2, :] for c in range(per_sub)]
            ends = [cum[(c + 1) * FAST_CHUNK - 1:(c + 1) * FAST_CHUNK, :] for c in range(per_sub)]
            mid = rep(mids)
            kk = kk_s[r, :]
            q = q_s[r, :]
            qh_s[r, :] = (q * jnp.exp(cum - mid)).astype(BF16)
            qi_s[r, :] = (q * jnp.exp(cum)).astype(BF16)
            k_mid = kk * jnp.exp(mid - cum)
            for h in range(HG_GROUP):
                kbd_s[h, r, :] = (k_mid * hmask_ref[h:h + 1, :]).astype(BF16)
            ks_s[r, :] = (kk * jnp.exp(rep(ends) - cum)).astype(BF16)
            for c, e in enumerate(ends):
                ci = i * per_sub + c
                alast_s[ci:ci + 1, :] = jnp.exp(e)

    def chunk_fast_products(c):
        rows = slice(c * FAST_CHUNK, (c + 1) * FAST_CHUNK)
        causal = causal_ref[...] != 0
        bd64f = bd64f_ref[...]
        scores = [jnp.where(causal, _dot_nt(qh_s[rows, cs], bd_tile(kbd_s, rows, cs)), 0.0).astype(BF16)
                  for cs in groups]
        upds = [_dot_tn(v_s[rows, cs], ks_s[rows, cs]) * bd64f for cs in groups]
        return scores, upds

    def chunk_fast(c, scores, upds):
        rows = slice(c * FAST_CHUNK, (c + 1) * FAST_CHUNK)
        o_parts = []
        for g, cs in enumerate(groups):
            st = st_hg[g]
            o_parts.append(_dot(scores[g], bd_tile(vbd_s, rows, cs)) + _dot_nt(qi_s[rows, cs], st.astype(BF16)))
            st_hg[g] = st * alast_s[c:c + 1, cs] + upds[g]
        o_hg_s[rows, :] = jnp.concatenate(o_parts, axis=1)

    def chunk_safe(c):
        rows = slice(c * CHUNK, (c + 1) * CHUNK)
        q = q_s[rows, :]
        kk = kk_s[rows, :]
        lf2 = jnp.concatenate([lfh_s[rows, :], lfl_s[rows, :]], axis=0)
        ex = jnp.exp(_dot(sel_ref[...], lf2))
        lvl = lvl_ref[...]
        bd64 = bd64_ref[...]
        bd64f = bd64f_ref[...]
        o_parts = []
        for g, cs in enumerate(groups):
            scores = jnp.zeros((CHUNK, MXU_W), F32)
            for l in range(N_LEVELS + 1):
                if l < N_LEVELS:
                    e = ex[CHUNK * (l + 2):CHUNK * (l + 3), cs]
                    ql = (q[:, cs] * e).astype(BF16)
                    kl = (kk[:, cs] * e).astype(BF16)
                else:
                    ql = q[:, cs].astype(BF16)
                    kl = kk[:, cs].astype(BF16)
                rhs = jnp.concatenate([kl] * HG_GROUP, axis=0) * bd64
                scores = jnp.where(lvl == l, _dot_nt(ql, rhs), scores)
            e_cum = ex[0:CHUNK, cs]
            e_suf = ex[CHUNK:2 * CHUNK, cs]
            st = st_hg[g]
            o_inter = _dot_nt((q[:, cs] * e_cum).astype(BF16), st.astype(BF16))
            o_parts.append(_dot(scores.astype(BF16), bd_tile(vbd_s, rows, cs)) + o_inter)
            upd = _dot_tn(v_s[rows, cs], (kk[:, cs] * e_suf).astype(BF16))
            st_hg[g] = st * e_cum[CHUNK - 1:CHUNK, :] + upd * bd64f
        o_hg_s[rows, :] = jnp.concatenate(o_parts, axis=1)

    def retention(r):
        xn = xn_s[r, :]
        u = _dot(xn, w_in_ref[:, C_RQ:C_RG])
        cos_full = cos_ref[r, :]
        sin_signed = sin_ref[r, :]
        qr = _rotary(u[:, 0:RET_W], cos_full, sin_signed)
        kr = _rotary(u[:, RET_W:2 * RET_W], cos_full, sin_signed) * (RET_DK ** -0.5)
        vr = u[:, 2 * RET_W:3 * RET_W].astype(BF16)
        qrb = qr.astype(BF16)
        krb = kr.astype(BF16)
        o_heads = []
        for h in range(RET_HEADS):
            hs = slice(RET_DK * h, RET_DK * (h + 1))
            scores = _dot_nt(qrb[:, hs], krb[:, hs]) * dmask_ref[h]
            o_heads.append(_dot(scores.astype(BF16), vr[:, hs]))
        qd = (qr * qdec_ref[...]).astype(BF16)
        kd = (kr * kdec_ref[...]).astype(BF16)
        bd128f = bd128f_ref[...]
        o_inter = []
        for g in range(n_ret_groups):
            cs = slice(MXU_W * g, MXU_W * (g + 1))
            st = st_ret[g]
            o_inter.append(_dot(qd[:, cs], st.astype(BF16)))
            st_ret[g] = cd_ref[g] * st + _dot_tn(kd[:, cs], vr[:, cs]) * bd128f
        o_ret_s[r, :] = jnp.concatenate(o_heads, axis=1) + jnp.concatenate(o_inter, axis=1)

    def recurrences(fast):
        if fast:
            prepare_fast()
        for r in subs:
            retention(r)
        if fast:
            products = [chunk_fast_products(c) for c in range(MIX_TILE // FAST_CHUNK)]
            for c, (scores, upds) in enumerate(products):
                chunk_fast(c, scores, upds)
        else:
            for c in range(MIX_TILE // CHUNK):
                chunk_safe(c)

    pl.when(mild)(functools.partial(recurrences, True))
    pl.when(jnp.logical_not(mild))(functools.partial(recurrences, False))

    for r in whole:
        xn = xn_s[r, :]
        h_ref[r, :] = _merge_branches(
            x_ref[r, :], o_hg_s[r, :], o_ret_s[r, :],
            _dot(xn, w_in_ref[:, C_HG:C_RQ]), _dot(xn, w_in_ref[:, C_RG:C_GA]), _dot(xn, w_in_ref[:, C_GA:C_END]),
            g_hg_ref[...], g_ret_ref[...],
            w_br_hg_ref[...], w_br_ret_ref[...], w_out_ref[...])

    @pl.when(t_idx == pl.num_programs(1) - 1)
    def _():
        for g in range(n_hg_groups):
            st_t = st_hg[g].T
            for h in range(HG_GROUP):
                blk = slice(HG_FDIM * h, HG_FDIM * (h + 1))
                st_hg_out_ref[HG_GROUP * g + h] = st_t[blk, blk]
        for g in range(n_ret_groups):
            for h in range(RET_GROUP):
                blk = slice(RET_DK * h, RET_DK * (h + 1))
                st_ret_out_ref[RET_GROUP * g + h] = st_ret[g, blk, blk]


def _const_spec(shape):
    zeros = (0,) * len(shape)
    return pl.BlockSpec(shape, lambda *_: zeros, pipeline_mode=pl.Buffered(1))


def _mix_prompt(x, cos_full, sin_signed, w_in, g_mix, lb, consts, g_hg, g_ret,
                w_br_hg, w_br_ret, w_out):
    b, l, d = x.shape
    nt = l // MIX_TILE
    tile = lambda w: pl.BlockSpec((None, MIX_TILE, w), lambda i, j: (i, j, 0))
    table = lambda w: pl.BlockSpec((MIX_TILE, w), lambda i, j: (j, 0))
    n_hg_groups = HG_HEADS // HG_GROUP
    n_ret_groups = RET_HEADS // RET_GROUP
    const_args = (w_in, g_mix, lb) + tuple(consts) + (g_hg, g_ret, w_br_hg, w_br_ret, w_out)
    state_spec = lambda h, dk, dv: pl.BlockSpec((None, None, h, dk, dv), lambda i, j: (0, i, 0, 0, 0))
    return pl.pallas_call(
        _mix_prompt_kernel,
        grid=(b, nt),
        in_specs=[tile(d), table(RET_W), table(RET_W)] + [_const_spec(a.shape) for a in const_args],
        out_specs=[tile(d), state_spec(HG_HEADS, HG_FDIM, HG_IDIM), state_spec(RET_HEADS, RET_DK, RET_DV)],
        out_shape=[jax.ShapeDtypeStruct((b, l, d), F32),
                   jax.ShapeDtypeStruct((1, b, HG_HEADS, HG_FDIM, HG_IDIM), F32),
                   jax.ShapeDtypeStruct((1, b, RET_HEADS, RET_DK, RET_DV), F32)],
        scratch_shapes=[
            pltpu.VMEM((MIX_TILE, d), BF16),
            pltpu.VMEM((MIX_TILE, HG_W), F32),
            pltpu.VMEM((MIX_TILE, HG_W), F32),
            pltpu.VMEM((MIX_TILE, HG_W), BF16),
            pltpu.VMEM((MIX_TILE, HG_W), BF16),
            pltpu.VMEM((MIX_TILE, HG_W), BF16),
            pltpu.VMEM((HG_GROUP, MIX_TILE, HG_W), BF16),
            pltpu.VMEM((MIX_TILE, HG_W), BF16),
            pltpu.VMEM((MIX_TILE, HG_W), BF16),
            pltpu.VMEM((HG_GROUP, MIX_TILE, HG_W), BF16),
            pltpu.VMEM((MIX_TILE, HG_W), BF16),
            pltpu.VMEM((max(MIX_TILE // CHUNK, SUBLANES), HG_W), F32),
            pltpu.VMEM((MIX_TILE, HG_W), F32),
            pltpu.VMEM((MIX_TILE, RET_W), F32),
            pltpu.VMEM((n_hg_groups, MXU_W, MXU_W), F32),
            pltpu.VMEM((n_ret_groups, MXU_W, MXU_W), F32),
        ],
        compiler_params=pltpu.CompilerParams(
            dimension_semantics=("arbitrary", "arbitrary"), vmem_limit_bytes=MIX_VMEM_LIMIT),
        name="mix_prompt",
    )(x, cos_full, sin_signed, *const_args)


def _ffn_prompt_kernel(h_ref, p_ref, g_ffn_ref, w_up_ref, conv_w_ref, conv_b_ref, w_down_ref,
                       g_ple_ref, w_ple_gate_ref, w_ple_ref, g_final_ref,
                       y_ref, conv_out_ref, a_ext):
    t_idx = pl.program_id(1)
    tl = FFN_TILE

    @pl.when(t_idx == 0)
    def _():
        a_ext[0:CARRY_ROWS, :] = jnp.zeros((CARRY_ROWS, D_FF), F32)

    @pl.when(t_idx > 0)
    def _():
        a_ext[0:CARRY_ROWS, :] = a_ext[tl:tl + CARRY_ROWS, :]

    subs = [slice(i * FFN_SUB, (i + 1) * FFN_SUB) for i in range(tl // FFN_SUB)]
    xns = []
    for r in subs:
        xns.append(_rms(h_ref[r, :], g_ffn_ref[...]).astype(BF16))
        a_ext[CARRY_ROWS + r.start:CARRY_ROWS + r.stop, :] = _dot(xns[-1], w_up_ref[:, 0:D_FF])
    cw = conv_w_ref[...]
    mids = []
    for r, xn in zip(subs, xns):
        ac = conv_b_ref[...]
        for j in range(CONV_W):
            off = CARRY_ROWS - (CONV_W - 1) + j
            ac = ac + a_ext[off + r.start:off + r.stop, :] * cw[j:j + 1, :]
        mids.append((jax.nn.gelu(ac) * _dot(xn, w_up_ref[:, D_FF:2 * D_FF])).astype(BF16))
    for r, mid in zip(subs, mids):
        y_ref[r, :] = _ffn_tail(h_ref[r, :], mid, p_ref[r, :], w_down_ref[...], g_ple_ref[...],
                                w_ple_gate_ref[...], w_ple_ref[...], g_final_ref[...])

    @pl.when(t_idx == pl.num_programs(1) - 1)
    def _():
        conv_out_ref[...] = a_ext[CARRY_ROWS + tl - (CONV_W - 1):CARRY_ROWS + tl, :]


def _ffn_prompt(h, p, g_ffn, w_up, conv_w, conv_b, w_down, g_ple, w_ple_gate, w_ple, g_final):
    b, l, d = h.shape
    nt = l // FFN_TILE
    tile = lambda w: pl.BlockSpec((None, FFN_TILE, w), lambda i, j: (i, j, 0))
    const_args = (g_ffn, w_up, conv_w, conv_b, w_down, g_ple, w_ple_gate, w_ple, g_final)
    return pl.pallas_call(
        _ffn_prompt_kernel,
        grid=(b, nt),
        in_specs=[tile(d), tile(PLE_DIM)] + [_const_spec(a.shape) for a in const_args],
        out_specs=[tile(d), pl.BlockSpec((None, CONV_W - 1, D_FF), lambda i, j: (i, 0, 0))],
        out_shape=[jax.ShapeDtypeStruct((b, l, d), F32),
                   jax.ShapeDtypeStruct((b, CONV_W - 1, D_FF), F32)],
        scratch_shapes=[pltpu.VMEM((FFN_TILE + CARRY_ROWS, D_FF), F32)],
        compiler_params=pltpu.CompilerParams(
            dimension_semantics=("arbitrary", "arbitrary"), vmem_limit_bytes=FFN_VMEM_LIMIT),
        name="ffn_prompt",
    )(h, p, *const_args)


def _proj_sample_kernel(x_ref, cos_ref, sin_ref, w_in_ref, g_mix_ref, lb_ref,
                        f_ref, kk_ref, q_ref, v_ref, kr_ref, qr_ref, vr_ref, gates_ref):
    xn = _rms(x_ref[...], g_mix_ref[...]).astype(BF16)
    u = _dot(xn, w_in_ref[:, C_HQ:C_HG])
    f, kk = _hgrn_forget(u[:, HG_W:2 * HG_W], lb_ref[...])
    f_ref[...] = f.T
    kk_ref[...] = kk.T
    q_ref[...] = _silu(u[:, 0:HG_W]).T
    v_ref[...] = u[:, 2 * HG_W:3 * HG_W].T
    u = _dot(xn, w_in_ref[:, C_RQ:C_RG])
    cos_full = cos_ref[...]
    sin_signed = sin_ref[...]
    qr_ref[...] = _rotary(u[:, 0:RET_W], cos_full, sin_signed)
    kr_ref[...] = _rotary(u[:, RET_W:2 * RET_W], cos_full, sin_signed) * (RET_DK ** -0.5)
    vr_ref[...] = u[:, 2 * RET_W:3 * RET_W]
    gates_ref[:, 0:HG_W] = _dot(xn, w_in_ref[:, C_HG:C_RQ])
    gates_ref[:, HG_W:HG_W + RET_W] = _dot(xn, w_in_ref[:, C_RG:C_GA])
    gates_ref[:, HG_W + RET_W:] = _dot(xn, w_in_ref[:, C_GA:C_END])


def _proj_sample(x, cos_full, sin_signed, w_in, g_mix, lb):
    n = x.shape[0]
    col = jax.ShapeDtypeStruct((HG_W, n), F32)
    row = jax.ShapeDtypeStruct((n, RET_W), F32)
    return pl.pallas_call(
        _proj_sample_kernel,
        out_shape=[col] * 4 + [row] * 3 + [jax.ShapeDtypeStruct((n, HG_W + RET_W + 2 * D_MODEL), F32)],
        compiler_params=pltpu.CompilerParams(vmem_limit_bytes=SAMPLE_VMEM_LIMIT),
        name="proj_sample",
    )(x, cos_full, sin_signed, w_in, g_mix, lb)


def _state_tokens_minor_kernel(s_ref, a_t_ref, k_t_ref, q_t_ref, v_t_ref, s_out_ref, o_t_ref):
    dk, dv, _ = s_ref.shape
    row0 = pl.program_id(0) * dk
    v_blk = v_t_ref[...]

    def body(k, acc):
        s_new = a_t_ref[pl.ds(row0 + k, 1), :] * s_ref[k] + k_t_ref[pl.ds(row0 + k, 1), :] * v_blk
        s_out_ref[k] = s_new
        return acc + q_t_ref[pl.ds(row0 + k, 1), :] * s_new

    o_t_ref[...] = lax.fori_loop(0, dk, body, jnp.zeros(v_blk.shape, F32), unroll=SUBLANES)


def _state_update_tokens_minor(state_t, layer, a_t, k_t, q_t, v_t, *, name):
    _, heads, dk, dv, n = state_t.shape
    full = lambda a: pl.BlockSpec(a.shape, lambda h: (0, 0))
    per_head = pl.BlockSpec((dv, n), lambda h: (h, 0))
    slab = lambda lead: pl.BlockSpec((None, None, dk, dv, n), lambda h: (lead, h, 0, 0, 0))
    return pl.pallas_call(
        _state_tokens_minor_kernel,
        grid=(heads,),
        in_specs=[slab(layer), full(a_t), full(k_t), full(q_t), per_head],
        out_specs=[slab(0), per_head],
        out_shape=[jax.ShapeDtypeStruct((1,) + state_t.shape[1:], F32), jax.ShapeDtypeStruct(v_t.shape, F32)],
        compiler_params=pltpu.CompilerParams(
            dimension_semantics=("arbitrary",), vmem_limit_bytes=SAMPLE_VMEM_LIMIT),
        name=name,
    )(state_t, a_t, k_t, q_t, v_t)


def _state_kernel(s_ref, a_ref, k_ref, q_ref, v_ref, s_out_ref, o_ref, *, dk, dv, kb):
    j = pl.program_id(0)
    n = s_ref.shape[0]

    @pl.when(j == 0)
    def _():
        o_ref[...] = jnp.zeros_like(o_ref)

    group0 = j * (kb // SUBLANES)
    head_row = pl.multiple_of((j * kb // dk) * dv, dv)
    v_blk = v_ref[pl.ds(head_row, dv), :]
    acc = jnp.zeros(v_blk.shape, F32)
    for g in range(kb // SUBLANES):
        rows = slice(SUBLANES * g, SUBLANES * (g + 1))
        s_t = s_ref[:, rows, :].reshape(n * SUBLANES, dv).T
        s_new = a_ref[pl.ds(group0 + g, 1), :] * s_t + k_ref[pl.ds(group0 + g, 1), :] * v_blk
        s_out_ref[:, rows, :] = s_new.T.reshape(n, SUBLANES, dv)
        acc = acc + q_ref[pl.ds(group0 + g, 1), :] * s_new
    o_ref[pl.ds(head_row, dv), :] += acc


def _state_update(state, layer, a_il, k_il, q_il, v_rep, *, kb, name):
    _, n, heads, dk, dv = state.shape
    per_head = dk // kb
    full = lambda a: pl.BlockSpec(a.shape, lambda j: (0, 0))
    slab = lambda lead: pl.BlockSpec((None, n, None, kb, dv), lambda j: (lead, 0, j // per_head, j % per_head, 0))
    return pl.pallas_call(
        functools.partial(_state_kernel, dk=dk, dv=dv, kb=kb),
        grid=(heads * per_head,),
        in_specs=[slab(layer), full(a_il), full(k_il), full(q_il), full(v_rep)],
        out_specs=[slab(0), full(v_rep)],
        out_shape=[jax.ShapeDtypeStruct((1,) + state.shape[1:], F32), jax.ShapeDtypeStruct(v_rep.shape, F32)],
        compiler_params=pltpu.CompilerParams(
            dimension_semantics=("arbitrary",), vmem_limit_bytes=SAMPLE_VMEM_LIMIT),
        name=name,
    )(state, a_il, k_il, q_il, v_rep)


def _lane_group_sum(o_ref, sel):
    o = o_ref[...]
    hi = o.astype(BF16)
    lo = (o - hi.astype(F32)).astype(BF16)
    return _dot_nt(sel, hi) + _dot_nt(sel, lo)


def _tail_sample_kernel(x_ref, o_hg_t_ref, o_ret_ref, sel_ref, gates_ref, conv0_ref, p_ref,
                        g_hg_ref, g_ret_ref, w_br_hg_ref, w_br_ret_ref, w_out_ref,
                        g_ffn_ref, w_up_ref, conv_w_ref, conv_b_ref, w_down_ref,
                        g_ple_ref, w_ple_gate_ref, w_ple_ref, g_final_ref,
                        y_ref, conv_out_ref):
    h = _merge_branches(
        x_ref[...], o_hg_t_ref[...].T, _lane_group_sum(o_ret_ref, sel_ref[...]),
        gates_ref[:, 0:HG_W], gates_ref[:, HG_W:HG_W + RET_W], gates_ref[:, HG_W + RET_W:],
        g_hg_ref[...], g_ret_ref[...],
        w_br_hg_ref[...], w_br_ret_ref[...], w_out_ref[...])
    xn = _rms(h, g_ffn_ref[...]).astype(BF16)
    a = _dot(xn, w_up_ref[:, 0:D_FF])
    gate_half = _dot(xn, w_up_ref[:, D_FF:2 * D_FF])
    cw = conv_w_ref[...]
    prev = conv0_ref[:, 1, :]
    ac = conv_b_ref[...] + conv0_ref[:, 0, :] * cw[0:1, :] + prev * cw[1:2, :] + a * cw[2:3, :]
    conv_out_ref[0, :, 0, :] = prev
    conv_out_ref[0, :, 1, :] = a
    y_ref[...] = _ffn_tail(h, (jax.nn.gelu(ac) * gate_half).astype(BF16), p_ref[...], w_down_ref[...], g_ple_ref[...],
                           w_ple_gate_ref[...], w_ple_ref[...], g_final_ref[...])


def _tail_sample(x, o_hg_t, o_ret, sel, gates, conv0, p, *weights):
    n = x.shape[0]
    return pl.pallas_call(
        _tail_sample_kernel,
        out_shape=[jax.ShapeDtypeStruct((n, D_MODEL), F32),
                   jax.ShapeDtypeStruct((1, n, CONV_W - 1, D_FF), F32)],
        compiler_params=pltpu.CompilerParams(vmem_limit_bytes=SAMPLE_VMEM_LIMIT),
        name="tail_sample",
    )(x, o_hg_t, o_ret, sel, gates, conv0, p, *weights)


def kernel(x_prompt, x_sample, state_hgrn, state_ret, state_conv, p_prompt, p_sample, lb_logits, w_in, g_mix, g_hg_out, g_ret_out, w_br_hg, w_br_ret, w_out, g_ffn, w_up, conv_w, conv_b, w_down, g_ple, w_ple, w_ple_gate, g_final):
    assert DEPTH == 1 and x_prompt.shape == (BATCH, SEQ, D_MODEL) and x_sample.shape == (DEC_BATCH, DEC_SEQ, D_MODEL)
    i = 0
    row = lambda a: a.reshape(1, -1).astype(F32)
    lb = jnp.cumsum(jax.nn.softmax(lb_logits.astype(F32), axis=0), axis=0)[i].reshape(1, HG_W)

    sel2, level = _level_tables()
    tri = np.tril(_block_mask(MIX_SUB, MIX_SUB, FAST_CHUNK, FAST_CHUNK))
    causal = np.tile(np.tril(np.ones((FAST_CHUNK, FAST_CHUNK), np.int32)), (1, HG_GROUP))
    hmask = (np.arange(HG_W)[None, :] // HG_FDIM) % HG_GROUP == np.arange(HG_GROUP)[:, None]
    consts = (
        jnp.asarray(sel2, BF16),
        jnp.asarray(np.concatenate([tri, tri], axis=1), BF16),
        jnp.asarray(hmask, F32),
        jnp.asarray(causal),
        jnp.asarray(np.tile(level, (1, HG_GROUP))),
        jnp.asarray(_block_mask(MXU_W, MXU_W, CHUNK, HG_FDIM), BF16),
        jnp.asarray(_block_mask(MXU_W, MXU_W, HG_IDIM, HG_FDIM), F32),
        jnp.asarray(_block_mask(MXU_W, MXU_W, RET_DK, RET_DV), F32),
    ) + _retention_tables()

    w_in_b = w_in[i].astype(BF16)
    mix_w = (row(g_hg_out[i]), row(g_ret_out[i]),
             w_br_hg[i].astype(BF16), w_br_ret[i].astype(BF16), w_out[i].astype(BF16))
    ffn_w = (row(g_ffn[i]), w_up[i].astype(BF16), conv_w[i].astype(F32), row(conv_b[i]), w_down[i].astype(BF16),
             row(g_ple[i]), w_ple_gate[i].astype(BF16), w_ple[i].astype(BF16), row(g_final))

    cos_p, sin_p = _rope_tables(0.0, SEQ)
    h_p, hg_state_p, ret_state_p = _mix_prompt(x_prompt, cos_p, sin_p, w_in_b, row(g_mix[i]), lb, consts, *mix_w)
    y_prompt, conv_p = _ffn_prompt(h_p, p_prompt[i], *ffn_w)

    n = DEC_BATCH
    xs = x_sample.reshape(n, D_MODEL)
    cos_s, sin_s = _rope_tables(float(PAST_LEN), DEC_SEQ)
    f_t, kk_t, q_t, v_t, kr, qr, vr, gates = _proj_sample(xs, cos_s, sin_s, w_in_b, row(g_mix[i]), lb)
    hg_new_t, o_hg_t = _state_update_tokens_minor(
        jnp.transpose(state_hgrn, (0, 2, 3, 4, 1)), i, f_t, kk_t, q_t, v_t, name="state_hgrn")
    hg_new = jnp.transpose(hg_new_t, (0, 4, 1, 2, 3))
    interleave = lambda a: a.reshape(n, -1, SUBLANES).transpose(1, 0, 2).reshape(-1, n * SUBLANES)
    repeat = lambda a: jnp.repeat(a.T, SUBLANES, axis=1)
    gamma = jnp.asarray(np.broadcast_to(np.repeat(np.exp(_log_gamma()), RET_DK // SUBLANES)[:, None],
                                        (RET_W // SUBLANES, n * SUBLANES)))
    ret_new, o_ret = _state_update(
        state_ret, i, gamma, interleave(kr), interleave(qr), repeat(vr), kb=RET_DK // 2, name="state_ret")
    sel = jnp.asarray(_block_mask(n, n * SUBLANES, 1, SUBLANES), BF16)
    y_sample, conv_s = _tail_sample(
        xs, o_hg_t, o_ret, sel, gates, state_conv[i], p_sample[i].reshape(n, PLE_DIM),
        *mix_w, *ffn_w)

    return (y_prompt, y_sample.reshape(n, DEC_SEQ, D_MODEL),
            hg_state_p, ret_state_p, conv_p[None],
            hg_new, ret_new,
            conv_s)
```

```python
import functools

import numpy as np
import jax
import jax.numpy as jnp
from jax import lax
from jax.experimental import pallas as pl
from jax.experimental.pallas import tpu as pltpu

D_MODEL = 1024
BATCH = 8
SEQ = 2048
DEPTH = 1
DEC_BATCH = 128
DEC_SEQ = 1
PAST_LEN = 16384
HG_HEADS = 8
HG_FDIM = 64
HG_IDIM = 64
HG_W = HG_HEADS * HG_FDIM
RET_HEADS = 4
RET_DK = 128
RET_DV = 128
RET_W = RET_HEADS * RET_DK
D_FF = 2816
CONV_W = 3
PLE_DIM = 256
CHUNK = 64
ROPE_BASE = 10000.0
EPS = 1e-6

C_HQ, C_HF, C_HI, C_HG = 0, 512, 1024, 1536
C_RQ, C_RK, C_RV, C_RG = 2048, 2560, 3072, 3584
C_GA, C_GB, C_END = 4096, 5120, 6144

N_LEVELS = 6
FAST_CHUNK = 128
FAST_DECAY_LIMIT = 80.0
HG_GROUP = 4
RET_GROUP = 2
MXU_W = 256
SUBLANES = 8
LANES = 128

MIX_TILE = 512
MIX_SUB = 256
FFN_TILE = 512
FFN_SUB = 512
FFN_COL_BLOCKS = 2
CARRY_ROWS = 8
MIB = 1024 * 1024
MIX_VMEM_LIMIT = 48 * MIB
FFN_VMEM_LIMIT = 48 * MIB
SAMPLE_VMEM_LIMIT = 32 * MIB

F32 = jnp.float32
BF16 = jnp.bfloat16


def _level_tables():
    c = CHUNK
    t = np.arange(c)[:, None]
    r = np.arange(c)[None, :]
    blocks = [r <= t, r > t]
    for l in range(N_LEVELS):
        same = (t >> l) == (r >> l)
        upper = ((t >> l) & 1) == 1
        blocks.append(np.where(upper, same & (r <= t), same & (r > t)))
    sel = np.concatenate(blocks, axis=0).astype(np.float32)
    sel2 = np.concatenate([sel, sel], axis=1)
    s = r
    level = np.full((c, c), N_LEVELS + 1, np.int32)
    level[np.broadcast_to(s == t, (c, c))] = N_LEVELS
    x = t ^ s
    for l in range(N_LEVELS):
        hit = (s < t) & ((x >> l) == 1)
        level[hit] = l
    return sel2, level


def _block_mask(rows, cols, rblk, cblk):
    r = np.arange(rows)[:, None] // rblk
    c = np.arange(cols)[None, :] // cblk
    return (r == c).astype(np.float32)


def _rope_tables(pos0, length):
    d = RET_DK
    inv_freq = ROPE_BASE ** (-np.arange(0, d, 2, dtype=np.float64) / d)
    pos = pos0 + np.arange(length, dtype=np.float64)
    ang = pos[:, None] * inv_freq[None, :]
    cos, sin = np.cos(ang), np.sin(ang)
    cos_full = np.tile(np.concatenate([cos, cos], axis=-1), (1, RET_HEADS))
    sin_signed = np.tile(np.concatenate([-sin, sin], axis=-1), (1, RET_HEADS))
    return jnp.asarray(cos_full, F32), jnp.asarray(sin_signed, F32)


def _log_gamma():
    return np.log1p(-np.exp2(-5.0 - np.arange(RET_HEADS, dtype=np.float32))).astype(np.float32)


def _retention_tables():
    c = MIX_SUB
    log_gamma = _log_gamma()
    t = np.arange(c, dtype=np.float32)
    delta = t[:, None] - t[None, :]
    dmask = np.where(delta >= 0, np.exp(np.maximum(delta, 0.0) * log_gamma[:, None, None]), 0.0)
    q_decay = np.exp((t + 1.0)[None, :] * log_gamma[:, None])
    k_decay = np.exp((c - 1.0 - t)[None, :] * log_gamma[:, None])
    qdec = np.repeat(q_decay.T, RET_DK, axis=1)
    kdec = np.repeat(k_decay.T, RET_DK, axis=1)
    chunk_decay = np.exp(np.float32(c) * log_gamma)
    cd = np.repeat(chunk_decay, RET_DK).reshape(RET_HEADS // RET_GROUP, MXU_W, 1)
    cd = np.broadcast_to(cd, (RET_HEADS // RET_GROUP, MXU_W, MXU_W))
    return tuple(jnp.asarray(a, F32) for a in (dmask, qdec, kdec, cd))


def _rms(x, g):
    ms = jnp.mean(x * x, axis=-1, keepdims=True)
    return x * lax.rsqrt(ms + EPS) * g


def _silu(x):
    return x * jax.nn.sigmoid(x)


def _dot(a, b):
    return jnp.dot(a, b, preferred_element_type=F32)


def _dot_nt(a, b):
    return lax.dot_general(a, b, (((1,), (1,)), ((), ())), preferred_element_type=F32)


def _dot_tn(a, b):
    return lax.dot_general(a, b, (((0,), (0,)), ((), ())), preferred_element_type=F32)


def _rotary(x, cos_full, sin_signed):
    swapped = jnp.concatenate(
        [pltpu.roll(x[:, RET_DK * h:RET_DK * (h + 1)], RET_DK // 2, 1) for h in range(RET_HEADS)], axis=1)
    return x * cos_full + swapped * sin_signed


def _hgrn_forget(hf, lb):
    f = lb + (1.0 - lb) * jax.nn.sigmoid(hf)
    return f, 1.0 - f


def _head_mean_square(o, head_w):
    parts = []
    for g in range(o.shape[1] // LANES):
        sq = o[:, LANES * g:LANES * (g + 1)]
        sq = sq * sq
        if head_w == LANES:
            ms = jnp.broadcast_to(jnp.sum(sq, axis=-1, keepdims=True), sq.shape)
        else:
            low = lax.broadcasted_iota(jnp.int32, sq.shape, 1) < head_w
            lo = jnp.sum(jnp.where(low, sq, 0.0), axis=-1, keepdims=True)
            hi = jnp.sum(jnp.where(low, 0.0, sq), axis=-1, keepdims=True)
            ms = jnp.where(low, lo, hi)
        parts.append(ms * (1.0 / head_w))
    return jnp.concatenate(parts, axis=1)


def _merge_branches(x, o_hg, o_ret, u_hg, u_rg, u_gab, g_hg, g_ret, w_br_hg, w_br_ret, w_out):
    a_hg = o_hg * lax.rsqrt(_head_mean_square(o_hg, HG_IDIM) + EPS) * g_hg * _silu(u_hg)
    y_hg = _dot(a_hg.astype(BF16), w_br_hg)
    a_ret = o_ret * lax.rsqrt(_head_mean_square(o_ret, RET_DV) + EPS) * g_ret * _silu(u_rg)
    y_ret = _dot(a_ret.astype(BF16), w_br_ret)
    mixed = jax.nn.sigmoid(u_gab[:, :D_MODEL]) * y_hg + jax.nn.sigmoid(u_gab[:, D_MODEL:]) * y_ret
    return x + _dot(mixed.astype(BF16), w_out)


def _ffn_tail(h, down, p, g_ple, w_ple_gate, w_ple, g_final):
    h = h + down
    embed = _dot(p.astype(BF16), w_ple)
    gate = jax.nn.sigmoid(_dot(_rms(h, g_ple).astype(BF16), w_ple_gate))
    h = h + gate * embed
    return _rms(h, g_final)


def _mix_prompt_kernel(x_ref, cos_ref, sin_ref, w_in_ref, g_mix_ref, lb_ref, sel_ref, tri_ref, hmask_ref, causal_ref, lvl_ref,
                       bd64_ref, bd64f_ref, bd128f_ref, dmask_ref, qdec_ref, kdec_ref, cd_ref,
                       g_hg_ref, g_ret_ref, w_br_hg_ref, w_br_ret_ref, w_out_ref,
                       h_ref, st_hg_out_ref, st_ret_out_ref,
                       xn_s, q_s, kk_s, lfh_s, lfl_s, v_s, vbd_s, qh_s, qi_s, kbd_s, ks_s, alast_s, o_hg_s, o_ret_s, st_hg, st_ret):
    t_idx = pl.program_id(1)
    n_hg_groups = HG_HEADS // HG_GROUP
    n_ret_groups = RET_HEADS // RET_GROUP

    @pl.when(t_idx == 0)
    def _():
        st_hg[...] = jnp.zeros_like(st_hg)
        st_ret[...] = jnp.zeros_like(st_ret)

    subs = [slice(i * MIX_SUB, (i + 1) * MIX_SUB) for i in range(MIX_TILE // MIX_SUB)]
    whole = [slice(0, MIX_TILE)]
    chunk_sums = []
    for r in whole:
        xn = _rms(x_ref[r, :], g_mix_ref[...]).astype(BF16)
        xn_s[r, :] = xn

        u = _dot(xn, w_in_ref[:, C_HQ:C_HG])
        f, kk = _hgrn_forget(u[:, HG_W:2 * HG_W], lb_ref[...])
        q_s[r, :] = _silu(u[:, 0:HG_W])
        kk_s[r, :] = kk
        lf = jnp.log(f)
        lf_hi = lf.astype(BF16)
        lfh_s[r, :] = lf_hi
        lfl_s[r, :] = (lf - lf_hi.astype(F32)).astype(BF16)
        v = u[:, 2 * HG_W:3 * HG_W]
        v_s[r, :] = v.astype(BF16)
        for h in range(HG_GROUP):
            vbd_s[h, r, :] = (v * hmask_ref[h:h + 1, :]).astype(BF16)
        chunk_sums += [jnp.sum(lf[c * CHUNK:(c + 1) * CHUNK, :], axis=0, keepdims=True)
                       for c in range(MIX_TILE // CHUNK)]

    mild = jnp.min(functools.reduce(jnp.minimum, chunk_sums)) >= -FAST_DECAY_LIMIT

    groups = [slice(MXU_W * g, MXU_W * (g + 1)) for g in range(n_hg_groups)]
    bd_tile = lambda ref, rows, cs: jnp.concatenate([ref[h, rows, cs] for h in range(HG_GROUP)], axis=0)

    def prepare_fast():
        per_sub = MIX_SUB // FAST_CHUNK
        rep = lambda rows: jnp.concatenate([jnp.broadcast_to(e, (FAST_CHUNK, HG_W)) for e in rows], axis=0)
        for i, r in enumerate(subs):
            lf2 = jnp.concatenate([lfh_s[r, :], lfl_s[r, :]], axis=0)
            cum = _dot(tri_ref[...], lf2)
            mids = [cum[c * FAST_CHUNK + FAST_CHUNK // 2 - 1:c * FAST_CHUNK + FAST_CHUNK // 2, :] for c in range(per_sub)]
            ends = [cum[(c + 1) * FAST_CHUNK - 1:(c + 1) * FAST_CHUNK, :] for c in range(per_sub)]
            mid = rep(mids)
            kk = kk_s[r, :]
            q = q_s[r, :]
            qh_s[r, :] = (q * jnp.exp(cum - mid)).astype(BF16)
            qi_s[r, :] = (q * jnp.exp(cum)).astype(BF16)
            k_mid = kk * jnp.exp(mid - cum)
            for h in range(HG_GROUP):
                kbd_s[h, r, :] = (k_mid * hmask_ref[h:h + 1, :]).astype(BF16)
            ks_s[r, :] = (kk * jnp.exp(rep(ends) - cum)).astype(BF16)
            for c, e in enumerate(ends):
                ci = i * per_sub + c
                alast_s[ci:ci + 1, :] = jnp.exp(e)

    def chunk_fast_products(c):
        rows = slice(c * FAST_CHUNK, (c + 1) * FAST_CHUNK)
        causal = causal_ref[...] != 0
        bd64f = bd64f_ref[...]
        scores = [jnp.where(causal, _dot_nt(qh_s[rows, cs], bd_tile(kbd_s, rows, cs)), 0.0).astype(BF16)
                  for cs in groups]
        upds = [_dot_tn(v_s[rows, cs], ks_s[rows, cs]) * bd64f for cs in groups]
        return scores, upds

    def chunk_fast(c, scores, upds):
        rows = slice(c * FAST_CHUNK, (c + 1) * FAST_CHUNK)
        o_parts = []
        for g, cs in enumerate(groups):
            st = st_hg[g]
            o_parts.append(_dot(scores[g], bd_tile(vbd_s, rows, cs)) + _dot_nt(qi_s[rows, cs], st.astype(BF16)))
            st_hg[g] = st * alast_s[c:c + 1, cs] + upds[g]
        o_hg_s[rows, :] = jnp.concatenate(o_parts, axis=1)

    def chunk_safe(c):
        rows = slice(c * CHUNK, (c + 1) * CHUNK)
        q = q_s[rows, :]
        kk = kk_s[rows, :]
        lf2 = jnp.concatenate([lfh_s[rows, :], lfl_s[rows, :]], axis=0)
        ex = jnp.exp(_dot(sel_ref[...], lf2))
        lvl = lvl_ref[...]
        bd64 = bd64_ref[...]
        bd64f = bd64f_ref[...]
        o_parts = []
        for g, cs in enumerate(groups):
            scores = jnp.zeros((CHUNK, MXU_W), F32)
            for l in range(N_LEVELS + 1):
                if l < N_LEVELS:
                    e = ex[CHUNK * (l + 2):CHUNK * (l + 3), cs]
                    ql = (q[:, cs] * e).astype(BF16)
                    kl = (kk[:, cs] * e).astype(BF16)
                else:
                    ql = q[:, cs].astype(BF16)
                    kl = kk[:, cs].astype(BF16)
                rhs = jnp.concatenate([kl] * HG_GROUP, axis=0) * bd64
                scores = jnp.where(lvl == l, _dot_nt(ql, rhs), scores)
            e_cum = ex[0:CHUNK, cs]
            e_suf = ex[CHUNK:2 * CHUNK, cs]
            st = st_hg[g]
            o_inter = _dot_nt((q[:, cs] * e_cum).astype(BF16), st.astype(BF16))
            o_parts.append(_dot(scores.astype(BF16), bd_tile(vbd_s, rows, cs)) + o_inter)
            upd = _dot_tn(v_s[rows, cs], (kk[:, cs] * e_suf).astype(BF16))
            st_hg[g] = st * e_cum[CHUNK - 1:CHUNK, :] + upd * bd64f
        o_hg_s[rows, :] = jnp.concatenate(o_parts, axis=1)

    def retention(r):
        xn = xn_s[r, :]
        u = _dot(xn, w_in_ref[:, C_RQ:C_RG])
        cos_full = cos_ref[r, :]
        sin_signed = sin_ref[r, :]
        qr = _rotary(u[:, 0:RET_W], cos_full, sin_signed)
        kr = _rotary(u[:, RET_W:2 * RET_W], cos_full, sin_signed) * (RET_DK ** -0.5)
        vr = u[:, 2 * RET_W:3 * RET_W].astype(BF16)
        qrb = qr.astype(BF16)
        krb = kr.astype(BF16)
        o_heads = []
        for h in range(RET_HEADS):
            hs = slice(RET_DK * h, RET_DK * (h + 1))
            scores = _dot_nt(qrb[:, hs], krb[:, hs]) * dmask_ref[h]
            o_heads.append(_dot(scores.astype(BF16), vr[:, hs]))
        qd = (qr * qdec_ref[...]).astype(BF16)
        kd = (kr * kdec_ref[...]).astype(BF16)
        bd128f = bd128f_ref[...]
        o_inter = []
        for g in range(n_ret_groups):
            cs = slice(MXU_W * g, MXU_W * (g + 1))
            st = st_ret[g]
            o_inter.append(_dot(qd[:, cs], st.astype(BF16)))
            st_ret[g] = cd_ref[g] * st + _dot_tn(kd[:, cs], vr[:, cs]) * bd128f
        o_ret_s[r, :] = jnp.concatenate(o_heads, axis=1) + jnp.concatenate(o_inter, axis=1)

    def recurrences(fast):
        if fast:
            prepare_fast()
        for r in subs:
            retention(r)
        if fast:
            products = [chunk_fast_products(c) for c in range(MIX_TILE // FAST_CHUNK)]
            for c, (scores, upds) in enumerate(products):
                chunk_fast(c, scores, upds)
        else:
            for c in range(MIX_TILE // CHUNK):
                chunk_safe(c)

    pl.when(mild)(functools.partial(recurrences, True))
    pl.when(jnp.logical_not(mild))(functools.partial(recurrences, False))

    for r in whole:
        xn = xn_s[r, :]
        h_ref[r, :] = _merge_branches(
            x_ref[r, :], o_hg_s[r, :], o_ret_s[r, :],
            _dot(xn, w_in_ref[:, C_HG:C_RQ]), _dot(xn, w_in_ref[:, C_RG:C_GA]), _dot(xn, w_in_ref[:, C_GA:C_END]),
            g_hg_ref[...], g_ret_ref[...],
            w_br_hg_ref[...], w_br_ret_ref[...], w_out_ref[...])

    @pl.when(t_idx == pl.num_programs(1) - 1)
    def _():
        for g in range(n_hg_groups):
            st_t = st_hg[g].T
            for h in range(HG_GROUP):
                blk = slice(HG_FDIM * h, HG_FDIM * (h + 1))
                st_hg_out_ref[HG_GROUP * g + h] = st_t[blk, blk]
        for g in range(n_ret_groups):
            for h in range(RET_GROUP):
                blk = slice(RET_DK * h, RET_DK * (h + 1))
                st_ret_out_ref[RET_GROUP * g + h] = st_ret[g, blk, blk]


def _const_spec(shape):
    zeros = (0,) * len(shape)
    return pl.BlockSpec(shape, lambda *_: zeros, pipeline_mode=pl.Buffered(1))


def _mix_prompt(x, cos_full, sin_signed, w_in, g_mix, lb, consts, g_hg, g_ret,
                w_br_hg, w_br_ret, w_out):
    b, l, d = x.shape
    nt = l // MIX_TILE
    tile = lambda w: pl.BlockSpec((None, MIX_TILE, w), lambda i, j: (i, j, 0))
    table = lambda w: pl.BlockSpec((MIX_TILE, w), lambda i, j: (j, 0))
    n_hg_groups = HG_HEADS // HG_GROUP
    n_ret_groups = RET_HEADS // RET_GROUP
    const_args = (w_in, g_mix, lb) + tuple(consts) + (g_hg, g_ret, w_br_hg, w_br_ret, w_out)
    state_spec = lambda h, dk, dv: pl.BlockSpec((None, None, h, dk, dv), lambda i, j: (0, i, 0, 0, 0))
    return pl.pallas_call(
        _mix_prompt_kernel,
        grid=(b, nt),
        in_specs=[tile(d), table(RET_W), table(RET_W)] + [_const_spec(a.shape) for a in const_args],
        out_specs=[tile(d), state_spec(HG_HEADS, HG_FDIM, HG_IDIM), state_spec(RET_HEADS, RET_DK, RET_DV)],
        out_shape=[jax.ShapeDtypeStruct((b, l, d), F32),
                   jax.ShapeDtypeStruct((1, b, HG_HEADS, HG_FDIM, HG_IDIM), F32),
                   jax.ShapeDtypeStruct((1, b, RET_HEADS, RET_DK, RET_DV), F32)],
        scratch_shapes=[
            pltpu.VMEM((MIX_TILE, d), BF16),
            pltpu.VMEM((MIX_TILE, HG_W), F32),
            pltpu.VMEM((MIX_TILE, HG_W), F32),
            pltpu.VMEM((MIX_TILE, HG_W), BF16),
            pltpu.VMEM((MIX_TILE, HG_W), BF16),
            pltpu.VMEM((MIX_TILE, HG_W), BF16),
            pltpu.VMEM((HG_GROUP, MIX_TILE, HG_W), BF16),
            pltpu.VMEM((MIX_TILE, HG_W), BF16),
            pltpu.VMEM((MIX_TILE, HG_W), BF16),
            pltpu.VMEM((HG_GROUP, MIX_TILE, HG_W), BF16),
            pltpu.VMEM((MIX_TILE, HG_W), BF16),
            pltpu.VMEM((max(MIX_TILE // CHUNK, SUBLANES), HG_W), F32),
            pltpu.VMEM((MIX_TILE, HG_W), F32),
            pltpu.VMEM((MIX_TILE, RET_W), F32),
            pltpu.VMEM((n_hg_groups, MXU_W, MXU_W), F32),
            pltpu.VMEM((n_ret_groups, MXU_W, MXU_W), F32),
        ],
        compiler_params=pltpu.CompilerParams(
            dimension_semantics=("arbitrary", "arbitrary"), vmem_limit_bytes=MIX_VMEM_LIMIT),
        name="mix_prompt",
    )(x, cos_full, sin_signed, *const_args)


def _ffn_prompt_kernel(h_ref, p_ref, g_ffn_ref, w_up_ref, conv_w_ref, conv_b_ref, w_down_ref,
                       g_ple_ref, w_ple_gate_ref, w_ple_ref, g_final_ref,
                       y_ref, conv_out_ref, a_ext):
    t_idx = pl.program_id(1)
    tl = FFN_TILE

    @pl.when(t_idx == 0)
    def _():
        a_ext[0:CARRY_ROWS, :] = jnp.zeros((CARRY_ROWS, D_FF), F32)

    @pl.when(t_idx > 0)
    def _():
        a_ext[0:CARRY_ROWS, :] = a_ext[tl:tl + CARRY_ROWS, :]

    subs = [slice(i * FFN_SUB, (i + 1) * FFN_SUB) for i in range(tl // FFN_SUB)]
    xns = []
    for r in subs:
        xns.append(_rms(h_ref[r, :], g_ffn_ref[...]).astype(BF16))
        a_ext[CARRY_ROWS + r.start:CARRY_ROWS + r.stop, :] = _dot(xns[-1], w_up_ref[:, 0:D_FF])
    cw = conv_w_ref[...]
    cb = conv_b_ref[...]
    blk = D_FF // FFN_COL_BLOCKS
    for r, xn in zip(subs, xns):
        down = None
        for c in range(FFN_COL_BLOCKS):
            cs = slice(c * blk, (c + 1) * blk)
            ac = cb[:, cs]
            for j in range(CONV_W):
                off = CARRY_ROWS - (CONV_W - 1) + j
                ac = ac + a_ext[off + r.start:off + r.stop, cs] * cw[j:j + 1, cs]
            gate_half = _dot(xn, w_up_ref[:, D_FF + cs.start:D_FF + cs.stop])
            part = _dot((jax.nn.gelu(ac) * gate_half).astype(BF16), w_down_ref[cs, :])
            down = part if down is None else down + part
        y_ref[r, :] = _ffn_tail(h_ref[r, :], down, p_ref[r, :], g_ple_ref[...],
                                w_ple_gate_ref[...], w_ple_ref[...], g_final_ref[...])

    @pl.when(t_idx == pl.num_programs(1) - 1)
    def _():
        conv_out_ref[...] = a_ext[CARRY_ROWS + tl - (CONV_W - 1):CARRY_ROWS + tl, :]


def _ffn_prompt(h, p, g_ffn, w_up, conv_w, conv_b, w_down, g_ple, w_ple_gate, w_ple, g_final):
    b, l, d = h.shape
    nt = l // FFN_TILE
    tile = lambda w: pl.BlockSpec((None, FFN_TILE, w), lambda i, j: (i, j, 0))
    const_args = (g_ffn, w_up, conv_w, conv_b, w_down, g_ple, w_ple_gate, w_ple, g_final)
    return pl.pallas_call(
        _ffn_prompt_kernel,
        grid=(b, nt),
        in_specs=[tile(d), tile(PLE_DIM)] + [_const_spec(a.shape) for a in const_args],
        out_specs=[tile(d), pl.BlockSpec((None, CONV_W - 1, D_FF), lambda i, j: (i, 0, 0))],
        out_shape=[jax.ShapeDtypeStruct((b, l, d), F32),
                   jax.ShapeDtypeStruct((b, CONV_W - 1, D_FF), F32)],
        scratch_shapes=[pltpu.VMEM((FFN_TILE + CARRY_ROWS, D_FF), F32)],
        compiler_params=pltpu.CompilerParams(
            dimension_semantics=("arbitrary", "arbitrary"), vmem_limit_bytes=FFN_VMEM_LIMIT),
        name="ffn_prompt",
    )(h, p, *const_args)


def _proj_sample_kernel(x_ref, cos_ref, sin_ref, w_in_ref, g_mix_ref, lb_ref,
                        f_ref, kk_ref, q_ref, v_ref, kr_ref, qr_ref, vr_ref, gates_ref):
    xn = _rms(x_ref[...], g_mix_ref[...]).astype(BF16)
    u = _dot(xn, w_in_ref[:, C_HQ:C_HG])
    f, kk = _hgrn_forget(u[:, HG_W:2 * HG_W], lb_ref[...])
    f_ref[...] = f.T
    kk_ref[...] = kk.T
    q_ref[...] = _silu(u[:, 0:HG_W]).T
    v_ref[...] = u[:, 2 * HG_W:3 * HG_W].T
    u = _dot(xn, w_in_ref[:, C_RQ:C_RG])
    cos_full = cos_ref[...]
    sin_signed = sin_ref[...]
    qr_ref[...] = _rotary(u[:, 0:RET_W], cos_full, sin_signed)
    kr_ref[...] = _rotary(u[:, RET_W:2 * RET_W], cos_full, sin_signed) * (RET_DK ** -0.5)
    vr_ref[...] = u[:, 2 * RET_W:3 * RET_W]
    gates_ref[:, 0:HG_W] = _dot(xn, w_in_ref[:, C_HG:C_RQ])
    gates_ref[:, HG_W:HG_W + RET_W] = _dot(xn, w_in_ref[:, C_RG:C_GA])
    gates_ref[:, HG_W + RET_W:] = _dot(xn, w_in_ref[:, C_GA:C_END])


def _proj_sample(x, cos_full, sin_signed, w_in, g_mix, lb):
    n = x.shape[0]
    col = jax.ShapeDtypeStruct((HG_W, n), F32)
    row = jax.ShapeDtypeStruct((n, RET_W), F32)
    return pl.pallas_call(
        _proj_sample_kernel,
        out_shape=[col] * 4 + [row] * 3 + [jax.ShapeDtypeStruct((n, HG_W + RET_W + 2 * D_MODEL), F32)],
        compiler_params=pltpu.CompilerParams(vmem_limit_bytes=SAMPLE_VMEM_LIMIT),
        name="proj_sample",
    )(x, cos_full, sin_signed, w_in, g_mix, lb)


def _state_tokens_minor_kernel(s_ref, a_t_ref, k_t_ref, q_t_ref, v_t_ref, s_out_ref, o_t_ref):
    dk, dv, _ = s_ref.shape
    row0 = pl.program_id(0) * dk
    v_blk = v_t_ref[...]

    def body(k, acc):
        s_new = a_t_ref[pl.ds(row0 + k, 1), :] * s_ref[k] + k_t_ref[pl.ds(row0 + k, 1), :] * v_blk
        s_out_ref[k] = s_new
        return acc + q_t_ref[pl.ds(row0 + k, 1), :] * s_new

    o_t_ref[...] = lax.fori_loop(0, dk, body, jnp.zeros(v_blk.shape, F32), unroll=SUBLANES)


def _state_update_tokens_minor(state_t, layer, a_t, k_t, q_t, v_t, *, name):
    _, heads, dk, dv, n = state_t.shape
    full = lambda a: pl.BlockSpec(a.shape, lambda h: (0, 0))
    per_head = pl.BlockSpec((dv, n), lambda h: (h, 0))
    slab = lambda lead, **kw: pl.BlockSpec((None, None, dk, dv, n), lambda h: (lead, h, 0, 0, 0), **kw)
    return pl.pallas_call(
        _state_tokens_minor_kernel,
        grid=(heads,),
        in_specs=[slab(layer), full(a_t), full(k_t), full(q_t), per_head],
        out_specs=[slab(0), per_head],
        out_shape=[jax.ShapeDtypeStruct((1,) + state_t.shape[1:], F32), jax.ShapeDtypeStruct(v_t.shape, F32)],
        compiler_params=pltpu.CompilerParams(
            dimension_semantics=("arbitrary",), vmem_limit_bytes=SAMPLE_VMEM_LIMIT),
        name=name,
    )(state_t, a_t, k_t, q_t, v_t)


def _state_kernel(s_ref, a_ref, k_ref, q_ref, v_ref, s_out_ref, o_ref, *, dk, dv, kb):
    j = pl.program_id(0)
    n = s_ref.shape[0]

    @pl.when(j == 0)
    def _():
        o_ref[...] = jnp.zeros_like(o_ref)

    group0 = j * (kb // SUBLANES)
    head_row = pl.multiple_of((j * kb // dk) * dv, dv)
    v_blk = v_ref[pl.ds(head_row, dv), :]
    acc = jnp.zeros(v_blk.shape, F32)
    for g in range(kb // SUBLANES):
        rows = slice(SUBLANES * g, SUBLANES * (g + 1))
        s_t = s_ref[:, rows, :].reshape(n * SUBLANES, dv).T
        s_new = a_ref[pl.ds(group0 + g, 1), :] * s_t + k_ref[pl.ds(group0 + g, 1), :] * v_blk
        s_out_ref[:, rows, :] = s_new.T.reshape(n, SUBLANES, dv)
        acc = acc + q_ref[pl.ds(group0 + g, 1), :] * s_new
    o_ref[pl.ds(head_row, dv), :] += acc


def _state_update(state, layer, a_il, k_il, q_il, v_rep, *, kb, name):
    _, n, heads, dk, dv = state.shape
    per_head = dk // kb
    full = lambda a: pl.BlockSpec(a.shape, lambda j: (0, 0))
    slab = lambda lead, **kw: pl.BlockSpec((None, n, None, kb, dv),
                                           lambda j: (lead, 0, j // per_head, j % per_head, 0), **kw)
    return pl.pallas_call(
        functools.partial(_state_kernel, dk=dk, dv=dv, kb=kb),
        grid=(heads * per_head,),
        in_specs=[slab(layer), full(a_il), full(k_il), full(q_il), full(v_rep)],
        out_specs=[slab(0), full(v_rep)],
        out_shape=[jax.ShapeDtypeStruct((1,) + state.shape[1:], F32), jax.ShapeDtypeStruct(v_rep.shape, F32)],
        compiler_params=pltpu.CompilerParams(
            dimension_semantics=("arbitrary",), vmem_limit_bytes=SAMPLE_VMEM_LIMIT),
        name=name,
    )(state, a_il, k_il, q_il, v_rep)


def _lane_group_sum(o_ref, sel):
    o = o_ref[...]
    hi = o.astype(BF16)
    lo = (o - hi.astype(F32)).astype(BF16)
    return _dot_nt(sel, hi) + _dot_nt(sel, lo)


def _tail_sample_kernel(x_ref, o_hg_t_ref, o_ret_ref, sel_ref, gates_ref, conv0_ref, p_ref,
                        g_hg_ref, g_ret_ref, w_br_hg_ref, w_br_ret_ref, w_out_ref,
                        g_ffn_ref, w_up_ref, conv_w_ref, conv_b_ref, w_down_ref,
                        g_ple_ref, w_ple_gate_ref, w_ple_ref, g_final_ref,
                        y_ref, conv_out_ref):
    h = _merge_branches(
        x_ref[...], o_hg_t_ref[...].T, _lane_group_sum(o_ret_ref, sel_ref[...]),
        gates_ref[:, 0:HG_W], gates_ref[:, HG_W:HG_W + RET_W], gates_ref[:, HG_W + RET_W:],
        g_hg_ref[...], g_ret_ref[...],
        w_br_hg_ref[...], w_br_ret_ref[...], w_out_ref[...])
    xn = _rms(h, g_ffn_ref[...]).astype(BF16)
    a = _dot(xn, w_up_ref[:, 0:D_FF])
    gate_half = _dot(xn, w_up_ref[:, D_FF:2 * D_FF])
    cw = conv_w_ref[...]
    prev = conv0_ref[:, 1, :]
    ac = conv_b_ref[...] + conv0_ref[:, 0, :] * cw[0:1, :] + prev * cw[1:2, :] + a * cw[2:3, :]
    conv_out_ref[0, :, 0, :] = prev
    conv_out_ref[0, :, 1, :] = a
    y_ref[...] = _ffn_tail(h, _dot((jax.nn.gelu(ac) * gate_half).astype(BF16), w_down_ref[...]), p_ref[...], g_ple_ref[...],
                           w_ple_gate_ref[...], w_ple_ref[...], g_final_ref[...])


def _tail_sample(x, o_hg_t, o_ret, sel, gates, conv0, p, *weights):
    n = x.shape[0]
    return pl.pallas_call(
        _tail_sample_kernel,
        out_shape=[jax.ShapeDtypeStruct((n, D_MODEL), F32),
                   jax.ShapeDtypeStruct((1, n, CONV_W - 1, D_FF), F32)],
        compiler_params=pltpu.CompilerParams(vmem_limit_bytes=SAMPLE_VMEM_LIMIT),
        name="tail_sample",
    )(x, o_hg_t, o_ret, sel, gates, conv0, p, *weights)


def kernel(x_prompt, x_sample, state_hgrn, state_ret, state_conv, p_prompt, p_sample, lb_logits, w_in, g_mix, g_hg_out, g_ret_out, w_br_hg, w_br_ret, w_out, g_ffn, w_up, conv_w, conv_b, w_down, g_ple, w_ple, w_ple_gate, g_final):
    assert DEPTH == 1 and x_prompt.shape == (BATCH, SEQ, D_MODEL) and x_sample.shape == (DEC_BATCH, DEC_SEQ, D_MODEL)
    i = 0
    row = lambda a: a.reshape(1, -1).astype(F32)
    lb = jnp.cumsum(jax.nn.softmax(lb_logits.astype(F32), axis=0), axis=0)[i].reshape(1, HG_W)

    sel2, level = _level_tables()
    tri = np.tril(_block_mask(MIX_SUB, MIX_SUB, FAST_CHUNK, FAST_CHUNK))
    causal = np.tile(np.tril(np.ones((FAST_CHUNK, FAST_CHUNK), np.int32)), (1, HG_GROUP))
    hmask = (np.arange(HG_W)[None, :] // HG_FDIM) % HG_GROUP == np.arange(HG_GROUP)[:, None]
    consts = (
        jnp.asarray(sel2, BF16),
        jnp.asarray(np.concatenate([tri, tri], axis=1), BF16),
        jnp.asarray(hmask, F32),
        jnp.asarray(causal),
        jnp.asarray(np.tile(level, (1, HG_GROUP))),
        jnp.asarray(_block_mask(MXU_W, MXU_W, CHUNK, HG_FDIM), BF16),
        jnp.asarray(_block_mask(MXU_W, MXU_W, HG_IDIM, HG_FDIM), F32),
        jnp.asarray(_block_mask(MXU_W, MXU_W, RET_DK, RET_DV), F32),
    ) + _retention_tables()

    w_in_b = w_in[i].astype(BF16)
    mix_w = (row(g_hg_out[i]), row(g_ret_out[i]),
             w_br_hg[i].astype(BF16), w_br_ret[i].astype(BF16), w_out[i].astype(BF16))
    ffn_w = (row(g_ffn[i]), w_up[i].astype(BF16), conv_w[i].astype(F32), row(conv_b[i]), w_down[i].astype(BF16),
             row(g_ple[i]), w_ple_gate[i].astype(BF16), w_ple[i].astype(BF16), row(g_final))

    cos_p, sin_p = _rope_tables(0.0, SEQ)
    h_p, hg_state_p, ret_state_p = _mix_prompt(x_prompt, cos_p, sin_p, w_in_b, row(g_mix[i]), lb, consts, *mix_w)
    y_prompt, conv_p = _ffn_prompt(h_p, p_prompt[i], *ffn_w)

    n = DEC_BATCH
    xs = x_sample.reshape(n, D_MODEL)
    cos_s, sin_s = _rope_tables(float(PAST_LEN), DEC_SEQ)
    f_t, kk_t, q_t, v_t, kr, qr, vr, gates = _proj_sample(xs, cos_s, sin_s, w_in_b, row(g_mix[i]), lb)
    hg_new_t, o_hg_t = _state_update_tokens_minor(
        jnp.transpose(state_hgrn, (0, 2, 3, 4, 1)), i, f_t, kk_t, q_t, v_t, name="state_hgrn")
    hg_new = jnp.transpose(hg_new_t, (0, 4, 1, 2, 3))
    interleave = lambda a: a.reshape(n, -1, SUBLANES).transpose(1, 0, 2).reshape(-1, n * SUBLANES)
    repeat = lambda a: jnp.repeat(a.T, SUBLANES, axis=1)
    gamma = jnp.asarray(np.broadcast_to(np.repeat(np.exp(_log_gamma()), RET_DK // SUBLANES)[:, None],
                                        (RET_W // SUBLANES, n * SUBLANES)))
    ret_new, o_ret = _state_update(
        state_ret, i, gamma, interleave(kr), interleave(qr), repeat(vr), kb=RET_DK // 2, name="state_ret")
    sel = jnp.asarray(_block_mask(n, n * SUBLANES, 1, SUBLANES), BF16)
    y_sample, conv_s = _tail_sample(
        xs, o_hg_t, o_ret, sel, gates, state_conv[i], p_sample[i].reshape(n, PLE_DIM),
        *mix_w, *ffn_w)

    return (y_prompt, y_sample.reshape(n, DEC_SEQ, D_MODEL),
            hg_state_p, ret_state_p, conv_p[None],
            hg_new, ret_new,
            conv_s)
```
